```python
import math
import jax
import jax.numpy as jnp
from jax import lax
import numpy as np

D_MODEL = 2048
BATCH = 8
SEQ = 2048
DEPTH = 1

NORM_EPS = 1e-6
CONV_WIDTH = 4
PLE_DIM = 256

GDN_HEAD_DIM = 128
GDN_V_HEADS = D_MODEL // 128
GDN_QK_HEADS = GDN_V_HEADS // 2
GDN_KEY_DIM = GDN_QK_HEADS * GDN_HEAD_DIM
GDN_VALUE_DIM = GDN_V_HEADS * GDN_HEAD_DIM
GDN_CONV_CH = 2 * GDN_KEY_DIM + GDN_VALUE_DIM
GDN_CHUNK = 64

SSM_HEAD_DIM = 64
SSM_HEADS = D_MODEL // 64
SSM_INNER = SSM_HEADS * SSM_HEAD_DIM
SSM_GROUPS = 4
SSM_STATE = 128
SSM_CONV_CH = SSM_INNER + 2 * SSM_GROUPS * SSM_STATE
SSM_CHUNK = 128

D_MIX = GDN_VALUE_DIM + SSM_INNER
IN_SIZES = (GDN_CONV_CH, GDN_VALUE_DIM, GDN_V_HEADS, GDN_V_HEADS, SSM_CONV_CH, SSM_INNER, SSM_HEADS)
IN_COLS = GDN_CONV_CH + GDN_VALUE_DIM + 2 * GDN_V_HEADS + SSM_CONV_CH + SSM_INNER + SSM_HEADS

N_EXPERTS = 32
TOP_K = 4
EXPERT_FF = D_MODEL
SWIGLU_ALPHA = 1.702
SWIGLU_LIMIT = 7.0
MOE_BLOCK = 128

kernel_name = 'hybrid_gdn_ssd_moe_ple_layer'


def rms_norm(x, w):
    xf = x.astype(jnp.float32)
    y = xf * lax.rsqrt(jnp.mean(xf * xf, axis=-1, keepdims=True) + NORM_EPS)
    return (y * w.astype(jnp.float32)).astype(x.dtype)


def l2_normalize(x):
    return x * lax.rsqrt(jnp.sum(x * x, axis=-1, keepdims=True) + NORM_EPS)


def split_columns(t, sizes):
    points = []
    acc = 0
    for s in sizes[:-1]:
        acc += s
        points.append(acc)
    return jnp.split(t, points, axis=-1)


def causal_depthwise_conv(x, w):
    width = w.shape[0]
    seq = x.shape[1]
    xp = jnp.pad(x, ((0, 0), (width - 1, 0), (0, 0)))
    out = xp[:, 0:seq] * w[0]
    for k in range(1, width):
        out = out + xp[:, k:k + seq] * w[k]
    return out


def to_chunks(t, chunk):
    bsz, seq = t.shape[0], t.shape[1]
    t = t.reshape((bsz, seq // chunk, chunk) + t.shape[2:])
    return jnp.moveaxis(t, 3, 2)


def chunked_gated_delta_rule(q, k, v, g, beta):
    bsz, seq, heads, dk = k.shape
    dv = v.shape[-1]
    c = GDN_CHUNK
    q, k, v = to_chunks(q, c), to_chunks(k, c), to_chunks(v, c)
    g, beta = to_chunks(g, c), to_chunks(beta, c)
    g = jnp.cumsum(g, axis=-1)
    causal = jnp.tril(jnp.ones((c, c), dtype=bool))
    strict = jnp.tril(jnp.ones((c, c), dtype=bool), -1)
    decay = jnp.exp(jnp.where(causal, g[..., :, None] - g[..., None, :], -jnp.inf))
    k_beta = k * beta[..., None]
    v_beta = v * beta[..., None]
    lmat = jnp.where(strict, jnp.einsum('bnhcd,bnhsd->bnhcs', k_beta, k) * decay, 0.0)
    tmat = lmat + jnp.eye(c, dtype=lmat.dtype)
    rhs = jnp.concatenate([v_beta, k_beta * jnp.exp(g)[..., None]], axis=-1)
    sol = lax.linalg.triangular_solve(tmat, rhs, left_side=True, lower=True, unit_diagonal=True)
    u, w = sol[..., :dv], sol[..., dv:]
    attn = jnp.einsum('bnhcd,bnhsd->bnhcs', q, k) * decay
    q_dec = q * jnp.exp(g)[..., None]
    k_dec = k * jnp.exp(g[..., -1:] - g)[..., None]
    g_last = jnp.exp(g[..., -1])

    def step(state, inp):
        attn_c, u_c, w_c, qd_c, kd_c, gl_c = inp
        v_new = u_c - jnp.einsum('bhcd,bhdv->bhcv', w_c, state)
        o_c = jnp.einsum('bhcd,bhdv->bhcv', qd_c, state) + jnp.einsum('bhcs,bhsv->bhcv', attn_c, v_new)
        state = state * gl_c[..., None, None] + jnp.einsum('bhcd,bhcv->bhdv', kd_c, v_new)
        return state, o_c

    xs = tuple(jnp.moveaxis(t, 1, 0) for t in (attn, u, w, q_dec, k_dec, g_last))
    state0 = jnp.zeros((bsz, heads, dk, dv), jnp.float32)
    _, o = lax.scan(step, state0, xs)
    o = jnp.moveaxis(jnp.moveaxis(o, 0, 1), 3, 2)
    return o.reshape(bsz, seq, heads, dv)


def gated_deltanet_group(qkv, z, b, a, conv_w, a_log, dt_bias, norm_w):
    bsz, seq, _ = qkv.shape
    f32 = jnp.float32
    qkv = jax.nn.silu(causal_depthwise_conv(qkv, conv_w)).astype(f32)
    q, k, v = jnp.split(qkv, [GDN_KEY_DIM, 2 * GDN_KEY_DIM], axis=-1)
    rep = GDN_V_HEADS // GDN_QK_HEADS
    q = jnp.repeat(l2_normalize(q.reshape(bsz, seq, GDN_QK_HEADS, GDN_HEAD_DIM)), rep, axis=2) * (GDN_HEAD_DIM ** -0.5)
    k = jnp.repeat(l2_normalize(k.reshape(bsz, seq, GDN_QK_HEADS, GDN_HEAD_DIM)), rep, axis=2)
    v = v.reshape(bsz, seq, GDN_V_HEADS, GDN_HEAD_DIM)
    beta = jax.nn.sigmoid(b.astype(f32))
    g = -jnp.exp(a_log.astype(f32)) * jax.nn.softplus(a.astype(f32) + dt_bias.astype(f32))
    o = chunked_gated_delta_rule(q, k, v, g, beta)
    o = rms_norm(o, norm_w) * jax.nn.silu(z.astype(f32).reshape(bsz, seq, GDN_V_HEADS, GDN_HEAD_DIM))
    return o.reshape(bsz, seq, GDN_VALUE_DIM)


def chunked_ssd(x, a, bm, cm):
    bsz, seq, heads, hp = x.shape
    c = SSM_CHUNK
    nc = seq // c
    hg = heads // SSM_GROUPS
    x = x.reshape(bsz, nc, c, SSM_GROUPS, hg, hp)
    a = a.reshape(bsz, nc, c, SSM_GROUPS, hg)
    bm = bm.reshape(bsz, nc, c, SSM_GROUPS, SSM_STATE)
    cm = cm.reshape(bsz, nc, c, SSM_GROUPS, SSM_STATE)
    a_cs = jnp.cumsum(a, axis=2)
    a_t = jnp.moveaxis(a_cs, 2, -1)
    causal = jnp.tril(jnp.ones((c, c), dtype=bool))
    lmat = jnp.exp(jnp.where(causal, a_t[..., :, None] - a_t[..., None, :], -jnp.inf))
    cb = jnp.einsum('bclgn,bcsgn->bcgls', cm, bm)
    y_diag = jnp.einsum('bcghls,bcsghp->bclghp', cb[:, :, :, None] * lmat, x)
    decay_states = jnp.exp(a_cs[:, :, -1:] - a_cs)
    states = jnp.einsum('bclgn,bclghp->bcghpn', bm, x * decay_states[..., None])
    chunk_decay = jnp.exp(a_cs[:, :, -1])

    def step(carry, inp):
        s_c, d_c = inp
        return carry * d_c[..., None, None] + s_c, carry

    init = jnp.zeros((bsz, SSM_GROUPS, hg, hp, SSM_STATE), jnp.float32)
    _, prev = lax.scan(step, init, (jnp.moveaxis(states, 1, 0), jnp.moveaxis(chunk_decay, 1, 0)))
    prev = jnp.moveaxis(prev, 0, 1)
    y_off = jnp.einsum('bclgn,bcghpn->bclghp', cm, prev) * jnp.exp(a_cs)[..., None]
    return (y_diag + y_off).reshape(bsz, seq, heads, hp)


def mamba2_group(xbc, z, dt_raw, conv_w, conv_b, a_log, dt_bias, d_skip, norm_w):
    bsz, seq, _ = xbc.shape
    f32 = jnp.float32
    xbc = jax.nn.silu(causal_depthwise_conv(xbc, conv_w) + conv_b).astype(f32)
    xs, bm, cm = jnp.split(xbc, [SSM_INNER, SSM_INNER + SSM_GROUPS * SSM_STATE], axis=-1)
    xs = xs.reshape(bsz, seq, SSM_HEADS, SSM_HEAD_DIM)
    bm = bm.reshape(bsz, seq, SSM_GROUPS, SSM_STATE)
    cm = cm.reshape(bsz, seq, SSM_GROUPS, SSM_STATE)
    dt = jax.nn.softplus(dt_raw.astype(f32) + dt_bias.astype(f32))
    a = -jnp.exp(a_log.astype(f32)) * dt
    y = chunked_ssd(xs * dt[..., None], a, bm, cm) + d_skip.astype(f32)[:, None] * xs
    y = y.reshape(bsz, seq, SSM_INNER) * jax.nn.silu(z.astype(f32))
    gw = SSM_INNER // SSM_GROUPS
    y = rms_norm(y.reshape(bsz, seq, SSM_GROUPS, gw), norm_w.reshape(SSM_GROUPS, gw))
    return y.reshape(bsz, seq, SSM_INNER)


def hybrid_mixer(u, w_in, gdn_conv_w, gdn_a_log, gdn_dt_bias, gdn_norm_w, ssm_conv_w, ssm_conv_b,
                 ssm_a_log, ssm_dt_bias, ssm_d, ssm_norm_w, w_out):
    proj = u @ w_in
    gdn_qkv, gdn_z, gdn_b, gdn_a, ssm_xbc, ssm_z, ssm_dt = split_columns(proj, IN_SIZES)
    o_gdn = gated_deltanet_group(gdn_qkv, gdn_z, gdn_b, gdn_a, gdn_conv_w, gdn_a_log, gdn_dt_bias, gdn_norm_w)
    o_ssm = mamba2_group(ssm_xbc, ssm_z, ssm_dt, ssm_conv_w, ssm_conv_b, ssm_a_log, ssm_dt_bias, ssm_d, ssm_norm_w)
    o = jnp.concatenate([o_gdn, o_ssm], axis=-1).astype(u.dtype)
    return o @ w_out


def moe_ffn(u, w_router, b_router, w1, b1, w2, b2):
    bsz, seq, d = u.shape
    xt = u.reshape(-1, d)
    n_tok = xt.shape[0]
    n_rows = n_tok * TOP_K
    logits = (xt @ w_router + b_router).astype(jnp.float32)
    top_vals, top_idx = lax.top_k(logits, TOP_K)
    gates = jax.nn.softmax(top_vals, axis=-1)
    e_flat = top_idx.reshape(-1).astype(jnp.int32)
    t_flat = jnp.repeat(jnp.arange(n_tok, dtype=jnp.int32), TOP_K)
    g_flat = gates.reshape(-1)
    order = jnp.argsort(e_flat)
    e_sorted, t_sorted, g_sorted = e_flat[order], t_flat[order], g_flat[order]
    counts = jnp.bincount(e_flat, length=N_EXPERTS)
    padded = ((counts + MOE_BLOCK - 1) // MOE_BLOCK) * MOE_BLOCK
    pad_end = jnp.cumsum(padded)
    pad_start = pad_end - padded
    sort_start = jnp.cumsum(counts) - counts
    dest = pad_start[e_sorted] + (jnp.arange(n_rows, dtype=jnp.int32) - sort_start[e_sorted])
    n_blocks = n_rows // MOE_BLOCK + N_EXPERTS
    tok_pad = jnp.zeros((n_blocks * MOE_BLOCK,), jnp.int32).at[dest].set(t_sorted)
    gate_pad = jnp.zeros((n_blocks * MOE_BLOCK,), jnp.float32).at[dest].set(g_sorted)
    block_start = jnp.arange(n_blocks, dtype=jnp.int32) * MOE_BLOCK
    block_expert = jnp.minimum(jnp.searchsorted(pad_end, block_start, side='right'), N_EXPERTS - 1)

    def expert_block(args):
        idx, e = args
        h = xt[idx] @ w1[e] + b1[e]
        glu = jnp.minimum(h[:, :EXPERT_FF], SWIGLU_LIMIT)
        lin = jnp.clip(h[:, EXPERT_FF:], -SWIGLU_LIMIT, SWIGLU_LIMIT)
        act = glu * jax.nn.sigmoid(SWIGLU_ALPHA * glu) * (lin + 1.0)
        return act @ w2[e] + b2[e]

    out = lax.map(expert_block, (tok_pad.reshape(n_blocks, MOE_BLOCK), block_expert))
    out = out.reshape(-1, d) * gate_pad[:, None].astype(out.dtype)
    y = jnp.zeros_like(xt).at[tok_pad].add(out)
    return y.reshape(bsz, seq, d)


def _dt_bias(key, shape):
    dt = jnp.exp(jax.random.uniform(key, shape, jnp.float32, math.log(1e-3), math.log(1e-1)))
    return dt + jnp.log(-jnp.expm1(-dt))


def setup_inputs(seed: int = 0) -> dict:
    key = jax.random.key(seed)
    ks = jax.random.split(key, 32)
    f32 = jnp.float32

    def nrm(k, shape, scale):
        return jax.random.normal(k, shape, f32) * scale

    def gain(k, shape):
        return 1.0 + 0.02 * jax.random.normal(k, shape, f32)

    L = DEPTH
    return {
        'x': nrm(ks[0], (BATCH, SEQ, D_MODEL), 1.0),
        'p': nrm(ks[1], (DEPTH, BATCH, SEQ, PLE_DIM), 1.0),
        'mix_norm': gain(ks[2], (L, D_MODEL)),
        'w_in': nrm(ks[3], (L, D_MODEL, IN_COLS), D_MODEL ** -0.5),
        'gdn_conv_w': nrm(ks[4], (L, CONV_WIDTH, GDN_CONV_CH), CONV_WIDTH ** -0.5),
        'gdn_a_log': jnp.log(jax.random.uniform(ks[5], (L, GDN_V_HEADS), f32, 1.0, 16.0)),
        'gdn_dt_bias': _dt_bias(ks[6], (L, GDN_V_HEADS)),
        'gdn_norm_w': gain(ks[7], (L, GDN_HEAD_DIM)),
        'ssm_conv_w': nrm(ks[8], (L, CONV_WIDTH, SSM_CONV_CH), CONV_WIDTH ** -0.5),
        'ssm_conv_b': nrm(ks[9], (L, SSM_CONV_CH), 0.02),
        'ssm_a_log': jnp.log(jax.random.uniform(ks[10], (L, SSM_HEADS), f32, 1.0, 16.0)),
        'ssm_dt_bias': _dt_bias(ks[11], (L, SSM_HEADS)),
        'ssm_d': gain(ks[12], (L, SSM_HEADS)),
        'ssm_norm_w': gain(ks[13], (L, SSM_INNER)),
        'w_out': nrm(ks[14], (L, D_MIX, D_MODEL), D_MIX ** -0.5),
        'ffn_norm': gain(ks[15], (L, D_MODEL)),
        'w_router': nrm(ks[16], (L, D_MODEL, N_EXPERTS), D_MODEL ** -0.5),
        'b_router': nrm(ks[17], (L, N_EXPERTS), 0.01),
        'w_mlp1': nrm(ks[18], (L, N_EXPERTS, D_MODEL, 2 * EXPERT_FF), D_MODEL ** -0.5),
        'b_mlp1': nrm(ks[19], (L, N_EXPERTS, 2 * EXPERT_FF), 0.01),
        'w_mlp2': nrm(ks[20], (L, N_EXPERTS, EXPERT_FF, D_MODEL), EXPERT_FF ** -0.5),
        'b_mlp2': nrm(ks[21], (L, N_EXPERTS, D_MODEL), 0.01),
        'ple_norm': gain(ks[22], (L, D_MODEL)),
        'w_ple_gate': nrm(ks[23], (L, D_MODEL, D_MODEL), D_MODEL ** -0.5),
        'w_ple_proj': nrm(ks[24], (L, PLE_DIM, D_MODEL), PLE_DIM ** -0.5),
        'final_norm': gain(ks[25], (D_MODEL,)),
    }


def reference(x, p, mix_norm, w_in, gdn_conv_w, gdn_a_log, gdn_dt_bias, gdn_norm_w, ssm_conv_w, ssm_conv_b,
              ssm_a_log, ssm_dt_bias, ssm_d, ssm_norm_w, w_out, ffn_norm, w_router, b_router, w_mlp1, b_mlp1,
              w_mlp2, b_mlp2, ple_norm, w_ple_gate, w_ple_proj, final_norm):
    h = x
    for i in range(DEPTH):
        u = rms_norm(h, mix_norm[i])
        h = h + hybrid_mixer(u, w_in[i], gdn_conv_w[i], gdn_a_log[i], gdn_dt_bias[i], gdn_norm_w[i],
                             ssm_conv_w[i], ssm_conv_b[i], ssm_a_log[i], ssm_dt_bias[i], ssm_d[i],
                             ssm_norm_w[i], w_out[i])
        u = rms_norm(h, ffn_norm[i])
        h = h + moe_ffn(u, w_router[i], b_router[i], w_mlp1[i], b_mlp1[i], w_mlp2[i], b_mlp2[i])
        u = rms_norm(h, ple_norm[i])
        h = h + (p[i] @ w_ple_proj[i]) * jax.nn.sigmoid(u @ w_ple_gate[i])
    return rms_norm(h, final_norm)
```

```python
import functools

import jax
import jax.numpy as jnp
from jax import lax
from jax.experimental import pallas as pl
from jax.experimental.pallas import tpu as pltpu

F32 = jnp.float32
BF16 = jnp.bfloat16
HIGHEST = lax.Precision.HIGHEST

NORM_EPS = 1e-6
CONV_WIDTH = 4
GDN_HEAD_DIM = 128
SSM_HEAD_DIM = 64
SSM_GROUPS = 4
SSM_STATE = 128
N_EXPERTS = 32
TOP_K = 4
SWIGLU_ALPHA = 1.702
SWIGLU_LIMIT = 7.0

LANES = 128
CHUNK = 128
VMEM_LIMIT = 56 * 1024 * 1024

NEG_BIG = -1e30


def _params(sem, vmem=VMEM_LIMIT):
    return pltpu.CompilerParams(dimension_semantics=sem, vmem_limit_bytes=vmem)


def _rms(x, w):
    ms = jnp.mean(x * x, axis=-1, keepdims=True)
    return x * lax.rsqrt(ms + NORM_EPS) * w


def _sigmoid(x):
    return 1.0 / (1.0 + jnp.exp(-x))


def _softplus(x):
    return jnp.maximum(x, 0.0) + jnp.log(1.0 + jnp.exp(-jnp.abs(x)))


def _mm(a, b):
    return jnp.dot(a.astype(BF16), b.astype(BF16), preferred_element_type=F32)


def _mm_nt(a, b):
    return lax.dot_general(a.astype(BF16), b.astype(BF16), (((1,), (1,)), ((), ())),
                           preferred_element_type=F32)


def _mm_tn(a, b):
    return lax.dot_general(a.astype(BF16), b.astype(BF16), (((0,), (0,)), ((), ())),
                           preferred_element_type=F32)


def _mm_exact(a, b):
    return jnp.dot(a, b, precision=HIGHEST, preferred_element_type=F32)


def _causal_conv_silu(x, w, bias=None):
    seq = x.shape[0]
    row = lax.broadcasted_iota(jnp.int32, (seq, 1), 0)
    acc = x * w[CONV_WIDTH - 1:CONV_WIDTH, :]
    for k in range(CONV_WIDTH - 1):
        shift = CONV_WIDTH - 1 - k
        xs = pltpu.roll(x, shift, axis=0)
        xs = jnp.where(row >= shift, xs, 0.0)
        acc = acc + xs * w[k:k + 1, :]
    if bias is not None:
        acc = acc + bias
    return acc * _sigmoid(acc)


def _chunk_cumsum(vals, out_ref):
    seq = vals.shape[0]
    r = lax.broadcasted_iota(jnp.int32, (CHUNK, CHUNK), 0)
    c = lax.broadcasted_iota(jnp.int32, (CHUNK, CHUNK), 1)
    tri = (r >= c).astype(F32)
    for i in range(seq // CHUNK):
        out_ref[i * CHUNK:(i + 1) * CHUNK, :] = _mm_exact(tri, vals[i * CHUNK:(i + 1) * CHUNK, :])


def _in_proj_kernel(x_ref, g_ref, w_ref, ws_ref, o_ref, os_ref, u_ref):
    @pl.when(pl.program_id(1) == 0)
    def _():
        u = _rms(x_ref[...], g_ref[...]).astype(BF16)
        u_ref[...] = u
        os_ref[...] = jnp.dot(u, ws_ref[...], preferred_element_type=F32)

    o_ref[...] = jnp.dot(u_ref[...], w_ref[...], preferred_element_type=F32).astype(o_ref.dtype)


def _in_proj(x2, norm_w, w_main, w_small, tm=512, tn=1024):
    tok, d = x2.shape
    n = w_main.shape[1]
    return pl.pallas_call(
        _in_proj_kernel,
        grid=(tok // tm, n // tn),
        in_specs=[
            pl.BlockSpec((tm, d), lambda i, j: (i, 0)),
            pl.BlockSpec((1, d), lambda i, j: (0, 0)),
            pl.BlockSpec((d, tn), lambda i, j: (0, j)),
            pl.BlockSpec((d, LANES), lambda i, j: (0, 0)),
        ],
        out_specs=[
            pl.BlockSpec((tm, tn), lambda i, j: (i, j)),
            pl.BlockSpec((tm, LANES), lambda i, j: (i, 0)),
        ],
        out_shape=[
            jax.ShapeDtypeStruct((tok, n), BF16),
            jax.ShapeDtypeStruct((tok, LANES), F32),
        ],
        scratch_shapes=[pltpu.VMEM((tm, d), BF16)],
        compiler_params=_params(("parallel", "arbitrary")),
        name="in_proj",
    )(x2, norm_w, w_main, w_small)


def _tri_masks(n):
    r = lax.broadcasted_iota(jnp.int32, (n, n), 0)
    c = lax.broadcasted_iota(jnp.int32, (n, n), 1)
    same = lambda s: (r // s) == (c // s)
    bd16 = same(16).astype(F32)
    levels = []
    s = 16
    while s < n:
        levels.append((same(2 * s) & jnp.logical_not(same(s))).astype(F32))
        s *= 2
    return bd16, levels


def _unit_lower_inverse(lmat, eye, bd16, levels):
    ld = lmat * bd16
    p = eye - ld
    m = _mm(ld, ld)
    p = p + _mm(p, m)
    m = _mm(m, m)
    p = p + _mm(p, m)
    m = _mm(m, m)
    p = p + _mm(p, m)
    for msk in levels:
        p = p - _mm(p, _mm(lmat * msk, p))
    return p


def _gdn_kernel(q_ref, k_ref, v_ref, z_ref, sm_ref, wq_ref, wk_ref, wv_ref, prm_ref, nw_ref,
                o_ref, gc_ref, gct_ref, q_s, k_s, kb_s, vb_s, kbe_s, qd_s, st_s, oacc_s, *, n_vh):
    hq = pl.program_id(1)
    seq = q_ref.shape[0]
    dk = GDN_HEAD_DIM
    nchunk = seq // CHUNK

    q = _causal_conv_silu(q_ref[...].astype(F32), wq_ref[...])
    k = _causal_conv_silu(k_ref[...].astype(F32), wk_ref[...])
    v = _causal_conv_silu(v_ref[...].astype(F32), wv_ref[...])
    q = q * lax.rsqrt(jnp.sum(q * q, axis=-1, keepdims=True) + NORM_EPS) * (dk ** -0.5)
    k = k * lax.rsqrt(jnp.sum(k * k, axis=-1, keepdims=True) + NORM_EPS)
    q_s[...] = q.astype(BF16)
    k_s[...] = k.astype(BF16)

    sm = sm_ref[...]
    beta_all = _sigmoid(sm)
    g_all = -jnp.exp(prm_ref[0:1, :]) * _softplus(sm + prm_ref[1:2, :])
    r = lax.broadcasted_iota(jnp.int32, (LANES, LANES), 0)
    c = lax.broadcasted_iota(jnp.int32, (LANES, LANES), 1)
    sel_g = ((c < 2) & (r == n_vh + 2 * hq + c)).astype(F32)
    sel_b = ((c < 2) & (r == 2 * hq + c)).astype(F32)
    g_sel = _mm_exact(g_all, sel_g)
    beta = _mm_exact(beta_all, sel_b)
    _chunk_cumsum(g_sel, gc_ref)
    gc = gc_ref[...]
    gct_ref[...] = gc.T
    eg = jnp.exp(gc)
    for hv in range(2):
        bcol = beta[:, hv:hv + 1]
        ecol = eg[:, hv:hv + 1]
        kb = k * bcol
        kb_s[hv] = kb.astype(BF16)
        kbe_s[hv] = (kb * ecol).astype(BF16)
        vb_s[hv] = (v[:, hv * dk:(hv + 1) * dk] * bcol).astype(BF16)
        qd_s[hv] = (q * ecol).astype(BF16)
    st_s[...] = jnp.zeros_like(st_s)

    rr = lax.broadcasted_iota(jnp.int32, (CHUNK, CHUNK), 0)
    cc = lax.broadcasted_iota(jnp.int32, (CHUNK, CHUNK), 1)
    causal = rr >= cc
    strict = rr > cc
    eye = (rr == cc).astype(F32)
    bd16, levels = _tri_masks(CHUNK)

    def chunk_body(ci, carry):
        sl = pl.ds(pl.multiple_of(ci * CHUNK, CHUNK), CHUNK)
        q_c = q_s[sl, :]
        k_c = k_s[sl, :]
        gcol = gc_ref[sl, :]
        grow = gct_ref[:, sl]
        qk = _mm_nt(q_c, k_c)
        for hv in range(2):
            gc_c = gcol[:, hv:hv + 1]
            gr_c = grow[hv:hv + 1, :]
            decay = jnp.where(causal, jnp.exp(jnp.minimum(gc_c - gr_c, 0.0)), 0.0)
            kb_c = kb_s[hv, sl, :]
            lmat = jnp.where(strict, _mm_nt(kb_c, k_c) * decay, 0.0)
            pinv = _unit_lower_inverse(lmat, eye, bd16, levels)
            u = _mm(pinv, vb_s[hv, sl, :])
            w = _mm(pinv, kbe_s[hv, sl, :])
            attn = qk * decay
            g_last = gc_c[CHUNK - 1:CHUNK, :]
            kd = k_c.astype(F32) * jnp.exp(g_last - gc_c)
            state = st_s[hv]
            v_new = u - _mm(w, state)
            o_c = _mm(qd_s[hv, sl, :], state) + _mm(attn, v_new)
            st_s[hv] = state * jnp.exp(g_last) + _mm_tn(kd, v_new)
            oacc_s[hv, sl, :] = o_c
        return carry

    lax.fori_loop(0, nchunk, chunk_body, 0)

    z = z_ref[...].astype(F32)
    nw = nw_ref[...]
    for hv in range(2):
        o = _rms(oacc_s[hv], nw)
        zz = z[:, hv * dk:(hv + 1) * dk]
        o_ref[:, hv * dk:(hv + 1) * dk] = (o * (zz * _sigmoid(zz))).astype(o_ref.dtype)


def _gdn(proj3, small3, conv_w, prm, norm_w, *, qkv_off, z_off, n_qk, out_cols):
    bsz, seq, _ = proj3.shape
    dk = GDN_HEAD_DIM
    qb = qkv_off // dk
    kb = qb + n_qk
    vb = (qkv_off + 2 * n_qk * dk) // (2 * dk)
    zb = z_off // (2 * dk)
    cq = 0
    ck = n_qk
    cv = (2 * n_qk * dk) // (2 * dk)
    return pl.pallas_call(
        functools.partial(_gdn_kernel, n_vh=2 * n_qk),
        grid=(bsz, n_qk),
        in_specs=[
            pl.BlockSpec((None, seq, dk), lambda b, h: (b, 0, qb + h)),
            pl.BlockSpec((None, seq, dk), lambda b, h: (b, 0, kb + h)),
            pl.BlockSpec((None, seq, 2 * dk), lambda b, h: (b, 0, vb + h)),
            pl.BlockSpec((None, seq, 2 * dk), lambda b, h: (b, 0, zb + h)),
            pl.BlockSpec((None, seq, LANES), lambda b, h: (b, 0, 0)),
            pl.BlockSpec((CONV_WIDTH, dk), lambda b, h: (0, cq + h)),
            pl.BlockSpec((CONV_WIDTH, dk), lambda b, h: (0, ck + h)),
            pl.BlockSpec((CONV_WIDTH, 2 * dk), lambda b, h: (0, cv + h)),
            pl.BlockSpec((8, LANES), lambda b, h: (0, 0)),
            pl.BlockSpec((1, dk), lambda b, h: (0, 0)),
        ],
        out_specs=pl.BlockSpec((None, seq, 2 * dk), lambda b, h: (b, 0, h)),
        out_shape=jax.ShapeDtypeStruct((bsz, seq, out_cols), BF16),
        scratch_shapes=[
            pltpu.VMEM((seq, LANES), F32),
            pltpu.VMEM((LANES, seq), F32),
            pltpu.VMEM((seq, dk), BF16),
            pltpu.VMEM((seq, dk), BF16),
            pltpu.VMEM((2, seq, dk), BF16),
            pltpu.VMEM((2, seq, dk), BF16),
            pltpu.VMEM((2, seq, dk), BF16),
            pltpu.VMEM((2, seq, dk), BF16),
            pltpu.VMEM((2, dk, dk), F32),
            pltpu.VMEM((2, seq, dk), F32),
        ],
        compiler_params=_params(("parallel", "parallel")),
        name="gdn",
    )(proj3, proj3, proj3, proj3, small3, conv_w, conv_w, conv_w, prm, norm_w)


def _ssd_kernel(x_ref, b_ref, c_ref, z_ref, sm_ref, wx_ref, wb_ref, wc_ref, cbx_ref, cbb_ref, cbc_ref,
                prm_ref, drep_ref, nw_ref, o_ref,
                acs_ref, acst_ref, xs_s, xdt_s, b_s, c_s, st_s, y_s, *, dt_col0, heads_per_group):
    grp = pl.program_id(1)
    seq = x_ref.shape[0]
    hp = SSM_HEAD_DIM
    hg = heads_per_group
    nchunk = seq // CHUNK

    xs = _causal_conv_silu(x_ref[...].astype(F32), wx_ref[...], cbx_ref[...])
    bm = _causal_conv_silu(b_ref[...].astype(F32), wb_ref[...], cbb_ref[...])
    cm = _causal_conv_silu(c_ref[...].astype(F32), wc_ref[...], cbc_ref[...])
    xs_s[...] = xs
    b_s[...] = bm.astype(BF16)
    c_s[...] = cm.astype(BF16)

    sm = sm_ref[...]
    dt_all = _softplus(sm + prm_ref[1:2, :])
    a_all = -jnp.exp(prm_ref[0:1, :]) * dt_all
    r = lax.broadcasted_iota(jnp.int32, (LANES, LANES), 0)
    c = lax.broadcasted_iota(jnp.int32, (LANES, LANES), 1)
    sel = ((c < hg) & (r == dt_col0 + hg * grp + c)).astype(F32)
    dt_g = _mm_exact(dt_all, sel)
    a_g = _mm_exact(a_all, sel)
    _chunk_cumsum(a_g, acs_ref)
    acst_ref[...] = acs_ref[...].T
    for j in range(hg):
        xdt_s[:, j * hp:(j + 1) * hp] = (xs[:, j * hp:(j + 1) * hp] * dt_g[:, j:j + 1]).astype(BF16)
    st_s[...] = jnp.zeros_like(st_s)

    rr = lax.broadcasted_iota(jnp.int32, (CHUNK, CHUNK), 0)
    cc = lax.broadcasted_iota(jnp.int32, (CHUNK, CHUNK), 1)
    causal = rr >= cc

    def chunk_body(ci, carry):
        sl = pl.ds(pl.multiple_of(ci * CHUNK, CHUNK), CHUNK)
        b_c = b_s[sl, :]
        c_c = c_s[sl, :]
        cb = _mm_nt(c_c, b_c)
        acol = acs_ref[sl, :]
        arow = acst_ref[:, sl]
        for j in range(hg):
            ac = acol[:, j:j + 1]
            ar = arow[j:j + 1, :]
            lm = jnp.where(causal, jnp.exp(jnp.minimum(ac - ar, 0.0)), 0.0)
            xj = xdt_s[sl, j * hp:(j + 1) * hp]
            y_diag = _mm(cb * lm, xj)
            a_last = ac[CHUNK - 1:CHUNK, :]
            xdec = xj.astype(F32) * jnp.exp(a_last - ac)
            st_new = _mm_tn(xdec, b_c)
            prev = st_s[j]
            y_off = _mm_nt(c_c, prev) * jnp.exp(ac)
            st_s[j] = prev * jnp.exp(a_last) + st_new
            y_s[sl, j * hp:(j + 1) * hp] = y_diag + y_off
        return carry

    lax.fori_loop(0, nchunk, chunk_body, 0)

    z = z_ref[...].astype(F32)
    y = (y_s[...] + drep_ref[...] * xs_s[...]) * (z * _sigmoid(z))
    o_ref[...] = _rms(y, nw_ref[...]).astype(o_ref.dtype)


def _ssd(proj3, small3, conv_w, conv_b, prm, d_rep, norm_w, *, xbc_off, z_off, inner, dt_col0):
    bsz, seq, _ = proj3.shape
    gw = inner // SSM_GROUPS
    hg = gw // SSM_HEAD_DIM
    ns = SSM_STATE
    xb = xbc_off // gw
    bb = (xbc_off + inner) // ns
    cb = bb + SSM_GROUPS
    zb = z_off // gw
    wbb = inner // ns
    wcb = wbb + SSM_GROUPS
    kern = functools.partial(_ssd_kernel, dt_col0=dt_col0, heads_per_group=hg)
    return pl.pallas_call(
        kern,
        grid=(bsz, SSM_GROUPS),
        in_specs=[
            pl.BlockSpec((None, seq, gw), lambda b, g: (b, 0, xb + g)),
            pl.BlockSpec((None, seq, ns), lambda b, g: (b, 0, bb + g)),
            pl.BlockSpec((None, seq, ns), lambda b, g: (b, 0, cb + g)),
            pl.BlockSpec((None, seq, gw), lambda b, g: (b, 0, zb + g)),
            pl.BlockSpec((None, seq, LANES), lambda b, g: (b, 0, 0)),
            pl.BlockSpec((CONV_WIDTH, gw), lambda b, g: (0, g)),
            pl.BlockSpec((CONV_WIDTH, ns), lambda b, g: (0, wbb + g)),
            pl.BlockSpec((CONV_WIDTH, ns), lambda b, g: (0, wcb + g)),
            pl.BlockSpec((1, gw), lambda b, g: (0, g)),
            pl.BlockSpec((1, ns), lambda b, g: (0, wbb + g)),
            pl.BlockSpec((1, ns), lambda b, g: (0, wcb + g)),
            pl.BlockSpec((8, LANES), lambda b, g: (0, 0)),
            pl.BlockSpec((1, gw), lambda b, g: (0, g)),
            pl.BlockSpec((1, gw), lambda b, g: (0, g)),
        ],
        out_specs=pl.BlockSpec((None, seq, gw), lambda b, g: (b, 0, g)),
        out_shape=jax.ShapeDtypeStruct((bsz, seq, inner), BF16),
        scratch_shapes=[
            pltpu.VMEM((seq, LANES), F32),
            pltpu.VMEM((LANES, seq), F32),
            pltpu.VMEM((seq, gw), F32),
            pltpu.VMEM((seq, gw), BF16),
            pltpu.VMEM((seq, ns), BF16),
            pltpu.VMEM((seq, ns), BF16),
            pltpu.VMEM((hg, SSM_HEAD_DIM, ns), F32),
            pltpu.VMEM((seq, gw), F32),
        ],
        compiler_params=_params(("parallel", "parallel")),
        name="ssd",
    )(proj3, proj3, proj3, proj3, small3, conv_w, conv_w, conv_w, conv_b, conv_b, conv_b,
      prm, d_rep, norm_w)


def _out_proj_kernel(a_ref, b_ref, wa_ref, wb_ref, x_ref, o_ref):
    acc = jnp.dot(a_ref[...], wa_ref[...], preferred_element_type=F32)
    acc = acc + jnp.dot(b_ref[...], wb_ref[...], preferred_element_type=F32)
    o_ref[...] = x_ref[...] + acc


def _out_proj(o_a, o_b, w_a, w_b, x2, tm=512, tn=1024):
    tok, d = x2.shape
    ka = o_a.shape[1]
    kb = o_b.shape[1]
    return pl.pallas_call(
        _out_proj_kernel,
        grid=(tok // tm, d // tn),
        in_specs=[
            pl.BlockSpec((tm, ka), lambda i, j: (i, 0)),
            pl.BlockSpec((tm, kb), lambda i, j: (i, 0)),
            pl.BlockSpec((ka, tn), lambda i, j: (0, j)),
            pl.BlockSpec((kb, tn), lambda i, j: (0, j)),
            pl.BlockSpec((tm, tn), lambda i, j: (i, j)),
        ],
        out_specs=pl.BlockSpec((tm, tn), lambda i, j: (i, j)),
        out_shape=jax.ShapeDtypeStruct((tok, d), F32),
        compiler_params=_params(("parallel", "parallel")),
        name="out_proj",
    )(o_a, o_b, w_a, w_b, x2)


def _router_kernel(h_ref, nw_ref, wr_ref, br_ref, u_ref, idx_ref, gate_ref, rank_ref, cnt_ref, carry_s):
    step = pl.program_id(0)
    tm = h_ref.shape[0]

    @pl.when(step == 0)
    def _():
        carry_s[...] = jnp.zeros_like(carry_s)

    u = _rms(h_ref[...], nw_ref[...])
    u_ref[...] = u
    logits = _mm_exact(u, wr_ref[...]) + br_ref[...]
    lane = lax.broadcasted_iota(jnp.int32, (tm, LANES), 1).astype(F32)
    work = logits
    vals, idxs = [], []
    for _ in range(TOP_K):
        m = jnp.max(work, axis=1, keepdims=True)
        idx = jnp.min(jnp.where(work == m, lane, float(LANES)), axis=1, keepdims=True)
        vals.append(m)
        idxs.append(idx)
        work = jnp.where(lane == idx, -jnp.inf, work)
    onehot = (work == -jnp.inf).astype(F32)
    exps = [jnp.exp(v - vals[0]) for v in vals]
    denom = exps[0]
    for e in exps[1:]:
        denom = denom + e
    r = lax.broadcasted_iota(jnp.int32, (tm, tm), 0)
    c = lax.broadcasted_iota(jnp.int32, (tm, tm), 1)
    below = (r > c).astype(BF16)
    excl = jnp.dot(below, onehot.astype(BF16), preferred_element_type=F32) + carry_s[...]
    idx_out = jnp.zeros((tm, LANES), F32)
    gate_out = jnp.zeros((tm, LANES), F32)
    rank_out = jnp.zeros((tm, LANES), F32)
    for kk in range(TOP_K):
        rank_k = jnp.sum(jnp.where(lane == idxs[kk], excl, 0.0), axis=1, keepdims=True)
        idx_out = jnp.where(lane == kk, idxs[kk], idx_out)
        gate_out = jnp.where(lane == kk, exps[kk] / denom, gate_out)
        rank_out = jnp.where(lane == kk, rank_k, rank_out)
    idx_ref[...] = idx_out.astype(jnp.int32)
    gate_ref[...] = gate_out
    rank_ref[...] = rank_out.astype(jnp.int32)
    total = carry_s[...] + jnp.sum(onehot, axis=0, keepdims=True)
    carry_s[...] = total
    cnt_ref[...] = jnp.broadcast_to(total, cnt_ref.shape)


def _router(h2d, norm_w, w_r, b_r, tm=512):
    tok, d = h2d.shape
    return pl.pallas_call(
        _router_kernel,
        grid=(tok // tm,),
        in_specs=[
            pl.BlockSpec((tm, d), lambda i: (i, 0)),
            pl.BlockSpec((1, d), lambda i: (0, 0)),
            pl.BlockSpec((d, LANES), lambda i: (0, 0)),
            pl.BlockSpec((1, LANES), lambda i: (0, 0)),
        ],
        out_specs=[
            pl.BlockSpec((tm, d), lambda i: (i, 0)),
            pl.BlockSpec((tm, LANES), lambda i: (i, 0)),
            pl.BlockSpec((tm, LANES), lambda i: (i, 0)),
            pl.BlockSpec((tm, LANES), lambda i: (i, 0)),
            pl.BlockSpec((8, LANES), lambda i: (0, 0)),
        ],
        out_shape=[
            jax.ShapeDtypeStruct((tok, d), F32),
            jax.ShapeDtypeStruct((tok, LANES), jnp.int32),
            jax.ShapeDtypeStruct((tok, LANES), F32),
            jax.ShapeDtypeStruct((tok, LANES), jnp.int32),
            jax.ShapeDtypeStruct((8, LANES), F32),
        ],
        scratch_shapes=[pltpu.VMEM((1, LANES), F32)],
        compiler_params=_params(("arbitrary",)),
        name="router",
    )(h2d, norm_w, w_r, b_r)


def _dispatch_kernel(pos_ref, u_hbm, xs_in, xs_hbm, sem, *, tm):
    del xs_in
    base = pl.program_id(0) * tm

    def row_copy(t, k):
        return pltpu.make_async_copy(u_hbm.at[pl.ds(base + t, 1)],
                                     xs_hbm.at[pl.ds(pos_ref[t * TOP_K + k], 1)], sem)

    def issue(t, carry):
        for k in range(TOP_K):
            row_copy(t, k).start()
        return carry

    def drain(t, carry):
        for k in range(TOP_K):
            row_copy(t, k).wait()
        return carry

    lax.fori_loop(0, tm, issue, 0)
    lax.fori_loop(0, tm, drain, 0)


def _dispatch(pos_flat, u2d, xs_init, tm=256):
    tok, d = u2d.shape
    kern = functools.partial(_dispatch_kernel, tm=tm)
    return pl.pallas_call(
        kern,
        grid=(tok // tm,),
        in_specs=[
            pl.BlockSpec((tm * TOP_K,), lambda i: (i,), memory_space=pltpu.SMEM),
            pl.BlockSpec(memory_space=pl.ANY),
            pl.BlockSpec(memory_space=pl.ANY),
        ],
        out_specs=pl.BlockSpec(memory_space=pl.ANY),
        out_shape=jax.ShapeDtypeStruct(xs_init.shape, xs_init.dtype),
        scratch_shapes=[pltpu.SemaphoreType.DMA(())],
        input_output_aliases={2: 0},
        compiler_params=_params(("arbitrary",)),
        name="dispatch",
    )(pos_flat, u2d, xs_init)


def _moe_kernel(bexp_ref, nused_ref, x_ref, w1g_ref, w1l_ref, b1g_ref, b1l_ref, w2_ref, b2_ref,
                o_ref, xb_s):
    i = pl.program_id(0)
    j = pl.program_id(1)

    @pl.when(i < nused_ref[0])
    def _():
        @pl.when(j == 0)
        def _():
            xb_s[...] = x_ref[...].astype(BF16)

        x = xb_s[...]
        hg = jnp.dot(x, w1g_ref[...].astype(BF16), preferred_element_type=F32) + b1g_ref[...]
        hl = jnp.dot(x, w1l_ref[...].astype(BF16), preferred_element_type=F32) + b1l_ref[...]
        glu = jnp.minimum(hg, SWIGLU_LIMIT)
        lin = jnp.clip(hl, -SWIGLU_LIMIT, SWIGLU_LIMIT)
        act = glu * _sigmoid(SWIGLU_ALPHA * glu) * (lin + 1.0)
        y = jnp.dot(act.astype(BF16), w2_ref[...].astype(BF16), preferred_element_type=F32)

        @pl.when(j == 0)
        def _():
            o_ref[...] = y + b2_ref[...]

        @pl.when(j > 0)
        def _():
            o_ref[...] += y

    @pl.when((i >= nused_ref[0]) & (j == 0))
    def _():
        o_ref[...] = jnp.zeros_like(o_ref)


def _moe(bexp, nused, xs, w1, b1, w2, b2, *, bm, tf):
    rows, d = xs.shape
    ff = w2.shape[1]
    nb = rows // bm
    nf = ff // tf

    def jj(i, j, nu):
        return jnp.where(i < nu[0], j, nf - 1)

    grid_spec = pltpu.PrefetchScalarGridSpec(
        num_scalar_prefetch=2,
        grid=(nb, nf),
        in_specs=[
            pl.BlockSpec((bm, d), lambda i, j, be, nu: (i, 0)),
            pl.BlockSpec((None, d, tf), lambda i, j, be, nu: (be[i], 0, jj(i, j, nu))),
            pl.BlockSpec((None, d, tf), lambda i, j, be, nu: (be[i], 0, nf + jj(i, j, nu))),
            pl.BlockSpec((None, 1, tf), lambda i, j, be, nu: (be[i], 0, jj(i, j, nu))),
            pl.BlockSpec((None, 1, tf), lambda i, j, be, nu: (be[i], 0, nf + jj(i, j, nu))),
            pl.BlockSpec((None, tf, d), lambda i, j, be, nu: (be[i], jj(i, j, nu), 0)),
            pl.BlockSpec((None, 1, d), lambda i, j, be, nu: (be[i], 0, 0)),
        ],
        out_specs=pl.BlockSpec((bm, d), lambda i, j, be, nu: (i, 0)),
        scratch_shapes=[pltpu.VMEM((bm, d), BF16)],
    )
    return pl.pallas_call(
        _moe_kernel,
        grid_spec=grid_spec,
        out_shape=jax.ShapeDtypeStruct((rows, d), F32),
        compiler_params=_params(("arbitrary", "arbitrary")),
        name="moe",
    )(bexp, nused, xs, w1, w1, b1, b1, w2, b2)


def _combine_kernel(pos_ref, gate_ref, h_ref, ys_hbm, p_ref, pn_ref, wg_ref, wp_ref, fn_ref, o_ref,
                    gbuf, sem, *, tm):
    def row_copy(t, k):
        return pltpu.make_async_copy(ys_hbm.at[pl.ds(pos_ref[t * TOP_K + k], 1)],
                                     gbuf.at[k, pl.ds(t, 1)], sem)

    def issue(t, carry):
        for k in range(TOP_K):
            row_copy(t, k).start()
        return carry

    def drain(t, carry):
        for k in range(TOP_K):
            row_copy(t, k).wait()
        return carry

    lax.fori_loop(0, tm, issue, 0)
    lax.fori_loop(0, tm, drain, 0)

    gate = gate_ref[...]
    h = h_ref[...]
    for k in range(TOP_K):
        h = h + gate[:, k:k + 1] * gbuf[k]
    u = _rms(h, pn_ref[...]).astype(BF16)
    gv = _sigmoid(jnp.dot(u, wg_ref[...], preferred_element_type=F32))
    pp = jnp.dot(p_ref[...].astype(BF16), wp_ref[...], preferred_element_type=F32)
    h = h + pp * gv
    o_ref[...] = _rms(h, fn_ref[...])


def _combine(pos_flat, gate, h2d, ys, p2d, ple_norm, w_gate, w_proj, final_norm, tm=256):
    tok, d = h2d.shape
    pd = p2d.shape[1]
    kern = functools.partial(_combine_kernel, tm=tm)
    return pl.pallas_call(
        kern,
        grid=(tok // tm,),
        in_specs=[
            pl.BlockSpec((tm * TOP_K,), lambda i: (i,), memory_space=pltpu.SMEM),
            pl.BlockSpec((tm, LANES), lambda i: (i, 0)),
            pl.BlockSpec((tm, d), lambda i: (i, 0)),
            pl.BlockSpec(memory_space=pl.ANY),
            pl.BlockSpec((tm, pd), lambda i: (i, 0)),
            pl.BlockSpec((1, d), lambda i: (0, 0)),
            pl.BlockSpec((d, d), lambda i: (0, 0)),
            pl.BlockSpec((pd, d), lambda i: (0, 0)),
            pl.BlockSpec((1, d), lambda i: (0, 0)),
        ],
        out_specs=pl.BlockSpec((tm, d), lambda i: (i, 0)),
        out_shape=jax.ShapeDtypeStruct((tok, d), F32),
        scratch_shapes=[pltpu.VMEM((TOP_K, tm, d), F32), pltpu.SemaphoreType.DMA(())],
        compiler_params=_params(("arbitrary",)),
        name="combine",
    )(pos_flat, gate, h2d, ys, p2d, ple_norm, w_gate, w_proj, final_norm)


def _pad_cols(v, off):
    return jnp.pad(v.astype(F32), (off, LANES - off - v.shape[0]))


def _layer(h3, p3, mix_norm, w_in, gdn_conv_w, gdn_a_log, gdn_dt_bias, gdn_norm_w, ssm_conv_w,
           ssm_conv_b, ssm_a_log, ssm_dt_bias, ssm_d, ssm_norm_w, w_out, ffn_norm, w_router, b_router,
           w_mlp1, b_mlp1, w_mlp2, b_mlp2, ple_norm, w_ple_gate, w_ple_proj, final_norm, *, moe_bm, moe_tf):
    bsz, seq, d = h3.shape
    tok = bsz * seq
    n_vh = gdn_a_log.shape[0]
    n_qk = n_vh // 2
    key_dim = n_qk * GDN_HEAD_DIM
    val_dim = n_vh * GDN_HEAD_DIM
    gdn_conv_ch = 2 * key_dim + val_dim
    ssm_heads = ssm_a_log.shape[0]
    inner = ssm_heads * SSM_HEAD_DIM
    ssm_conv_ch = inner + 2 * SSM_GROUPS * SSM_STATE

    o0 = 0
    o_qkv, o0 = o0, o0 + gdn_conv_ch
    o_z, o0 = o0, o0 + val_dim
    o_b, o0 = o0, o0 + n_vh
    o_a, o0 = o0, o0 + n_vh
    o_xbc, o0 = o0, o0 + ssm_conv_ch
    o_sz, o0 = o0, o0 + inner
    o_dt, o0 = o0, o0 + ssm_heads
    w_main = jnp.concatenate([w_in[:, o_qkv:o_b], w_in[:, o_xbc:o_dt]], axis=1).astype(BF16)
    w_small = jnp.concatenate([w_in[:, o_b:o_xbc], w_in[:, o_dt:o0]], axis=1)
    n_small = w_small.shape[1]
    w_small = jnp.pad(w_small, ((0, 0), (0, LANES - n_small))).astype(BF16)
    m_qkv, m_z = 0, gdn_conv_ch
    m_xbc = gdn_conv_ch + val_dim
    m_sz = m_xbc + ssm_conv_ch
    dt_col0 = 2 * n_vh

    x2 = h3.reshape(tok, d)
    proj, small = _in_proj(x2, mix_norm.reshape(1, d), w_main, w_small)
    proj3 = proj.reshape(bsz, seq, proj.shape[1])
    small3 = small.reshape(bsz, seq, LANES)

    zrow = jnp.zeros((LANES,), F32)
    gdn_prm = jnp.stack([_pad_cols(gdn_a_log, n_vh), _pad_cols(gdn_dt_bias, n_vh)] + [zrow] * 6)
    o_gdn = _gdn(proj3, small3, gdn_conv_w, gdn_prm, gdn_norm_w.reshape(1, GDN_HEAD_DIM),
                 qkv_off=m_qkv, z_off=m_z, n_qk=n_qk, out_cols=val_dim)

    ssm_prm = jnp.stack([_pad_cols(ssm_a_log, dt_col0), _pad_cols(ssm_dt_bias, dt_col0)] + [zrow] * 6)
    d_rep = jnp.repeat(ssm_d.astype(F32), SSM_HEAD_DIM).reshape(1, inner)
    o_ssm = _ssd(proj3, small3, ssm_conv_w, ssm_conv_b.reshape(1, ssm_conv_ch), ssm_prm, d_rep,
                 ssm_norm_w.reshape(1, inner), xbc_off=m_xbc, z_off=m_sz, inner=inner, dt_col0=dt_col0)

    w_out_b = w_out.astype(BF16)
    h1 = _out_proj(o_gdn.reshape(tok, val_dim), o_ssm.reshape(tok, inner),
                   w_out_b[:val_dim], w_out_b[val_dim:], x2)

    n_exp = w_router.shape[1]
    w_r = jnp.pad(w_router, ((0, 0), (0, LANES - n_exp)))
    b_r = jnp.pad(b_router.astype(F32), (0, LANES - n_exp), constant_values=NEG_BIG).reshape(1, LANES)
    u2, idx_l, gate_l, rank_l, cnt = _router(h1, ffn_norm.reshape(1, d), w_r, b_r)

    counts = cnt[0, :n_exp].astype(jnp.int32)
    padded = ((counts + moe_bm - 1) // moe_bm) * moe_bm
    pad_end = jnp.cumsum(padded)
    pad_start = pad_end - padded
    top_idx = idx_l[:, :TOP_K]
    pos = (pad_start[top_idx] + rank_l[:, :TOP_K]).astype(jnp.int32).reshape(-1)
    n_blocks = (tok * TOP_K) // moe_bm + n_exp
    block_start = jnp.arange(n_blocks, dtype=jnp.int32) * moe_bm
    bexp = jnp.minimum(jnp.searchsorted(pad_end, block_start, side='right'), n_exp - 1).astype(jnp.int32)
    nused = (pad_end[-1] // moe_bm).astype(jnp.int32).reshape(1)

    xs = _dispatch(pos, u2, jnp.zeros((n_blocks * moe_bm, d), F32))
    ff = w_mlp2.shape[1]
    ys = _moe(bexp, nused, xs, w_mlp1, b_mlp1.reshape(n_exp, 1, 2 * ff), w_mlp2,
              b_mlp2.reshape(n_exp, 1, d), bm=moe_bm, tf=moe_tf)

    out = _combine(pos, gate_l, h1, ys, p3.reshape(tok, p3.shape[-1]), ple_norm.reshape(1, d),
                   w_ple_gate.astype(BF16), w_ple_proj.astype(BF16), final_norm.reshape(1, d))
    return out.reshape(bsz, seq, d)


def kernel(x, p, mix_norm, w_in, gdn_conv_w, gdn_a_log, gdn_dt_bias, gdn_norm_w, ssm_conv_w, ssm_conv_b,
           ssm_a_log, ssm_dt_bias, ssm_d, ssm_norm_w, w_out, ffn_norm, w_router, b_router, w_mlp1, b_mlp1,
           w_mlp2, b_mlp2, ple_norm, w_ple_gate, w_ple_proj, final_norm):
    assert mix_norm.shape[0] == 1, "single-layer trunk"
    return _layer(x, p[0], mix_norm[0], w_in[0], gdn_conv_w[0], gdn_a_log[0], gdn_dt_bias[0], gdn_norm_w[0],
                  ssm_conv_w[0], ssm_conv_b[0], ssm_a_log[0], ssm_dt_bias[0], ssm_d[0], ssm_norm_w[0],
                  w_out[0], ffn_norm[0], w_router[0], b_router[0], w_mlp1[0], b_mlp1[0], w_mlp2[0],
                  b_mlp2[0], ple_norm[0], w_ple_gate[0], w_ple_proj[0], final_norm, moe_bm=512, moe_tf=256)
```

```python
import functools

import jax
import jax.numpy as jnp
from jax import lax
from jax.experimental import pallas as pl
from jax.experimental.pallas import tpu as pltpu

F32 = jnp.float32
BF16 = jnp.bfloat16
HIGHEST = lax.Precision.HIGHEST

NORM_EPS = 1e-6
CONV_WIDTH = 4
GDN_HEAD_DIM = 128
SSM_HEAD_DIM = 64
SSM_GROUPS = 4
SSM_STATE = 128
N_EXPERTS = 32
TOP_K = 4
SWIGLU_ALPHA = 1.702
SWIGLU_LIMIT = 7.0

LANES = 128
CHUNK = 128
GDN_CHUNKS_PER_TRIP = 4
VMEM_LIMIT = 56 * 1024 * 1024

NEG_BIG = -1e30


def _params(sem, vmem=VMEM_LIMIT):
    return pltpu.CompilerParams(dimension_semantics=sem, vmem_limit_bytes=vmem)


def _rms(x, w):
    ms = jnp.mean(x * x, axis=-1, keepdims=True)
    return x * lax.rsqrt(ms + NORM_EPS) * w


def _sigmoid(x):
    return 1.0 / (1.0 + jnp.exp(-x))


def _softplus(x):
    return jnp.maximum(x, 0.0) + jnp.log(1.0 + jnp.exp(-jnp.abs(x)))


def _mm(a, b):
    return jnp.dot(a.astype(BF16), b.astype(BF16), preferred_element_type=F32)


def _mm_nt(a, b):
    return lax.dot_general(a.astype(BF16), b.astype(BF16), (((1,), (1,)), ((), ())),
                           preferred_element_type=F32)


def _mm_tn(a, b):
    return lax.dot_general(a.astype(BF16), b.astype(BF16), (((0,), (0,)), ((), ())),
                           preferred_element_type=F32)


def _mm_exact(a, b):
    return jnp.dot(a, b, precision=HIGHEST, preferred_element_type=F32)


def _causal_conv_silu(x, w, bias=None):
    seq = x.shape[0]
    row = lax.broadcasted_iota(jnp.int32, (seq, 1), 0)
    acc = x * w[CONV_WIDTH - 1:CONV_WIDTH, :]
    for k in range(CONV_WIDTH - 1):
        shift = CONV_WIDTH - 1 - k
        xs = pltpu.roll(x, shift, axis=0)
        xs = jnp.where(row >= shift, xs, 0.0)
        acc = acc + xs * w[k:k + 1, :]
    if bias is not None:
        acc = acc + bias
    return acc * _sigmoid(acc)


def _chunk_cumsum(vals, out_ref):
    seq = vals.shape[0]
    r = lax.broadcasted_iota(jnp.int32, (CHUNK, CHUNK), 0)
    c = lax.broadcasted_iota(jnp.int32, (CHUNK, CHUNK), 1)
    tri = (r >= c).astype(F32)
    for i in range(seq // CHUNK):
        out_ref[i * CHUNK:(i + 1) * CHUNK, :] = _mm_exact(tri, vals[i * CHUNK:(i + 1) * CHUNK, :])


def _in_proj_kernel(x_ref, g_ref, w_ref, ws_ref, o_ref, os_ref, u_ref):
    @pl.when(pl.program_id(1) == 0)
    def _():
        u = _rms(x_ref[...], g_ref[...]).astype(BF16)
        u_ref[...] = u
        os_ref[...] = jnp.dot(u, ws_ref[...], preferred_element_type=F32)

    o_ref[...] = jnp.dot(u_ref[...], w_ref[...], preferred_element_type=F32).astype(o_ref.dtype)


def _in_proj(x2, norm_w, w_main, w_small, tm=512, tn=1024):
    tok, d = x2.shape
    n = w_main.shape[1]
    return pl.pallas_call(
        _in_proj_kernel,
        grid=(tok // tm, n // tn),
        in_specs=[
            pl.BlockSpec((tm, d), lambda i, j: (i, 0)),
            pl.BlockSpec((1, d), lambda i, j: (0, 0)),
            pl.BlockSpec((d, tn), lambda i, j: (0, j)),
            pl.BlockSpec((d, LANES), lambda i, j: (0, 0)),
        ],
        out_specs=[
            pl.BlockSpec((tm, tn), lambda i, j: (i, j)),
            pl.BlockSpec((tm, LANES), lambda i, j: (i, 0)),
        ],
        out_shape=[
            jax.ShapeDtypeStruct((tok, n), BF16),
            jax.ShapeDtypeStruct((tok, LANES), F32),
        ],
        scratch_shapes=[pltpu.VMEM((tm, d), BF16)],
        compiler_params=_params(("parallel", "arbitrary")),
        name="in_proj",
    )(x2, norm_w, w_main, w_small)


def _tri_masks(n):
    r = lax.broadcasted_iota(jnp.int32, (n, n), 0)
    c = lax.broadcasted_iota(jnp.int32, (n, n), 1)
    same = lambda s: (r // s) == (c // s)
    bd16 = same(16).astype(F32)
    levels = []
    s = 16
    while s < n:
        levels.append((same(2 * s) & jnp.logical_not(same(s))).astype(F32))
        s *= 2
    return bd16, levels


def _unit_lower_inverse(lmats, eye, bd16, levels):
    lds = [l * bd16 for l in lmats]
    ps = [eye - ld for ld in lds]
    ms = [_mm(ld, ld) for ld in lds]
    for it in range(3):
        ps = [p + _mm(p, m) for p, m in zip(ps, ms)]
        if it < 2:
            ms = [_mm(m, m) for m in ms]
    for msk in levels:
        ts = [_mm(l * msk, p) for l, p in zip(lmats, ps)]
        ps = [p - _mm(p, t) for p, t in zip(ps, ts)]
    return ps


def _gdn_kernel(q_ref, k_ref, v_ref, z_ref, sm_ref, wq_ref, wk_ref, wv_ref, prm_ref, nw_ref,
                o_ref, gc_ref, gct_ref, q_s, k_s, kb_s, rhs_s, qd_s, a_s, b_s, qp_s, oacc_s, *, n_vh):
    hq = pl.program_id(1)
    seq = q_ref.shape[0]
    dk = GDN_HEAD_DIM
    nchunk = seq // CHUNK

    q = _causal_conv_silu(q_ref[...].astype(F32), wq_ref[...])
    k = _causal_conv_silu(k_ref[...].astype(F32), wk_ref[...])
    v = _causal_conv_silu(v_ref[...].astype(F32), wv_ref[...])
    q = q * lax.rsqrt(jnp.sum(q * q, axis=-1, keepdims=True) + NORM_EPS) * (dk ** -0.5)
    k = k * lax.rsqrt(jnp.sum(k * k, axis=-1, keepdims=True) + NORM_EPS)
    q_s[...] = q.astype(BF16)
    k_s[...] = k.astype(BF16)

    sm = sm_ref[...]
    beta_all = _sigmoid(sm)
    g_all = -jnp.exp(prm_ref[0:1, :]) * _softplus(sm + prm_ref[1:2, :])
    r = lax.broadcasted_iota(jnp.int32, (LANES, LANES), 0)
    c = lax.broadcasted_iota(jnp.int32, (LANES, LANES), 1)
    sel_g = ((c < 2) & (r == n_vh + 2 * hq + c)).astype(F32)
    sel_b = ((c < 2) & (r == 2 * hq + c)).astype(F32)
    g_sel = _mm_exact(g_all, sel_g)
    beta = _mm_exact(beta_all, sel_b)
    _chunk_cumsum(g_sel, gc_ref)
    gc = gc_ref[...]
    gct_ref[...] = gc.T
    eg = jnp.exp(gc)
    for hv in range(2):
        bcol = beta[:, hv:hv + 1]
        ecol = eg[:, hv:hv + 1]
        kb = k * bcol
        kb_s[hv] = kb.astype(BF16)
        rhs_s[hv, :, 0:dk] = (v[:, hv * dk:(hv + 1) * dk] * bcol).astype(BF16)
        rhs_s[hv, :, dk:2 * dk] = (kb * ecol).astype(BF16)
        qd_s[hv] = (q * ecol).astype(BF16)

    rr = lax.broadcasted_iota(jnp.int32, (CHUNK, CHUNK), 0)
    cc = lax.broadcasted_iota(jnp.int32, (CHUNK, CHUNK), 1)
    causal = rr >= cc
    strict = rr > cc
    eye = (rr == cc).astype(F32)
    bd16, levels = _tri_masks(CHUNK)

    unroll = GDN_CHUNKS_PER_TRIP

    def terms_body(ci, carry):
        sls = [pl.ds(pl.multiple_of((ci * unroll + s) * CHUNK, CHUNK), CHUNK) for s in range(unroll)]
        k_cs = [k_s[sl, :] for sl in sls]
        qks = [_mm_nt(q_s[sl, :], k_c) for sl, k_c in zip(sls, k_cs)]
        gcols = [gc_ref[sl, :] for sl in sls]
        grows = [gct_ref[:, sl] for sl in sls]
        chains = [(s, hv) for s in range(unroll) for hv in range(2)]
        gc_cs = [gcols[s][:, hv:hv + 1] for s, hv in chains]
        decays = [jnp.where(causal, jnp.exp(jnp.minimum(gc_c - grows[s][hv:hv + 1, :], 0.0)), 0.0)
                  for gc_c, (s, hv) in zip(gc_cs, chains)]
        lmats = [jnp.where(strict, _mm_nt(kb_s[hv, sls[s], :], k_cs[s]) * dec, 0.0)
                 for dec, (s, hv) in zip(decays, chains)]
        pinvs = _unit_lower_inverse(lmats, eye, bd16, levels)
        uws = [_mm(pinv, rhs_s[hv, sls[s], :]) for pinv, (s, hv) in zip(pinvs, chains)]
        kds = [k_cs[s].astype(F32) * jnp.exp(gc_c[CHUNK - 1:CHUNK, :] - gc_c)
               for gc_c, (s, hv) in zip(gc_cs, chains)]
        kdws = [_mm_tn(kd, uw) for kd, uw in zip(kds, uws)]
        aws = [_mm(qks[s] * dec, uw) for dec, uw, (s, hv) in zip(decays, uws, chains)]
        for kdw, aw, (s, hv) in zip(kdws, aws, chains):
            sl = sls[s]
            a_s[hv, sl, :] = (-kdw[:, dk:2 * dk]).astype(BF16)
            b_s[hv, sl, :] = kdw[:, 0:dk]
            qp_s[hv, sl, :] = (qd_s[hv, sl, :].astype(F32) - aw[:, dk:2 * dk]).astype(BF16)
            oacc_s[hv, sl, :] = aw[:, 0:dk]
        return carry

    lax.fori_loop(0, nchunk // unroll, terms_body, 0)

    def recur_body(ci, states):
        sl = pl.ds(pl.multiple_of(ci * CHUNK, CHUNK), CHUNK)
        g_last = gc_ref[pl.ds(ci * CHUNK + CHUNK - 1, 1), :]
        new_states = []
        for hv in range(2):
            sb = states[hv].astype(BF16)
            oacc_s[hv, sl, :] += _mm(qp_s[hv, sl, :], sb)
            new_states.append(states[hv] * jnp.exp(g_last[:, hv:hv + 1])
                              + _mm(a_s[hv, sl, :], sb) + b_s[hv, sl, :])
        return tuple(new_states)

    zero = jnp.zeros((dk, dk), F32)
    lax.fori_loop(0, nchunk, recur_body, (zero, zero))

    z = z_ref[...].astype(F32)
    nw = nw_ref[...]
    for hv in range(2):
        o = _rms(oacc_s[hv], nw)
        zz = z[:, hv * dk:(hv + 1) * dk]
        o_ref[:, hv * dk:(hv + 1) * dk] = (o * (zz * _sigmoid(zz))).astype(o_ref.dtype)


def _gdn(proj3, small3, conv_w, prm, norm_w, *, qkv_off, z_off, n_qk, out_cols):
    bsz, seq, _ = proj3.shape
    assert seq % (CHUNK * GDN_CHUNKS_PER_TRIP) == 0
    dk = GDN_HEAD_DIM
    qb = qkv_off // dk
    kb = qb + n_qk
    vb = (qkv_off + 2 * n_qk * dk) // (2 * dk)
    zb = z_off // (2 * dk)
    cq = 0
    ck = n_qk
    cv = (2 * n_qk * dk) // (2 * dk)
    return pl.pallas_call(
        functools.partial(_gdn_kernel, n_vh=2 * n_qk),
        grid=(bsz, n_qk),
        in_specs=[
            pl.BlockSpec((None, seq, dk), lambda b, h: (b, 0, qb + h)),
            pl.BlockSpec((None, seq, dk), lambda b, h: (b, 0, kb + h)),
            pl.BlockSpec((None, seq, 2 * dk), lambda b, h: (b, 0, vb + h)),
            pl.BlockSpec((None, seq, 2 * dk), lambda b, h: (b, 0, zb + h)),
            pl.BlockSpec((None, seq, LANES), lambda b, h: (b, 0, 0)),
            pl.BlockSpec((CONV_WIDTH, dk), lambda b, h: (0, cq + h)),
            pl.BlockSpec((CONV_WIDTH, dk), lambda b, h: (0, ck + h)),
            pl.BlockSpec((CONV_WIDTH, 2 * dk), lambda b, h: (0, cv + h)),
            pl.BlockSpec((8, LANES), lambda b, h: (0, 0)),
            pl.BlockSpec((1, dk), lambda b, h: (0, 0)),
        ],
        out_specs=pl.BlockSpec((None, seq, 2 * dk), lambda b, h: (b, 0, h)),
        out_shape=jax.ShapeDtypeStruct((bsz, seq, out_cols), BF16),
        scratch_shapes=[
            pltpu.VMEM((seq, LANES), F32),
            pltpu.VMEM((LANES, seq), F32),
            pltpu.VMEM((seq, dk), BF16),
            pltpu.VMEM((seq, dk), BF16),
            pltpu.VMEM((2, seq, dk), BF16),
            pltpu.VMEM((2, seq, 2 * dk), BF16),
            pltpu.VMEM((2, seq, dk), BF16),
            pltpu.VMEM((2, seq, dk), BF16),
            pltpu.VMEM((2, seq, dk), F32),
            pltpu.VMEM((2, seq, dk), BF16),
            pltpu.VMEM((2, seq, dk), F32),
        ],
        compiler_params=_params(("parallel", "parallel")),
        name="gdn",
    )(proj3, proj3, proj3, proj3, small3, conv_w, conv_w, conv_w, prm, norm_w)


def _ssd_kernel(x_ref, b_ref, c_ref, z_ref, sm_ref, wx_ref, wb_ref, wc_ref, cbx_ref, cbb_ref, cbc_ref,
                prm_ref, drep_ref, nw_ref, o_ref,
                acs_ref, acst_ref, xs_s, xdt_s, b_s, c_s, st_s, y_s, *, dt_col0, heads_per_group):
    grp = pl.program_id(1)
    seq = x_ref.shape[0]
    hp = SSM_HEAD_DIM
    hg = heads_per_group
    nchunk = seq // CHUNK

    xs = _causal_conv_silu(x_ref[...].astype(F32), wx_ref[...], cbx_ref[...])
    bm = _causal_conv_silu(b_ref[...].astype(F32), wb_ref[...], cbb_ref[...])
    cm = _causal_conv_silu(c_ref[...].astype(F32), wc_ref[...], cbc_ref[...])
    xs_s[...] = xs
    b_s[...] = bm.astype(BF16)
    c_s[...] = cm.astype(BF16)

    sm = sm_ref[...]
    dt_all = _softplus(sm + prm_ref[1:2, :])
    a_all = -jnp.exp(prm_ref[0:1, :]) * dt_all
    r = lax.broadcasted_iota(jnp.int32, (LANES, LANES), 0)
    c = lax.broadcasted_iota(jnp.int32, (LANES, LANES), 1)
    sel = ((c < hg) & (r == dt_col0 + hg * grp + c)).astype(F32)
    dt_g = _mm_exact(dt_all, sel)
    a_g = _mm_exact(a_all, sel)
    _chunk_cumsum(a_g, acs_ref)
    acst_ref[...] = acs_ref[...].T
    for j in range(hg):
        xdt_s[:, j * hp:(j + 1) * hp] = (xs[:, j * hp:(j + 1) * hp] * dt_g[:, j:j + 1]).astype(BF16)
    st_s[...] = jnp.zeros_like(st_s)

    rr = lax.broadcasted_iota(jnp.int32, (CHUNK, CHUNK), 0)
    cc = lax.broadcasted_iota(jnp.int32, (CHUNK, CHUNK), 1)
    causal = rr >= cc

    def chunk_body(ci, carry):
        sl = pl.ds(pl.multiple_of(ci * CHUNK, CHUNK), CHUNK)
        b_c = b_s[sl, :]
        c_c = c_s[sl, :]
        cb = _mm_nt(c_c, b_c)
        acol = acs_ref[sl, :]
        arow = acst_ref[:, sl]
        heads = range(hg)
        acs = [acol[:, j:j + 1] for j in heads]
        xjs = [xdt_s[sl, j * hp:(j + 1) * hp] for j in heads]
        prevs = [st_s[j] for j in heads]
        y_offs = [_mm_nt(c_c, prev) for prev in prevs]
        y_diags = [_mm(cb * jnp.where(causal, jnp.exp(jnp.minimum(ac - arow[j:j + 1, :], 0.0)), 0.0), xj)
                   for j, ac, xj in zip(heads, acs, xjs)]
        st_news = [_mm_tn(xj.astype(F32) * jnp.exp(ac[CHUNK - 1:CHUNK, :] - ac), b_c)
                   for ac, xj in zip(acs, xjs)]
        for j, ac, prev, y_off, y_diag, st_new in zip(heads, acs, prevs, y_offs, y_diags, st_news):
            st_s[j] = prev * jnp.exp(ac[CHUNK - 1:CHUNK, :]) + st_new
            y_s[sl, j * hp:(j + 1) * hp] = y_diag + y_off * jnp.exp(ac)
        return carry

    lax.fori_loop(0, nchunk, chunk_body, 0)

    z = z_ref[...].astype(F32)
    y = (y_s[...] + drep_ref[...] * xs_s[...]) * (z * _sigmoid(z))
    o_ref[...] = _rms(y, nw_ref[...]).astype(o_ref.dtype)


def _ssd(proj3, small3, conv_w, conv_b, prm, d_rep, norm_w, *, xbc_off, z_off, inner, dt_col0):
    bsz, seq, _ = proj3.shape
    gw = inner // SSM_GROUPS
    hg = gw // SSM_HEAD_DIM
    ns = SSM_STATE
    xb = xbc_off // gw
    bb = (xbc_off + inner) // ns
    cb = bb + SSM_GROUPS
    zb = z_off // gw
    wbb = inner // ns
    wcb = wbb + SSM_GROUPS
    kern = functools.partial(_ssd_kernel, dt_col0=dt_col0, heads_per_group=hg)
    return pl.pallas_call(
        kern,
        grid=(bsz, SSM_GROUPS),
        in_specs=[
            pl.BlockSpec((None, seq, gw), lambda b, g: (b, 0, xb + g)),
            pl.BlockSpec((None, seq, ns), lambda b, g: (b, 0, bb + g)),
            pl.BlockSpec((None, seq, ns), lambda b, g: (b, 0, cb + g)),
            pl.BlockSpec((None, seq, gw), lambda b, g: (b, 0, zb + g)),
            pl.BlockSpec((None, seq, LANES), lambda b, g: (b, 0, 0)),
            pl.BlockSpec((CONV_WIDTH, gw), lambda b, g: (0, g)),
            pl.BlockSpec((CONV_WIDTH, ns), lambda b, g: (0, wbb + g)),
            pl.BlockSpec((CONV_WIDTH, ns), lambda b, g: (0, wcb + g)),
            pl.BlockSpec((1, gw), lambda b, g: (0, g)),
            pl.BlockSpec((1, ns), lambda b, g: (0, wbb + g)),
            pl.BlockSpec((1, ns), lambda b, g: (0, wcb + g)),
            pl.BlockSpec((8, LANES), lambda b, g: (0, 0)),
            pl.BlockSpec((1, gw), lambda b, g: (0, g)),
            pl.BlockSpec((1, gw), lambda b, g: (0, g)),
        ],
        out_specs=pl.BlockSpec((None, seq, gw), lambda b, g: (b, 0, g)),
        out_shape=jax.ShapeDtypeStruct((bsz, seq, inner), BF16),
        scratch_shapes=[
            pltpu.VMEM((seq, LANES), F32),
            pltpu.VMEM((LANES, seq), F32),
            pltpu.VMEM((seq, gw), F32),
            pltpu.VMEM((seq, gw), BF16),
            pltpu.VMEM((seq, ns), BF16),
            pltpu.VMEM((seq, ns), BF16),
            pltpu.VMEM((hg, SSM_HEAD_DIM, ns), F32),
            pltpu.VMEM((seq, gw), F32),
        ],
        compiler_params=_params(("parallel", "parallel")),
        name="ssd",
    )(proj3, proj3, proj3, proj3, small3, conv_w, conv_w, conv_w, conv_b, conv_b, conv_b,
      prm, d_rep, norm_w)


def _out_proj_kernel(a_ref, b_ref, wa_ref, wb_ref, x_ref, o_ref):
    acc = jnp.dot(a_ref[...], wa_ref[...], preferred_element_type=F32)
    acc = acc + jnp.dot(b_ref[...], wb_ref[...], preferred_element_type=F32)
    o_ref[...] = x_ref[...] + acc


def _out_proj(o_a, o_b, w_a, w_b, x2, tm=512, tn=1024):
    tok, d = x2.shape
    ka = o_a.shape[1]
    kb = o_b.shape[1]
    return pl.pallas_call(
        _out_proj_kernel,
        grid=(tok // tm, d // tn),
        in_specs=[
            pl.BlockSpec((tm, ka), lambda i, j: (i, 0)),
            pl.BlockSpec((tm, kb), lambda i, j: (i, 0)),
            pl.BlockSpec((ka, tn), lambda i, j: (0, j)),
            pl.BlockSpec((kb, tn), lambda i, j: (0, j)),
            pl.BlockSpec((tm, tn), lambda i, j: (i, j)),
        ],
        out_specs=pl.BlockSpec((tm, tn), lambda i, j: (i, j)),
        out_shape=jax.ShapeDtypeStruct((tok, d), F32),
        compiler_params=_params(("parallel", "parallel")),
        name="out_proj",
    )(o_a, o_b, w_a, w_b, x2)


def _router_kernel(h_ref, nw_ref, wr_ref, br_ref, u_ref, idx_ref, gate_ref, rank_ref, cnt_ref, carry_s):
    step = pl.program_id(0)
    tm = h_ref.shape[0]

    @pl.when(step == 0)
    def _():
        carry_s[...] = jnp.zeros_like(carry_s)

    u = _rms(h_ref[...], nw_ref[...])
    u_ref[...] = u
    logits = _mm_exact(u, wr_ref[...]) + br_ref[...]
    lane = lax.broadcasted_iota(jnp.int32, (tm, LANES), 1).astype(F32)
    work = logits
    vals, idxs = [], []
    for _ in range(TOP_K):
        m = jnp.max(work, axis=1, keepdims=True)
        idx = jnp.min(jnp.where(work == m, lane, float(LANES)), axis=1, keepdims=True)
        vals.append(m)
        idxs.append(idx)
        work = jnp.where(lane == idx, -jnp.inf, work)
    onehot = (work == -jnp.inf).astype(F32)
    exps = [jnp.exp(v - vals[0]) for v in vals]
    denom = exps[0]
    for e in exps[1:]:
        denom = denom + e
    r = lax.broadcasted_iota(jnp.int32, (tm, tm), 0)
    c = lax.broadcasted_iota(jnp.int32, (tm, tm), 1)
    below = (r > c).astype(BF16)
    excl = jnp.dot(below, onehot.astype(BF16), preferred_element_type=F32) + carry_s[...]
    idx_out = jnp.zeros((tm, LANES), F32)
    gate_out = jnp.zeros((tm, LANES), F32)
    rank_out = jnp.zeros((tm, LANES), F32)
    for kk in range(TOP_K):
        rank_k = jnp.sum(jnp.where(lane == idxs[kk], excl, 0.0), axis=1, keepdims=True)
        idx_out = jnp.where(lane == kk, idxs[kk], idx_out)
        gate_out = jnp.where(lane == kk, exps[kk] / denom, gate_out)
        rank_out = jnp.where(lane == kk, rank_k, rank_out)
    idx_ref[...] = idx_out.astype(jnp.int32)
    gate_ref[...] = gate_out
    rank_ref[...] = rank_out.astype(jnp.int32)
    total = carry_s[...] + jnp.sum(onehot, axis=0, keepdims=True)
    carry_s[...] = total
    cnt_ref[...] = jnp.broadcast_to(total, cnt_ref.shape)


def _router(h2d, norm_w, w_r, b_r, tm=512):
    tok, d = h2d.shape
    return pl.pallas_call(
        _router_kernel,
        grid=(tok // tm,),
        in_specs=[
            pl.BlockSpec((tm, d), lambda i: (i, 0)),
            pl.BlockSpec((1, d), lambda i: (0, 0)),
            pl.BlockSpec((d, LANES), lambda i: (0, 0)),
            pl.BlockSpec((1, LANES), lambda i: (0, 0)),
        ],
        out_specs=[
            pl.BlockSpec((tm, d), lambda i: (i, 0)),
            pl.BlockSpec((tm, LANES), lambda i: (i, 0)),
            pl.BlockSpec((tm, LANES), lambda i: (i, 0)),
            pl.BlockSpec((tm, LANES), lambda i: (i, 0)),
            pl.BlockSpec((8, LANES), lambda i: (0, 0)),
        ],
        out_shape=[
            jax.ShapeDtypeStruct((tok, d), F32),
            jax.ShapeDtypeStruct((tok, LANES), jnp.int32),
            jax.ShapeDtypeStruct((tok, LANES), F32),
            jax.ShapeDtypeStruct((tok, LANES), jnp.int32),
            jax.ShapeDtypeStruct((8, LANES), F32),
        ],
        scratch_shapes=[pltpu.VMEM((1, LANES), F32)],
        compiler_params=_params(("arbitrary",)),
        name="router",
    )(h2d, norm_w, w_r, b_r)


def _dispatch_kernel(pos_ref, u_ref, xs_in, xs_hbm, sem, *, tm):
    del xs_in

    def row_copy(t, k):
        return pltpu.make_async_copy(u_ref.at[pl.ds(t, 1)],
                                     xs_hbm.at[pl.ds(pos_ref[t * TOP_K + k], 1)], sem)

    def issue(t, carry):
        for k in range(TOP_K):
            row_copy(t, k).start()
        return carry

    def drain(t, carry):
        for k in range(TOP_K):
            row_copy(t, k).wait()
        return carry

    lax.fori_loop(0, tm, issue, 0)
    lax.fori_loop(0, tm, drain, 0)


def _dispatch(pos_flat, u2d, xs_init, tm=256):
    tok, d = u2d.shape
    kern = functools.partial(_dispatch_kernel, tm=tm)
    return pl.pallas_call(
        kern,
        grid=(tok // tm,),
        in_specs=[
            pl.BlockSpec((tm * TOP_K,), lambda i: (i,), memory_space=pltpu.SMEM),
            pl.BlockSpec((tm, d), lambda i: (i, 0)),
            pl.BlockSpec(memory_space=pl.ANY),
        ],
        out_specs=pl.BlockSpec(memory_space=pl.ANY),
        out_shape=jax.ShapeDtypeStruct(xs_init.shape, xs_init.dtype),
        scratch_shapes=[pltpu.SemaphoreType.DMA(())],
        input_output_aliases={2: 0},
        compiler_params=_params(("arbitrary",)),
        name="dispatch",
    )(pos_flat, u2d, xs_init)


def _moe_kernel(bexp_ref, nused_ref, x_ref, w1g_ref, w1l_ref, b1g_ref, b1l_ref, w2_ref, b2_ref,
                o_ref, xb_s):
    i = pl.program_id(0)
    j = pl.program_id(1)

    @pl.when(i < nused_ref[0])
    def _():
        @pl.when(j == 0)
        def _():
            xb_s[...] = x_ref[...].astype(BF16)

        x = xb_s[...]
        hg = jnp.dot(x, w1g_ref[...].astype(BF16), preferred_element_type=F32) + b1g_ref[...]
        hl = jnp.dot(x, w1l_ref[...].astype(BF16), preferred_element_type=F32) + b1l_ref[...]
        glu = jnp.minimum(hg, SWIGLU_LIMIT)
        lin = jnp.clip(hl, -SWIGLU_LIMIT, SWIGLU_LIMIT)
        act = glu * _sigmoid(SWIGLU_ALPHA * glu) * (lin + 1.0)
        y = jnp.dot(act.astype(BF16), w2_ref[...].astype(BF16), preferred_element_type=F32)

        @pl.when(j == 0)
        def _():
            o_ref[...] = y + b2_ref[...]

        @pl.when(j > 0)
        def _():
            o_ref[...] += y

    @pl.when((i >= nused_ref[0]) & (j == 0))
    def _():
        o_ref[...] = jnp.zeros_like(o_ref)


def _moe(bexp, nused, xs, w1, b1, w2, b2, *, bm, tf):
    rows, d = xs.shape
    ff = w2.shape[1]
    nb = rows // bm
    nf = ff // tf

    def jj(i, j, nu):
        return jnp.where(i < nu[0], j, nf - 1)

    grid_spec = pltpu.PrefetchScalarGridSpec(
        num_scalar_prefetch=2,
        grid=(nb, nf),
        in_specs=[
            pl.BlockSpec((bm, d), lambda i, j, be, nu: (i, 0)),
            pl.BlockSpec((None, d, tf), lambda i, j, be, nu: (be[i], 0, jj(i, j, nu))),
            pl.BlockSpec((None, d, tf), lambda i, j, be, nu: (be[i], 0, nf + jj(i, j, nu))),
            pl.BlockSpec((None, 1, tf), lambda i, j, be, nu: (be[i], 0, jj(i, j, nu))),
            pl.BlockSpec((None, 1, tf), lambda i, j, be, nu: (be[i], 0, nf + jj(i, j, nu))),
            pl.BlockSpec((None, tf, d), lambda i, j, be, nu: (be[i], jj(i, j, nu), 0)),
            pl.BlockSpec((None, 1, d), lambda i, j, be, nu: (be[i], 0, 0)),
        ],
        out_specs=pl.BlockSpec((bm, d), lambda i, j, be, nu: (i, 0)),
        scratch_shapes=[pltpu.VMEM((bm, d), BF16)],
    )
    return pl.pallas_call(
        _moe_kernel,
        grid_spec=grid_spec,
        out_shape=jax.ShapeDtypeStruct((rows, d), F32),
        compiler_params=_params(("arbitrary", "arbitrary")),
        name="moe",
    )(bexp, nused, xs, w1, w1, b1, b1, w2, b2)


def _combine_kernel(pos_ref, gate_ref, h_ref, ys_hbm, p_ref, pn_ref, wg_ref, wp_ref, fn_ref, o_ref,
                    gbuf, sem, *, tm):
    def row_copy(t, k):
        return pltpu.make_async_copy(ys_hbm.at[pl.ds(pos_ref[t * TOP_K + k], 1)],
                                     gbuf.at[k, pl.ds(t, 1)], sem)

    def issue(t, carry):
        for k in range(TOP_K):
            row_copy(t, k).start()
        return carry

    def drain(t, carry):
        for k in range(TOP_K):
            row_copy(t, k).wait()
        return carry

    lax.fori_loop(0, tm, issue, 0)
    lax.fori_loop(0, tm, drain, 0)

    gate = gate_ref[...]
    h = h_ref[...]
    for k in range(TOP_K):
        h = h + gate[:, k:k + 1] * gbuf[k]
    u = _rms(h, pn_ref[...]).astype(BF16)
    gv = _sigmoid(jnp.dot(u, wg_ref[...], preferred_element_type=F32))
    pp = jnp.dot(p_ref[...].astype(BF16), wp_ref[...], preferred_element_type=F32)
    h = h + pp * gv
    o_ref[...] = _rms(h, fn_ref[...])


def _combine(pos_flat, gate, h2d, ys, p2d, ple_norm, w_gate, w_proj, final_norm, tm=256):
    tok, d = h2d.shape
    pd = p2d.shape[1]
    kern = functools.partial(_combine_kernel, tm=tm)
    return pl.pallas_call(
        kern,
        grid=(tok // tm,),
        in_specs=[
            pl.BlockSpec((tm * TOP_K,), lambda i: (i,), memory_space=pltpu.SMEM),
            pl.BlockSpec((tm, LANES), lambda i: (i, 0)),
            pl.BlockSpec((tm, d), lambda i: (i, 0)),
            pl.BlockSpec(memory_space=pl.ANY),
            pl.BlockSpec((tm, pd), lambda i: (i, 0)),
            pl.BlockSpec((1, d), lambda i: (0, 0)),
            pl.BlockSpec((d, d), lambda i: (0, 0)),
            pl.BlockSpec((pd, d), lambda i: (0, 0)),
            pl.BlockSpec((1, d), lambda i: (0, 0)),
        ],
        out_specs=pl.BlockSpec((tm, d), lambda i: (i, 0)),
        out_shape=jax.ShapeDtypeStruct((tok, d), F32),
        scratch_shapes=[pltpu.VMEM((TOP_K, tm, d), F32), pltpu.SemaphoreType.DMA(())],
        compiler_params=_params(("arbitrary",)),
        name="combine",
    )(pos_flat, gate, h2d, ys, p2d, ple_norm, w_gate, w_proj, final_norm)


def _pad_cols(v, off):
    return jnp.pad(v.astype(F32), (off, LANES - off - v.shape[0]))


def _layer(h3, p3, mix_norm, w_in, gdn_conv_w, gdn_a_log, gdn_dt_bias, gdn_norm_w, ssm_conv_w,
           ssm_conv_b, ssm_a_log, ssm_dt_bias, ssm_d, ssm_norm_w, w_out, ffn_norm, w_router, b_router,
           w_mlp1, b_mlp1, w_mlp2, b_mlp2, ple_norm, w_ple_gate, w_ple_proj, final_norm, *, moe_bm, moe_tf):
    bsz, seq, d = h3.shape
    tok = bsz * seq
    n_vh = gdn_a_log.shape[0]
    n_qk = n_vh // 2
    key_dim = n_qk * GDN_HEAD_DIM
    val_dim = n_vh * GDN_HEAD_DIM
    gdn_conv_ch = 2 * key_dim + val_dim
    ssm_heads = ssm_a_log.shape[0]
    inner = ssm_heads * SSM_HEAD_DIM
    ssm_conv_ch = inner + 2 * SSM_GROUPS * SSM_STATE

    o0 = 0
    o_qkv, o0 = o0, o0 + gdn_conv_ch
    o_z, o0 = o0, o0 + val_dim
    o_b, o0 = o0, o0 + n_vh
    o_a, o0 = o0, o0 + n_vh
    o_xbc, o0 = o0, o0 + ssm_conv_ch
    o_sz, o0 = o0, o0 + inner
    o_dt, o0 = o0, o0 + ssm_heads
    w_main = jnp.concatenate([w_in[:, o_qkv:o_b], w_in[:, o_xbc:o_dt]], axis=1).astype(BF16)
    w_small = jnp.concatenate([w_in[:, o_b:o_xbc], w_in[:, o_dt:o0]], axis=1)
    n_small = w_small.shape[1]
    w_small = jnp.pad(w_small, ((0, 0), (0, LANES - n_small))).astype(BF16)
    m_qkv, m_z = 0, gdn_conv_ch
    m_xbc = gdn_conv_ch + val_dim
    m_sz = m_xbc + ssm_conv_ch
    dt_col0 = 2 * n_vh

    x2 = h3.reshape(tok, d)
    proj, small = _in_proj(x2, mix_norm.reshape(1, d), w_main, w_small)
    proj3 = proj.reshape(bsz, seq, proj.shape[1])
    small3 = small.reshape(bsz, seq, LANES)

    zrow = jnp.zeros((LANES,), F32)
    gdn_prm = jnp.stack([_pad_cols(gdn_a_log, n_vh), _pad_cols(gdn_dt_bias, n_vh)] + [zrow] * 6)
    o_gdn = _gdn(proj3, small3, gdn_conv_w, gdn_prm, gdn_norm_w.reshape(1, GDN_HEAD_DIM),
                 qkv_off=m_qkv, z_off=m_z, n_qk=n_qk, out_cols=val_dim)

    ssm_prm = jnp.stack([_pad_cols(ssm_a_log, dt_col0), _pad_cols(ssm_dt_bias, dt_col0)] + [zrow] * 6)
    d_rep = jnp.repeat(ssm_d.astype(F32), SSM_HEAD_DIM).reshape(1, inner)
    o_ssm = _ssd(proj3, small3, ssm_conv_w, ssm_conv_b.reshape(1, ssm_conv_ch), ssm_prm, d_rep,
                 ssm_norm_w.reshape(1, inner), xbc_off=m_xbc, z_off=m_sz, inner=inner, dt_col0=dt_col0)

    w_out_b = w_out.astype(BF16)
    h1 = _out_proj(o_gdn.reshape(tok, val_dim), o_ssm.reshape(tok, inner),
                   w_out_b[:val_dim], w_out_b[val_dim:], x2)

    n_exp = w_router.shape[1]
    w_r = jnp.pad(w_router, ((0, 0), (0, LANES - n_exp)))
    b_r = jnp.pad(b_router.astype(F32), (0, LANES - n_exp), constant_values=NEG_BIG).reshape(1, LANES)
    u2, idx_l, gate_l, rank_l, cnt = _router(h1, ffn_norm.reshape(1, d), w_r, b_r)

    counts = cnt[0, :n_exp].astype(jnp.int32)
    padded = ((counts + moe_bm - 1) // moe_bm) * moe_bm
    pad_end = jnp.cumsum(padded)
    pad_start = pad_end - padded
    top_idx = idx_l[:, :TOP_K]
    pos = (pad_start[top_idx] + rank_l[:, :TOP_K]).astype(jnp.int32).reshape(-1)
    n_blocks = (tok * TOP_K) // moe_bm + n_exp
    block_start = jnp.arange(n_blocks, dtype=jnp.int32) * moe_bm
    bexp = jnp.minimum(jnp.searchsorted(pad_end, block_start, side='right'), n_exp - 1).astype(jnp.int32)
    nused = (pad_end[-1] // moe_bm).astype(jnp.int32).reshape(1)

    xs = _dispatch(pos, u2, jnp.zeros((n_blocks * moe_bm, d), F32))
    ff = w_mlp2.shape[1]
    ys = _moe(bexp, nused, xs, w_mlp1, b_mlp1.reshape(n_exp, 1, 2 * ff), w_mlp2,
              b_mlp2.reshape(n_exp, 1, d), bm=moe_bm, tf=moe_tf)

    out = _combine(pos, gate_l, h1, ys, p3.reshape(tok, p3.shape[-1]), ple_norm.reshape(1, d),
                   w_ple_gate.astype(BF16), w_ple_proj.astype(BF16), final_norm.reshape(1, d))
    return out.reshape(bsz, seq, d)


def kernel(x, p, mix_norm, w_in, gdn_conv_w, gdn_a_log, gdn_dt_bias, gdn_norm_w, ssm_conv_w, ssm_conv_b,
           ssm_a_log, ssm_dt_bias, ssm_d, ssm_norm_w, w_out, ffn_norm, w_router, b_router, w_mlp1, b_mlp1,
           w_mlp2, b_mlp2, ple_norm, w_ple_gate, w_ple_proj, final_norm):
    assert mix_norm.shape[0] == 1, "single-layer trunk"
    return _layer(x, p[0], mix_norm[0], w_in[0], gdn_conv_w[0], gdn_a_log[0], gdn_dt_bias[0], gdn_norm_w[0],
                  ssm_conv_w[0], ssm_conv_b[0], ssm_a_log[0], ssm_dt_bias[0], ssm_d[0], ssm_norm_w[0],
                  w_out[0], ffn_norm[0], w_router[0], b_router[0], w_mlp1[0], b_mlp1[0], w_mlp2[0],
                  b_mlp2[0], ple_norm[0], w_ple_gate[0], w_ple_proj[0], final_norm, moe_bm=512, moe_tf=256)
```

```python
import functools

import jax
import jax.numpy as jnp
from jax import lax
from jax.experimental import pallas as pl
from jax.experimental.pallas import tpu as pltpu

F32 = jnp.float32
BF16 = jnp.bfloat16
HIGHEST = lax.Precision.HIGHEST

NORM_EPS = 1e-6
CONV_WIDTH = 4
GDN_HEAD_DIM = 128
SSM_HEAD_DIM = 64
SSM_GROUPS = 4
SSM_STATE = 128
N_EXPERTS = 32
TOP_K = 4
SWIGLU_ALPHA = 1.702
SWIGLU_LIMIT = 7.0

LANES = 128
CHUNK = 128
GDN_CHUNKS_PER_TRIP = 4
VMEM_LIMIT = 56 * 1024 * 1024

NEG_BIG = -1e30


def _params(sem, vmem=VMEM_LIMIT):
    return pltpu.CompilerParams(dimension_semantics=sem, vmem_limit_bytes=vmem)


def _rms(x, w):
    ms = jnp.mean(x * x, axis=-1, keepdims=True)
    return x * lax.rsqrt(ms + NORM_EPS) * w


def _sigmoid(x):
    return 0.5 * jnp.tanh(0.5 * x) + 0.5


def _softplus(x):
    return jnp.maximum(x, 0.0) + jnp.log(1.0 + jnp.exp(-jnp.abs(x)))


def _mm(a, b):
    return jnp.dot(a.astype(BF16), b.astype(BF16), preferred_element_type=F32)


def _mm_nt(a, b):
    return lax.dot_general(a.astype(BF16), b.astype(BF16), (((1,), (1,)), ((), ())),
                           preferred_element_type=F32)


def _mm_tn(a, b):
    return lax.dot_general(a.astype(BF16), b.astype(BF16), (((0,), (0,)), ((), ())),
                           preferred_element_type=F32)


def _mm_exact(a, b):
    return jnp.dot(a, b, precision=HIGHEST, preferred_element_type=F32)


def _pack_bf16_pairs(x):
    n = x.shape[1] // 2
    xb = x.astype(BF16).astype(F32)
    lo = lax.bitcast_convert_type(xb[:, :n], jnp.uint32)
    hi = lax.bitcast_convert_type(xb[:, n:], jnp.uint32)
    return (lo >> 16) | (hi & jnp.uint32(0xFFFF0000))


def _unpack_bf16_pairs(w):
    lo = lax.bitcast_convert_type(w << 16, F32)
    hi = lax.bitcast_convert_type(w & jnp.uint32(0xFFFF0000), F32)
    return lo, hi


def _silu(x):
    h = 0.5 * x
    return h * jnp.tanh(h) + h


def _causal_conv_silu(x, w, bias, out_ref):
    def conv(xv, masked):
        acc = xv * w[CONV_WIDTH - 1:CONV_WIDTH, :]
        for k in range(CONV_WIDTH - 1):
            shift = CONV_WIDTH - 1 - k
            xs = pltpu.roll(xv, shift, axis=0)
            if masked:
                row = lax.broadcasted_iota(jnp.int32, xv.shape, 0)
                xs = jnp.where(row >= shift, xs, 0.0)
            acc = acc + xs * w[k:k + 1, :]
        if bias is not None:
            acc = acc + bias
        return _silu(acc)

    out_ref[...] = conv(x, False).astype(out_ref.dtype)
    out_ref[0:8, :] = conv(x[0:8, :], True).astype(out_ref.dtype)


def _bf16_pieces(x):
    hi = x.astype(BF16)
    r1 = x - hi.astype(F32)
    mid = r1.astype(BF16)
    lo = (r1 - mid.astype(F32)).astype(BF16)
    return hi, mid, lo


def _select_cols(vals, emat):
    out = None
    for piece in _bf16_pieces(vals):
        t = jnp.dot(piece, emat, preferred_element_type=F32)
        out = t if out is None else out + t
    return out


def _chunk_cumsum(vals, out_ref):
    seq = vals.shape[0]
    r = lax.broadcasted_iota(jnp.int32, (CHUNK, CHUNK), 0)
    c = lax.broadcasted_iota(jnp.int32, (CHUNK, CHUNK), 1)
    tri = (r >= c).astype(BF16)
    pieces = _bf16_pieces(vals)
    for i in range(seq // CHUNK):
        rows = slice(i * CHUNK, (i + 1) * CHUNK)
        acc = None
        for piece in pieces:
            t = jnp.dot(tri, piece[rows, :], preferred_element_type=F32)
            acc = t if acc is None else acc + t
        out_ref[rows, :] = acc


def _in_proj_kernel(x_ref, g_ref, w_ref, ws_ref, o_ref, os_ref, u_ref):
    @pl.when(pl.program_id(1) == 0)
    def _():
        u = _rms(x_ref[...], g_ref[...]).astype(BF16)
        u_ref[...] = u
        os_ref[...] = jnp.dot(u, ws_ref[...], preferred_element_type=F32)

    o_ref[...] = jnp.dot(u_ref[...], w_ref[...], preferred_element_type=F32).astype(o_ref.dtype)


def _in_proj(x2, norm_w, w_main, w_small, tm=512, tn=1024):
    tok, d = x2.shape
    n = w_main.shape[1]
    return pl.pallas_call(
        _in_proj_kernel,
        grid=(tok // tm, n // tn),
        in_specs=[
            pl.BlockSpec((tm, d), lambda i, j: (i, 0)),
            pl.BlockSpec((1, d), lambda i, j: (0, 0)),
            pl.BlockSpec((d, tn), lambda i, j: (0, j)),
            pl.BlockSpec((d, LANES), lambda i, j: (0, 0)),
        ],
        out_specs=[
            pl.BlockSpec((tm, tn), lambda i, j: (i, j)),
            pl.BlockSpec((tm, LANES), lambda i, j: (i, 0)),
        ],
        out_shape=[
            jax.ShapeDtypeStruct((tok, n), BF16),
            jax.ShapeDtypeStruct((tok, LANES), F32),
        ],
        scratch_shapes=[pltpu.VMEM((tm, d), BF16)],
        compiler_params=_params(("parallel", "arbitrary")),
        name="in_proj",
    )(x2, norm_w, w_main, w_small)


def _tri_masks(n):
    r = lax.broadcasted_iota(jnp.int32, (n, n), 0)
    c = lax.broadcasted_iota(jnp.int32, (n, n), 1)
    same = lambda s: (r // s) == (c // s)
    bd16 = same(16).astype(F32)
    levels = []
    s = 16
    while s < n:
        levels.append((same(2 * s) & jnp.logical_not(same(s))).astype(F32))
        s *= 2
    return bd16, levels


def _unit_lower_inverse(lmats, eye, bd16, levels):
    lds = [l * bd16 for l in lmats]
    ps = [eye - ld for ld in lds]
    ms = [_mm(ld, ld) for ld in lds]
    for it in range(3):
        ps = [p + _mm(p, m) for p, m in zip(ps, ms)]
        if it < 2:
            ms = [_mm(m, m) for m in ms]
    for msk in levels:
        ts = [_mm(l * msk, p) for l, p in zip(lmats, ps)]
        ps = [p - _mm(p, t) for p, t in zip(ps, ts)]
    return ps


def _gdn_kernel(q_ref, k_ref, v_ref, z_ref, sm_ref, wq_ref, wk_ref, wv_ref, prm_ref, nw_ref,
                o_ref, gc_ref, gct_ref, cv_s, q_s, k_s, kb_s, rhs_s, qd_s, a_s, b_s, qp_s, oacc_s, *, n_vh):
    hq = pl.program_id(1)
    seq = q_ref.shape[0]
    dk = GDN_HEAD_DIM
    nchunk = seq // CHUNK

    _causal_conv_silu(q_ref[...].astype(F32), wq_ref[...], None, cv_s.at[:, 0:dk])
    q = cv_s[:, 0:dk]
    _causal_conv_silu(k_ref[...].astype(F32), wk_ref[...], None, cv_s.at[:, 0:dk])
    k = cv_s[:, 0:dk]
    _causal_conv_silu(v_ref[...].astype(F32), wv_ref[...], None, cv_s)
    v = cv_s[...]
    q = q * lax.rsqrt(jnp.sum(q * q, axis=-1, keepdims=True) + NORM_EPS) * (dk ** -0.5)
    k = k * lax.rsqrt(jnp.sum(k * k, axis=-1, keepdims=True) + NORM_EPS)
    q_s[...] = q.astype(BF16)
    k_s[...] = k.astype(BF16)

    sm = sm_ref[...]
    beta_all = _sigmoid(sm)
    g_all = -jnp.exp(prm_ref[0:1, :]) * _softplus(sm + prm_ref[1:2, :])
    r = lax.broadcasted_iota(jnp.int32, (LANES, LANES), 0)
    c = lax.broadcasted_iota(jnp.int32, (LANES, LANES), 1)
    sel_g = ((c < 2) & (r == n_vh + 2 * hq + c)).astype(BF16)
    sel_b = ((c < 2) & (r == 2 * hq + c)).astype(BF16)
    g_sel = _select_cols(g_all, sel_g)
    beta = _select_cols(beta_all, sel_b)
    _chunk_cumsum(g_sel, gc_ref)
    gc = gc_ref[...]
    gct_ref[...] = gc.T
    eg = jnp.exp(gc)
    for hv in range(2):
        bcol = beta[:, hv:hv + 1]
        ecol = eg[:, hv:hv + 1]
        kb = k * bcol
        kb_s[hv] = kb.astype(BF16)
        rhs_s[hv, :, 0:dk] = (v[:, hv * dk:(hv + 1) * dk] * bcol).astype(BF16)
        rhs_s[hv, :, dk:2 * dk] = (kb * ecol).astype(BF16)
        qd_s[hv] = (q * ecol).astype(BF16)

    rr = lax.broadcasted_iota(jnp.int32, (CHUNK, CHUNK), 0)
    cc = lax.broadcasted_iota(jnp.int32, (CHUNK, CHUNK), 1)
    causal = rr >= cc
    strict = rr > cc
    eye = (rr == cc).astype(F32)
    bd16, levels = _tri_masks(CHUNK)

    unroll = GDN_CHUNKS_PER_TRIP

    def terms_body(ci, carry):
        sls = [pl.ds(pl.multiple_of((ci * unroll + s) * CHUNK, CHUNK), CHUNK) for s in range(unroll)]
        k_cs = [k_s[sl, :] for sl in sls]
        qks = [_mm_nt(q_s[sl, :], k_c) for sl, k_c in zip(sls, k_cs)]
        gcols = [gc_ref[sl, :] for sl in sls]
        grows = [gct_ref[:, sl] for sl in sls]
        chains = [(s, hv) for s in range(unroll) for hv in range(2)]
        gc_cs = [gcols[s][:, hv:hv + 1] for s, hv in chains]
        decays = [jnp.where(causal, jnp.exp(jnp.minimum(gc_c - grows[s][hv:hv + 1, :], 0.0)), 0.0)
                  for gc_c, (s, hv) in zip(gc_cs, chains)]
        lmats = [jnp.where(strict, _mm_nt(kb_s[hv, sls[s], :], k_cs[s]) * dec, 0.0)
                 for dec, (s, hv) in zip(decays, chains)]
        pinvs = _unit_lower_inverse(lmats, eye, bd16, levels)
        uws = [_mm(pinv, rhs_s[hv, sls[s], :]) for pinv, (s, hv) in zip(pinvs, chains)]
        kds = [k_cs[s].astype(F32) * jnp.exp(gc_c[CHUNK - 1:CHUNK, :] - gc_c)
               for gc_c, (s, hv) in zip(gc_cs, chains)]
        kdws = [_mm_tn(kd, uw) for kd, uw in zip(kds, uws)]
        aws = [_mm(qks[s] * dec, uw) for dec, uw, (s, hv) in zip(decays, uws, chains)]
        for kdw, aw, (s, hv) in zip(kdws, aws, chains):
            sl = sls[s]
            a_s[hv, sl, :] = (-kdw[:, dk:2 * dk]).astype(BF16)
            b_s[hv, sl, :] = kdw[:, 0:dk]
            qp_s[hv, sl, :] = (qd_s[hv, sl, :].astype(F32) - aw[:, dk:2 * dk]).astype(BF16)
            oacc_s[hv, sl, :] = aw[:, 0:dk]
        return carry

    lax.fori_loop(0, nchunk // unroll, terms_body, 0)

    def recur_body(ci, states):
        sl = pl.ds(pl.multiple_of(ci * CHUNK, CHUNK), CHUNK)
        g_last = gc_ref[pl.ds(ci * CHUNK + CHUNK - 1, 1), :]
        new_states = []
        for hv in range(2):
            sb = states[hv].astype(BF16)
            oacc_s[hv, sl, :] += _mm(qp_s[hv, sl, :], sb)
            new_states.append(states[hv] * jnp.exp(g_last[:, hv:hv + 1])
                              + _mm(a_s[hv, sl, :], sb) + b_s[hv, sl, :])
        return tuple(new_states)

    zero = jnp.zeros((dk, dk), F32)
    lax.fori_loop(0, nchunk, recur_body, (zero, zero))

    z = z_ref[...].astype(F32)
    nw = nw_ref[...]
    for hv in range(2):
        o = _rms(oacc_s[hv], nw)
        zz = z[:, hv * dk:(hv + 1) * dk]
        o_ref[:, hv * dk:(hv + 1) * dk] = (o * _silu(zz)).astype(o_ref.dtype)


def _gdn(proj3, small3, conv_w, prm, norm_w, *, qkv_off, z_off, n_qk, out_cols):
    bsz, seq, _ = proj3.shape
    assert seq % (CHUNK * GDN_CHUNKS_PER_TRIP) == 0
    dk = GDN_HEAD_DIM
    qb = qkv_off // dk
    kb = qb + n_qk
    vb = (qkv_off + 2 * n_qk * dk) // (2 * dk)
    zb = z_off // (2 * dk)
    cq = 0
    ck = n_qk
    cv = (2 * n_qk * dk) // (2 * dk)
    return pl.pallas_call(
        functools.partial(_gdn_kernel, n_vh=2 * n_qk),
        grid=(bsz, n_qk),
        in_specs=[
            pl.BlockSpec((None, seq, dk), lambda b, h: (b, 0, qb + h)),
            pl.BlockSpec((None, seq, dk), lambda b, h: (b, 0, kb + h)),
            pl.BlockSpec((None, seq, 2 * dk), lambda b, h: (b, 0, vb + h)),
            pl.BlockSpec((None, seq, 2 * dk), lambda b, h: (b, 0, zb + h)),
            pl.BlockSpec((None, seq, LANES), lambda b, h: (b, 0, 0)),
            pl.BlockSpec((CONV_WIDTH, dk), lambda b, h: (0, cq + h)),
            pl.BlockSpec((CONV_WIDTH, dk), lambda b, h: (0, ck + h)),
            pl.BlockSpec((CONV_WIDTH, 2 * dk), lambda b, h: (0, cv + h)),
            pl.BlockSpec((8, LANES), lambda b, h: (0, 0)),
            pl.BlockSpec((1, dk), lambda b, h: (0, 0)),
        ],
        out_specs=pl.BlockSpec((None, seq, 2 * dk), lambda b, h: (b, 0, h)),
        out_shape=jax.ShapeDtypeStruct((bsz, seq, out_cols), BF16),
        scratch_shapes=[
            pltpu.VMEM((seq, LANES), F32),
            pltpu.VMEM((LANES, seq), F32),
            pltpu.VMEM((seq, 2 * dk), F32),
            pltpu.VMEM((seq, dk), BF16),
            pltpu.VMEM((seq, dk), BF16),
            pltpu.VMEM((2, seq, dk), BF16),
            pltpu.VMEM((2, seq, 2 * dk), BF16),
            pltpu.VMEM((2, seq, dk), BF16),
            pltpu.VMEM((2, seq, dk), BF16),
            pltpu.VMEM((2, seq, dk), F32),
            pltpu.VMEM((2, seq, dk), BF16),
            pltpu.VMEM((2, seq, dk), F32),
        ],
        compiler_params=_params(("parallel", "parallel")),
        name="gdn",
    )(proj3, proj3, proj3, proj3, small3, conv_w, conv_w, conv_w, prm, norm_w)


def _ssd_kernel(x_ref, b_ref, c_ref, z_ref, sm_ref, wx_ref, wb_ref, wc_ref, cbx_ref, cbb_ref, cbc_ref,
                prm_ref, drep_ref, nw_ref, o_ref,
                acs_ref, acst_ref, cv_s, xs_s, xdt_s, b_s, c_s, st_s, y_s, *, dt_col0, heads_per_group):
    grp = pl.program_id(1)
    seq = x_ref.shape[0]
    hp = SSM_HEAD_DIM
    hg = heads_per_group
    nchunk = seq // CHUNK

    _causal_conv_silu(x_ref[...].astype(F32), wx_ref[...], cbx_ref[...], xs_s)
    xs = xs_s[...]
    _causal_conv_silu(b_ref[...].astype(F32), wb_ref[...], cbb_ref[...], cv_s)
    b_s[...] = cv_s[...].astype(BF16)
    _causal_conv_silu(c_ref[...].astype(F32), wc_ref[...], cbc_ref[...], cv_s)
    c_s[...] = cv_s[...].astype(BF16)

    sm = sm_ref[...]
    dt_all = _softplus(sm + prm_ref[1:2, :])
    a_all = -jnp.exp(prm_ref[0:1, :]) * dt_all
    r = lax.broadcasted_iota(jnp.int32, (LANES, LANES), 0)
    c = lax.broadcasted_iota(jnp.int32, (LANES, LANES), 1)
    sel = ((c < hg) & (r == dt_col0 + hg * grp + c)).astype(BF16)
    a_g = _select_cols(a_all, sel)
    _chunk_cumsum(a_g, acs_ref)
    acst_ref[...] = acs_ref[...].T
    gw = xs.shape[1]
    r2 = lax.broadcasted_iota(jnp.int32, (LANES, gw), 0)
    c2 = lax.broadcasted_iota(jnp.int32, (LANES, gw), 1)
    rep = (r2 == dt_col0 + hg * grp + c2 // hp).astype(BF16)
    xdt_s[...] = (xs * _select_cols(dt_all, rep)).astype(BF16)
    st_s[...] = jnp.zeros_like(st_s)

    rr = lax.broadcasted_iota(jnp.int32, (CHUNK, CHUNK), 0)
    cc = lax.broadcasted_iota(jnp.int32, (CHUNK, CHUNK), 1)
    causal = rr >= cc

    def chunk_body(ci, carry):
        sl = pl.ds(pl.multiple_of(ci * CHUNK, CHUNK), CHUNK)
        b_c = b_s[sl, :]
        c_c = c_s[sl, :]
        cb = _mm_nt(c_c, b_c)
        acol = acs_ref[sl, :]
        arow = acst_ref[:, sl]
        heads = range(hg)
        acs = [acol[:, j:j + 1] for j in heads]
        xjs = [xdt_s[sl, j * hp:(j + 1) * hp] for j in heads]
        prevs = [st_s[j] for j in heads]
        y_offs = [_mm_nt(c_c, prev) for prev in prevs]
        y_diags = [_mm(cb * jnp.where(causal, jnp.exp(jnp.minimum(ac - arow[j:j + 1, :], 0.0)), 0.0), xj)
                   for j, ac, xj in zip(heads, acs, xjs)]
        st_news = [_mm_tn(xj.astype(F32) * jnp.exp(ac[CHUNK - 1:CHUNK, :] - ac), b_c)
                   for ac, xj in zip(acs, xjs)]
        for j, ac, prev, y_off, y_diag, st_new in zip(heads, acs, prevs, y_offs, y_diags, st_news):
            st_s[j] = prev * jnp.exp(ac[CHUNK - 1:CHUNK, :]) + st_new
            y_s[sl, j * hp:(j + 1) * hp] = y_diag + y_off * jnp.exp(ac)
        return carry

    lax.fori_loop(0, nchunk, chunk_body, 0)

    z = z_ref[...].astype(F32)
    y = (y_s[...] + drep_ref[...] * xs_s[...]) * _silu(z)
    o_ref[...] = _rms(y, nw_ref[...]).astype(o_ref.dtype)


def _ssd(proj3, small3, conv_w, conv_b, prm, d_rep, norm_w, *, xbc_off, z_off, inner, dt_col0):
    bsz, seq, _ = proj3.shape
    gw = inner // SSM_GROUPS
    hg = gw // SSM_HEAD_DIM
    ns = SSM_STATE
    xb = xbc_off // gw
    bb = (xbc_off + inner) // ns
    cb = bb + SSM_GROUPS
    zb = z_off // gw
    wbb = inner // ns
    wcb = wbb + SSM_GROUPS
    kern = functools.partial(_ssd_kernel, dt_col0=dt_col0, heads_per_group=hg)
    return pl.pallas_call(
        kern,
        grid=(bsz, SSM_GROUPS),
        in_specs=[
            pl.BlockSpec((None, seq, gw), lambda b, g: (b, 0, xb + g)),
            pl.BlockSpec((None, seq, ns), lambda b, g: (b, 0, bb + g)),
            pl.BlockSpec((None, seq, ns), lambda b, g: (b, 0, cb + g)),
            pl.BlockSpec((None, seq, gw), lambda b, g: (b, 0, zb + g)),
            pl.BlockSpec((None, seq, LANES), lambda b, g: (b, 0, 0)),
            pl.BlockSpec((CONV_WIDTH, gw), lambda b, g: (0, g)),
            pl.BlockSpec((CONV_WIDTH, ns), lambda b, g: (0, wbb + g)),
            pl.BlockSpec((CONV_WIDTH, ns), lambda b, g: (0, wcb + g)),
            pl.BlockSpec((1, gw), lambda b, g: (0, g)),
            pl.BlockSpec((1, ns), lambda b, g: (0, wbb + g)),
            pl.BlockSpec((1, ns), lambda b, g: (0, wcb + g)),
            pl.BlockSpec((8, LANES), lambda b, g: (0, 0)),
            pl.BlockSpec((1, gw), lambda b, g: (0, g)),
            pl.BlockSpec((1, gw), lambda b, g: (0, g)),
        ],
        out_specs=pl.BlockSpec((None, seq, gw), lambda b, g: (b, 0, g)),
        out_shape=jax.ShapeDtypeStruct((bsz, seq, inner), BF16),
        scratch_shapes=[
            pltpu.VMEM((seq, LANES), F32),
            pltpu.VMEM((LANES, seq), F32),
            pltpu.VMEM((seq, ns), F32),
            pltpu.VMEM((seq, gw), F32),
            pltpu.VMEM((seq, gw), BF16),
            pltpu.VMEM((seq, ns), BF16),
            pltpu.VMEM((seq, ns), BF16),
            pltpu.VMEM((hg, SSM_HEAD_DIM, ns), F32),
            pltpu.VMEM((seq, gw), F32),
        ],
        compiler_params=_params(("parallel", "parallel")),
        name="ssd",
    )(proj3, proj3, proj3, proj3, small3, conv_w, conv_w, conv_w, conv_b, conv_b, conv_b,
      prm, d_rep, norm_w)


def _out_proj_kernel(a_ref, b_ref, wa_ref, wb_ref, x_ref, o_ref):
    acc = jnp.dot(a_ref[...], wa_ref[...], preferred_element_type=F32)
    acc = acc + jnp.dot(b_ref[...], wb_ref[...], preferred_element_type=F32)
    o_ref[...] = x_ref[...] + acc


def _out_proj(o_a, o_b, w_a, w_b, x2, tm=512, tn=1024):
    tok, d = x2.shape
    ka = o_a.shape[1]
    kb = o_b.shape[1]
    return pl.pallas_call(
        _out_proj_kernel,
        grid=(tok // tm, d // tn),
        in_specs=[
            pl.BlockSpec((tm, ka), lambda i, j: (i, 0)),
            pl.BlockSpec((tm, kb), lambda i, j: (i, 0)),
            pl.BlockSpec((ka, tn), lambda i, j: (0, j)),
            pl.BlockSpec((kb, tn), lambda i, j: (0, j)),
            pl.BlockSpec((tm, tn), lambda i, j: (i, j)),
        ],
        out_specs=pl.BlockSpec((tm, tn), lambda i, j: (i, j)),
        out_shape=jax.ShapeDtypeStruct((tok, d), F32),
        compiler_params=_params(("parallel", "parallel")),
        name="out_proj",
    )(o_a, o_b, w_a, w_b, x2)


def _router_kernel(h_ref, nw_ref, wr_ref, br_ref, u_ref, idx_ref, gate_ref, rank_ref, cnt_ref, carry_s):
    step = pl.program_id(0)
    tm = h_ref.shape[0]

    @pl.when(step == 0)
    def _():
        carry_s[...] = jnp.zeros_like(carry_s)

    u = _rms(h_ref[...], nw_ref[...])
    u_ref[...] = _pack_bf16_pairs(u)
    logits = _mm_exact(u, wr_ref[...]) + br_ref[...]
    lane = lax.broadcasted_iota(jnp.int32, (tm, LANES), 1).astype(F32)
    work = logits
    vals, idxs = [], []
    for _ in range(TOP_K):
        m = jnp.max(work, axis=1, keepdims=True)
        idx = jnp.min(jnp.where(work == m, lane, float(LANES)), axis=1, keepdims=True)
        vals.append(m)
        idxs.append(idx)
        work = jnp.where(lane == idx, -jnp.inf, work)
    onehot = (work == -jnp.inf).astype(F32)
    exps = [jnp.exp(v - vals[0]) for v in vals]
    denom = exps[0]
    for e in exps[1:]:
        denom = denom + e
    r = lax.broadcasted_iota(jnp.int32, (tm, tm), 0)
    c = lax.broadcasted_iota(jnp.int32, (tm, tm), 1)
    below = (r > c).astype(BF16)
    excl = jnp.dot(below, onehot.astype(BF16), preferred_element_type=F32) + carry_s[...]
    idx_out = jnp.zeros((tm, LANES), F32)
    gate_out = jnp.zeros((tm, LANES), F32)
    rank_out = jnp.zeros((tm, LANES), F32)
    for kk in range(TOP_K):
        rank_k = jnp.sum(jnp.where(lane == idxs[kk], excl, 0.0), axis=1, keepdims=True)
        idx_out = jnp.where(lane == kk, idxs[kk], idx_out)
        gate_out = jnp.where(lane == kk, exps[kk] / denom, gate_out)
        rank_out = jnp.where(lane == kk, rank_k, rank_out)
    idx_ref[...] = idx_out.astype(jnp.int32)
    gate_ref[...] = gate_out
    rank_ref[...] = rank_out.astype(jnp.int32)
    total = carry_s[...] + jnp.sum(onehot, axis=0, keepdims=True)
    carry_s[...] = total
    cnt_ref[...] = jnp.broadcast_to(total, cnt_ref.shape)


def _router(h2d, norm_w, w_r, b_r, tm=512):
    tok, d = h2d.shape
    return pl.pallas_call(
        _router_kernel,
        grid=(tok // tm,),
        in_specs=[
            pl.BlockSpec((tm, d), lambda i: (i, 0)),
            pl.BlockSpec((1, d), lambda i: (0, 0)),
            pl.BlockSpec((d, LANES), lambda i: (0, 0)),
            pl.BlockSpec((1, LANES), lambda i: (0, 0)),
        ],
        out_specs=[
            pl.BlockSpec((tm, d // 2), lambda i: (i, 0)),
            pl.BlockSpec((tm, LANES), lambda i: (i, 0)),
            pl.BlockSpec((tm, LANES), lambda i: (i, 0)),
            pl.BlockSpec((tm, LANES), lambda i: (i, 0)),
            pl.BlockSpec((8, LANES), lambda i: (0, 0)),
        ],
        out_shape=[
            jax.ShapeDtypeStruct((tok, d // 2), jnp.uint32),
            jax.ShapeDtypeStruct((tok, LANES), jnp.int32),
            jax.ShapeDtypeStruct((tok, LANES), F32),
            jax.ShapeDtypeStruct((tok, LANES), jnp.int32),
            jax.ShapeDtypeStruct((8, LANES), F32),
        ],
        scratch_shapes=[pltpu.VMEM((1, LANES), F32)],
        compiler_params=_params(("arbitrary",)),
        name="router",
    )(h2d, norm_w, w_r, b_r)


def _dispatch_kernel(pos_ref, u_ref, xs_in, xs_hbm, sem, *, tm):
    del xs_in

    def row_copy(t, k):
        return pltpu.make_async_copy(u_ref.at[pl.ds(t, 1)],
                                     xs_hbm.at[pl.ds(pos_ref[t * TOP_K + k], 1)], sem)

    def issue(t, carry):
        for k in range(TOP_K):
            row_copy(t, k).start()
        return carry

    def drain(t, carry):
        for k in range(TOP_K):
            row_copy(t, k).wait()
        return carry

    lax.fori_loop(0, tm, issue, 0)
    lax.fori_loop(0, tm, drain, 0)


def _dispatch(pos_flat, u2d, xs_init, tm=256):
    tok, d = u2d.shape
    kern = functools.partial(_dispatch_kernel, tm=tm)
    return pl.pallas_call(
        kern,
        grid=(tok // tm,),
        in_specs=[
            pl.BlockSpec((tm * TOP_K,), lambda i: (i,), memory_space=pltpu.SMEM),
            pl.BlockSpec((tm, d), lambda i: (i, 0)),
            pl.BlockSpec(memory_space=pl.ANY),
        ],
        out_specs=pl.BlockSpec(memory_space=pl.ANY),
        out_shape=jax.ShapeDtypeStruct(xs_init.shape, xs_init.dtype),
        scratch_shapes=[pltpu.SemaphoreType.DMA(())],
        input_output_aliases={2: 0},
        compiler_params=_params(("arbitrary",)),
        name="dispatch",
    )(pos_flat, u2d, xs_init)


def _moe_kernel(bexp_ref, nval_ref, x_ref, w1g_ref, w1l_ref, b1g_ref, b1l_ref, w2_ref, b2_ref,
                o_ref, xb_s, acc_s, wg_s, wl_s, w2_s, *, sub):
    i = pl.program_id(0)
    j = pl.program_id(1)
    nf = pl.num_programs(1)
    nval = nval_ref[i]
    bm, d = xb_s.shape

    @pl.when((nval > 0) & (j == 0))
    def _():
        lo, hi = _unpack_bf16_pairs(x_ref[...])
        xb_s[:, 0:d // 2] = lo.astype(BF16)
        xb_s[:, d // 2:d] = hi.astype(BF16)
        acc_s[...] = jnp.broadcast_to(b2_ref[...], acc_s.shape)

    for s in range(bm // sub):
        @pl.when(nval > s * sub)
        def _(s=s):
            if s == 0:
                wg = w1g_ref[...].astype(BF16)
                wl = w1l_ref[...].astype(BF16)
                w2 = w2_ref[...].astype(BF16)
                wg_s[...] = wg
                wl_s[...] = wl
                w2_s[...] = w2
            else:
                wg = wg_s[...]
                wl = wl_s[...]
                w2 = w2_s[...]
            rows = pl.ds(s * sub, sub)
            x = xb_s[rows, :]
            hg = jnp.dot(x, wg, preferred_element_type=F32) + b1g_ref[...]
            hl = jnp.dot(x, wl, preferred_element_type=F32) + b1l_ref[...]
            glu = jnp.minimum(hg, SWIGLU_LIMIT)
            lin = jnp.clip(hl, -SWIGLU_LIMIT, SWIGLU_LIMIT)
            act = glu * _sigmoid(SWIGLU_ALPHA * glu) * (lin + 1.0)
            acc_s[rows, :] += jnp.dot(act.astype(BF16), w2, preferred_element_type=F32)

    @pl.when((nval > 0) & (j == nf - 1))
    def _():
        o_ref[...] = _pack_bf16_pairs(acc_s[...])

    @pl.when((nval == 0) & (j == 0))
    def _():
        o_ref[...] = jnp.zeros_like(o_ref)


def _moe(bexp, nval, xs, w1, b1, w2, b2, *, bm, tf, sub):
    rows, dh = xs.shape
    d = 2 * dh
    ff = w2.shape[1]
    nb = rows // bm
    nf = ff // tf

    def jj(i, j, nv):
        return jnp.where(nv[i] > 0, j, nf - 1)

    grid_spec = pltpu.PrefetchScalarGridSpec(
        num_scalar_prefetch=2,
        grid=(nb, nf),
        in_specs=[
            pl.BlockSpec((bm, dh), lambda i, j, be, nv: (i, 0)),
            pl.BlockSpec((None, d, tf), lambda i, j, be, nv: (be[i], 0, jj(i, j, nv))),
            pl.BlockSpec((None, d, tf), lambda i, j, be, nv: (be[i], 0, nf + jj(i, j, nv))),
            pl.BlockSpec((None, 1, tf), lambda i, j, be, nv: (be[i], 0, jj(i, j, nv))),
            pl.BlockSpec((None, 1, tf), lambda i, j, be, nv: (be[i], 0, nf + jj(i, j, nv))),
            pl.BlockSpec((None, tf, d), lambda i, j, be, nv: (be[i], jj(i, j, nv), 0)),
            pl.BlockSpec((None, 1, d), lambda i, j, be, nv: (be[i], 0, 0)),
        ],
        out_specs=pl.BlockSpec((bm, dh), lambda i, j, be, nv: (i, 0)),
        scratch_shapes=[
            pltpu.VMEM((bm, d), BF16),
            pltpu.VMEM((bm, d), F32),
            pltpu.VMEM((d, tf), BF16),
            pltpu.VMEM((d, tf), BF16),
            pltpu.VMEM((tf, d), BF16),
        ],
    )
    return pl.pallas_call(
        functools.partial(_moe_kernel, sub=sub),
        grid_spec=grid_spec,
        out_shape=jax.ShapeDtypeStruct((rows, dh), jnp.uint32),
        compiler_params=_params(("arbitrary", "arbitrary")),
        name="moe",
    )(bexp, nval, xs, w1, w1, b1, b1, w2, b2)


def _combine_kernel(pos_ref, gate_ref, h_ref, ys_hbm, p_ref, pn_ref, wg_ref, wp_ref, fn_ref, o_ref,
                    gbuf, sem, *, tm):
    def row_copy(t, k):
        return pltpu.make_async_copy(ys_hbm.at[pl.ds(pos_ref[t * TOP_K + k], 1)],
                                     gbuf.at[k, pl.ds(t, 1)], sem)

    def issue(t, carry):
        for k in range(TOP_K):
            row_copy(t, k).start()
        return carry

    def drain(t, carry):
        for k in range(TOP_K):
            row_copy(t, k).wait()
        return carry

    lax.fori_loop(0, tm, issue, 0)
    lax.fori_loop(0, tm, drain, 0)

    gate = gate_ref[...]
    h = h_ref[...]
    dh = h.shape[1] // 2
    m_lo = jnp.zeros((tm, dh), F32)
    m_hi = jnp.zeros((tm, dh), F32)
    for k in range(TOP_K):
        lo, hi = _unpack_bf16_pairs(gbuf[k])
        m_lo = m_lo + gate[:, k:k + 1] * lo
        m_hi = m_hi + gate[:, k:k + 1] * hi
    h = h + jnp.concatenate([m_lo, m_hi], axis=1)
    u = _rms(h, pn_ref[...]).astype(BF16)
    gv = _sigmoid(jnp.dot(u, wg_ref[...], preferred_element_type=F32))
    pp = jnp.dot(p_ref[...].astype(BF16), wp_ref[...], preferred_element_type=F32)
    h = h + pp * gv
    o_ref[...] = _rms(h, fn_ref[...])


def _combine(pos_flat, gate, h2d, ys, p2d, ple_norm, w_gate, w_proj, final_norm, tm=256):
    tok, d = h2d.shape
    pd = p2d.shape[1]
    kern = functools.partial(_combine_kernel, tm=tm)
    return pl.pallas_call(
        kern,
        grid=(tok // tm,),
        in_specs=[
            pl.BlockSpec((tm * TOP_K,), lambda i: (i,), memory_space=pltpu.SMEM),
            pl.BlockSpec((tm, LANES), lambda i: (i, 0)),
            pl.BlockSpec((tm, d), lambda i: (i, 0)),
            pl.BlockSpec(memory_space=pl.ANY),
            pl.BlockSpec((tm, pd), lambda i: (i, 0)),
            pl.BlockSpec((1, d), lambda i: (0, 0)),
            pl.BlockSpec((d, d), lambda i: (0, 0)),
            pl.BlockSpec((pd, d), lambda i: (0, 0)),
            pl.BlockSpec((1, d), lambda i: (0, 0)),
        ],
        out_specs=pl.BlockSpec((tm, d), lambda i: (i, 0)),
        out_shape=jax.ShapeDtypeStruct((tok, d), F32),
        scratch_shapes=[pltpu.VMEM((TOP_K, tm, d // 2), jnp.uint32), pltpu.SemaphoreType.DMA(())],
        compiler_params=_params(("arbitrary",)),
        name="combine",
    )(pos_flat, gate, h2d, ys, p2d, ple_norm, w_gate, w_proj, final_norm)


def _pad_cols(v, off):
    return jnp.pad(v.astype(F32), (off, LANES - off - v.shape[0]))


def _layer(h3, p3, mix_norm, w_in, gdn_conv_w, gdn_a_log, gdn_dt_bias, gdn_norm_w, ssm_conv_w,
           ssm_conv_b, ssm_a_log, ssm_dt_bias, ssm_d, ssm_norm_w, w_out, ffn_norm, w_router, b_router,
           w_mlp1, b_mlp1, w_mlp2, b_mlp2, ple_norm, w_ple_gate, w_ple_proj, final_norm, *, moe_bm, moe_tf, moe_sub):
    bsz, seq, d = h3.shape
    tok = bsz * seq
    n_vh = gdn_a_log.shape[0]
    n_qk = n_vh // 2
    key_dim = n_qk * GDN_HEAD_DIM
    val_dim = n_vh * GDN_HEAD_DIM
    gdn_conv_ch = 2 * key_dim + val_dim
    ssm_heads = ssm_a_log.shape[0]
    inner = ssm_heads * SSM_HEAD_DIM
    ssm_conv_ch = inner + 2 * SSM_GROUPS * SSM_STATE

    o0 = 0
    o_qkv, o0 = o0, o0 + gdn_conv_ch
    o_z, o0 = o0, o0 + val_dim
    o_b, o0 = o0, o0 + n_vh
    o_a, o0 = o0, o0 + n_vh
    o_xbc, o0 = o0, o0 + ssm_conv_ch
    o_sz, o0 = o0, o0 + inner
    o_dt, o0 = o0, o0 + ssm_heads
    w_main = jnp.concatenate([w_in[:, o_qkv:o_b], w_in[:, o_xbc:o_dt]], axis=1).astype(BF16)
    w_small = jnp.concatenate([w_in[:, o_b:o_xbc], w_in[:, o_dt:o0]], axis=1)
    n_small = w_small.shape[1]
    w_small = jnp.pad(w_small, ((0, 0), (0, LANES - n_small))).astype(BF16)
    m_qkv, m_z = 0, gdn_conv_ch
    m_xbc = gdn_conv_ch + val_dim
    m_sz = m_xbc + ssm_conv_ch
    dt_col0 = 2 * n_vh

    x2 = h3.reshape(tok, d)
    proj, small = _in_proj(x2, mix_norm.reshape(1, d), w_main, w_small)
    proj3 = proj.reshape(bsz, seq, proj.shape[1])
    small3 = small.reshape(bsz, seq, LANES)

    zrow = jnp.zeros((LANES,), F32)
    gdn_prm = jnp.stack([_pad_cols(gdn_a_log, n_vh), _pad_cols(gdn_dt_bias, n_vh)] + [zrow] * 6)
    o_gdn = _gdn(proj3, small3, gdn_conv_w, gdn_prm, gdn_norm_w.reshape(1, GDN_HEAD_DIM),
                 qkv_off=m_qkv, z_off=m_z, n_qk=n_qk, out_cols=val_dim)

    ssm_prm = jnp.stack([_pad_cols(ssm_a_log, dt_col0), _pad_cols(ssm_dt_bias, dt_col0)] + [zrow] * 6)
    d_rep = jnp.repeat(ssm_d.astype(F32), SSM_HEAD_DIM).reshape(1, inner)
    o_ssm = _ssd(proj3, small3, ssm_conv_w, ssm_conv_b.reshape(1, ssm_conv_ch), ssm_prm, d_rep,
                 ssm_norm_w.reshape(1, inner), xbc_off=m_xbc, z_off=m_sz, inner=inner, dt_col0=dt_col0)

    w_out_b = w_out.astype(BF16)
    h1 = _out_proj(o_gdn.reshape(tok, val_dim), o_ssm.reshape(tok, inner),
                   w_out_b[:val_dim], w_out_b[val_dim:], x2)

    n_exp = w_router.shape[1]
    w_r = jnp.pad(w_router, ((0, 0), (0, LANES - n_exp)))
    b_r = jnp.pad(b_router.astype(F32), (0, LANES - n_exp), constant_values=NEG_BIG).reshape(1, LANES)
    u2, idx_l, gate_l, rank_l, cnt = _router(h1, ffn_norm.reshape(1, d), w_r, b_r)

    counts = cnt[0, :n_exp].astype(jnp.int32)
    padded = ((counts + moe_bm - 1) // moe_bm) * moe_bm
    pad_end = jnp.cumsum(padded)
    pad_start = pad_end - padded
    top_idx = idx_l[:, :TOP_K]
    pos = (pad_start[top_idx] + rank_l[:, :TOP_K]).astype(jnp.int32).reshape(-1)
    n_blocks = (tok * TOP_K) // moe_bm + n_exp
    block_start = jnp.arange(n_blocks, dtype=jnp.int32) * moe_bm
    bexp = jnp.minimum(jnp.sum(block_start[:, None] >= pad_end[None, :], axis=1), n_exp - 1).astype(jnp.int32)
    nval = jnp.clip(pad_start[bexp] + counts[bexp] - block_start, 0, moe_bm)
    nval = jnp.where(block_start < pad_end[-1], nval, 0).astype(jnp.int32)

    xs = _dispatch(pos, u2, jnp.zeros((n_blocks * moe_bm, d // 2), jnp.uint32))
    ff = w_mlp2.shape[1]
    ys = _moe(bexp, nval, xs, w_mlp1, b_mlp1.reshape(n_exp, 1, 2 * ff), w_mlp2,
              b_mlp2.reshape(n_exp, 1, d), bm=moe_bm, tf=moe_tf, sub=moe_sub)

    out = _combine(pos, gate_l, h1, ys, p3.reshape(tok, p3.shape[-1]), ple_norm.reshape(1, d),
                   w_ple_gate.astype(BF16), w_ple_proj.astype(BF16), final_norm.reshape(1, d))
    return out.reshape(bsz, seq, d)


def kernel(x, p, mix_norm, w_in, gdn_conv_w, gdn_a_log, gdn_dt_bias, gdn_norm_w, ssm_conv_w, ssm_conv_b,
           ssm_a_log, ssm_dt_bias, ssm_d, ssm_norm_w, w_out, ffn_norm, w_router, b_router, w_mlp1, b_mlp1,
           w_mlp2, b_mlp2, ple_norm, w_ple_gate, w_ple_proj, final_norm):
    assert mix_norm.shape[0] == 1, "single-layer trunk"
    return _layer(x, p[0], mix_norm[0], w_in[0], gdn_conv_w[0], gdn_a_log[0], gdn_dt_bias[0], gdn_norm_w[0],
                  ssm_conv_w[0], ssm_conv_b[0], ssm_a_log[0], ssm_dt_bias[0], ssm_d[0], ssm_norm_w[0],
                  w_out[0], ffn_norm[0], w_router[0], b_router[0], w_mlp1[0], b_mlp1[0], w_mlp2[0],
                  b_mlp2[0], ple_norm[0], w_ple_gate[0], w_ple_proj[0], final_norm, moe_bm=1024, moe_tf=256, moe_sub=256)
```

```python
import functools

import jax
import jax.numpy as jnp
from jax import lax
from jax.experimental import pallas as pl
from jax.experimental.pallas import tpu as pltpu

F32 = jnp.float32
BF16 = jnp.bfloat16
HIGHEST = lax.Precision.HIGHEST

NORM_EPS = 1e-6
CONV_WIDTH = 4
GDN_HEAD_DIM = 128
SSM_HEAD_DIM = 64
SSM_GROUPS = 4
SSM_STATE = 128
N_EXPERTS = 32
TOP_K = 4
SWIGLU_ALPHA = 1.702
SWIGLU_LIMIT = 7.0

LANES = 128
CHUNK = 128
GDN_CHUNKS_PER_TRIP = 4
VMEM_LIMIT = 56 * 1024 * 1024

NEG_BIG = -1e30


def _params(sem, vmem=VMEM_LIMIT):
    return pltpu.CompilerParams(dimension_semantics=sem, vmem_limit_bytes=vmem)


def _rms(x, w):
    ms = jnp.mean(x * x, axis=-1, keepdims=True)
    return x * lax.rsqrt(ms + NORM_EPS) * w


def _sigmoid(x):
    return 0.5 * jnp.tanh(0.5 * x) + 0.5


def _softplus(x):
    return jnp.maximum(x, 0.0) + jnp.log(1.0 + jnp.exp(-jnp.abs(x)))


def _mm(a, b):
    return jnp.dot(a.astype(BF16), b.astype(BF16), preferred_element_type=F32)


def _mm_nt(a, b):
    return lax.dot_general(a.astype(BF16), b.astype(BF16), (((1,), (1,)), ((), ())),
                           preferred_element_type=F32)


def _mm_tn(a, b):
    return lax.dot_general(a.astype(BF16), b.astype(BF16), (((0,), (0,)), ((), ())),
                           preferred_element_type=F32)


def _mm_exact(a, b):
    return jnp.dot(a, b, precision=HIGHEST, preferred_element_type=F32)


def _pack_bf16_pairs(x):
    n = x.shape[1] // 2
    xb = x.astype(BF16).astype(F32)
    lo = lax.bitcast_convert_type(xb[:, :n], jnp.uint32)
    hi = lax.bitcast_convert_type(xb[:, n:], jnp.uint32)
    return (lo >> 16) | (hi & jnp.uint32(0xFFFF0000))


def _unpack_bf16_pairs(w):
    lo = lax.bitcast_convert_type(w << 16, F32)
    hi = lax.bitcast_convert_type(w & jnp.uint32(0xFFFF0000), F32)
    return lo, hi


def _silu(x):
    h = 0.5 * x
    return h * jnp.tanh(h) + h


def _causal_conv_silu(x, w, bias, out_ref):
    def conv(xv, masked):
        acc = xv * w[CONV_WIDTH - 1:CONV_WIDTH, :]
        for k in range(CONV_WIDTH - 1):
            shift = CONV_WIDTH - 1 - k
            xs = pltpu.roll(xv, shift, axis=0)
            if masked:
                row = lax.broadcasted_iota(jnp.int32, xv.shape, 0)
                xs = jnp.where(row >= shift, xs, 0.0)
            acc = acc + xs * w[k:k + 1, :]
        if bias is not None:
            acc = acc + bias
        return _silu(acc)

    out_ref[...] = conv(x, False).astype(out_ref.dtype)
    out_ref[0:8, :] = conv(x[0:8, :], True).astype(out_ref.dtype)


def _bf16_pieces(x):
    hi = x.astype(BF16)
    r1 = x - hi.astype(F32)
    mid = r1.astype(BF16)
    lo = (r1 - mid.astype(F32)).astype(BF16)
    return hi, mid, lo


def _select_cols(vals, emat):
    out = None
    for piece in _bf16_pieces(vals):
        t = jnp.dot(piece, emat, preferred_element_type=F32)
        out = t if out is None else out + t
    return out


def _chunk_cumsum(vals, out_ref):
    seq = vals.shape[0]
    r = lax.broadcasted_iota(jnp.int32, (CHUNK, CHUNK), 0)
    c = lax.broadcasted_iota(jnp.int32, (CHUNK, CHUNK), 1)
    tri = (r >= c).astype(BF16)
    pieces = _bf16_pieces(vals)
    for i in range(seq // CHUNK):
        rows = slice(i * CHUNK, (i + 1) * CHUNK)
        acc = None
        for piece in pieces:
            t = jnp.dot(tri, piece[rows, :], preferred_element_type=F32)
            acc = t if acc is None else acc + t
        out_ref[rows, :] = acc


def _in_proj_kernel(x_ref, g_ref, w_ref, ws_ref, o_ref, os_ref, u_ref):
    @pl.when(pl.program_id(1) == 0)
    def _():
        u = _rms(x_ref[...], g_ref[...]).astype(BF16)
        u_ref[...] = u
        os_ref[...] = jnp.dot(u, ws_ref[...], preferred_element_type=F32)

    o_ref[...] = jnp.dot(u_ref[...], w_ref[...], preferred_element_type=F32).astype(o_ref.dtype)


def _in_proj(x2, norm_w, w_main, w_small, tm=512, tn=1024):
    tok, d = x2.shape
    n = w_main.shape[1]
    return pl.pallas_call(
        _in_proj_kernel,
        grid=(tok // tm, n // tn),
        in_specs=[
            pl.BlockSpec((tm, d), lambda i, j: (i, 0)),
            pl.BlockSpec((1, d), lambda i, j: (0, 0)),
            pl.BlockSpec((d, tn), lambda i, j: (0, j)),
            pl.BlockSpec((d, LANES), lambda i, j: (0, 0)),
        ],
        out_specs=[
            pl.BlockSpec((tm, tn), lambda i, j: (i, j)),
            pl.BlockSpec((tm, LANES), lambda i, j: (i, 0)),
        ],
        out_shape=[
            jax.ShapeDtypeStruct((tok, n), BF16),
            jax.ShapeDtypeStruct((tok, LANES), F32),
        ],
        scratch_shapes=[pltpu.VMEM((tm, d), BF16)],
        compiler_params=_params(("parallel", "arbitrary")),
        name="in_proj",
    )(x2, norm_w, w_main, w_small)


def _tri_masks(n):
    r = lax.broadcasted_iota(jnp.int32, (n, n), 0)
    c = lax.broadcasted_iota(jnp.int32, (n, n), 1)
    same = lambda s: (r // s) == (c // s)
    bd16 = same(16).astype(F32)
    levels = []
    s = 16
    while s < n:
        levels.append((same(2 * s) & jnp.logical_not(same(s))).astype(F32))
        s *= 2
    return bd16, levels


def _unit_lower_inverse(lmats, eye, bd16, levels):
    lds = [l * bd16 for l in lmats]
    ps = [eye - ld for ld in lds]
    ms = [_mm(ld, ld) for ld in lds]
    for it in range(3):
        ps = [p + _mm(p, m) for p, m in zip(ps, ms)]
        if it < 2:
            ms = [_mm(m, m) for m in ms]
    for msk in levels:
        ts = [_mm(l * msk, p) for l, p in zip(lmats, ps)]
        ps = [p - _mm(p, t) for p, t in zip(ps, ts)]
    return ps


def _gdn_kernel(q_ref, k_ref, v_ref, z_ref, sm_ref, wq_ref, wk_ref, wv_ref, prm_ref, nw_ref,
                o_ref, gc_ref, gct_ref, cv_s, q_s, k_s, kb_s, rhs_s, qd_s, a_s, b_s, qp_s, oacc_s, *, n_vh):
    hq = pl.program_id(1)
    seq = q_ref.shape[0]
    dk = GDN_HEAD_DIM
    nchunk = seq // CHUNK

    _causal_conv_silu(q_ref[...].astype(F32), wq_ref[...], None, cv_s.at[:, 0:dk])
    q = cv_s[:, 0:dk]
    _causal_conv_silu(k_ref[...].astype(F32), wk_ref[...], None, cv_s.at[:, 0:dk])
    k = cv_s[:, 0:dk]
    _causal_conv_silu(v_ref[...].astype(F32), wv_ref[...], None, cv_s)
    v = cv_s[...]
    q = q * lax.rsqrt(jnp.sum(q * q, axis=-1, keepdims=True) + NORM_EPS) * (dk ** -0.5)
    k = k * lax.rsqrt(jnp.sum(k * k, axis=-1, keepdims=True) + NORM_EPS)
    q_s[...] = q.astype(BF16)
    k_s[...] = k.astype(BF16)

    sm = sm_ref[...]
    beta_all = _sigmoid(sm)
    g_all = -jnp.exp(prm_ref[0:1, :]) * _softplus(sm + prm_ref[1:2, :])
    r = lax.broadcasted_iota(jnp.int32, (LANES, LANES), 0)
    c = lax.broadcasted_iota(jnp.int32, (LANES, LANES), 1)
    sel_g = ((c < 2) & (r == n_vh + 2 * hq + c)).astype(BF16)
    sel_b = ((c < 2) & (r == 2 * hq + c)).astype(BF16)
    g_sel = _select_cols(g_all, sel_g)
    beta = _select_cols(beta_all, sel_b)
    _chunk_cumsum(g_sel, gc_ref)
    gc = gc_ref[...]
    gct_ref[...] = gc.T
    eg = jnp.exp(gc)
    for hv in range(2):
        bcol = beta[:, hv:hv + 1]
        ecol = eg[:, hv:hv + 1]
        kb = k * bcol
        kb_s[hv] = kb.astype(BF16)
        rhs_s[hv, :, 0:dk] = (v[:, hv * dk:(hv + 1) * dk] * bcol).astype(BF16)
        rhs_s[hv, :, dk:2 * dk] = (kb * ecol).astype(BF16)
        qd_s[hv] = (q * ecol).astype(BF16)

    rr = lax.broadcasted_iota(jnp.int32, (CHUNK, CHUNK), 0)
    cc = lax.broadcasted_iota(jnp.int32, (CHUNK, CHUNK), 1)
    causal = rr >= cc
    strict = rr > cc
    eye = (rr == cc).astype(F32)
    bd16, levels = _tri_masks(CHUNK)

    unroll = GDN_CHUNKS_PER_TRIP

    def terms_body(ci, carry):
        sls = [pl.ds(pl.multiple_of((ci * unroll + s) * CHUNK, CHUNK), CHUNK) for s in range(unroll)]
        k_cs = [k_s[sl, :] for sl in sls]
        qks = [_mm_nt(q_s[sl, :], k_c) for sl, k_c in zip(sls, k_cs)]
        gcols = [gc_ref[sl, :] for sl in sls]
        grows = [gct_ref[:, sl] for sl in sls]
        chains = [(s, hv) for s in range(unroll) for hv in range(2)]
        gc_cs = [gcols[s][:, hv:hv + 1] for s, hv in chains]
        decays = [jnp.where(causal, jnp.exp(jnp.minimum(gc_c - grows[s][hv:hv + 1, :], 0.0)), 0.0)
                  for gc_c, (s, hv) in zip(gc_cs, chains)]
        lmats = [jnp.where(strict, _mm_nt(kb_s[hv, sls[s], :], k_cs[s]) * dec, 0.0)
                 for dec, (s, hv) in zip(decays, chains)]
        pinvs = _unit_lower_inverse(lmats, eye, bd16, levels)
        uws = [_mm(pinv, rhs_s[hv, sls[s], :]) for pinv, (s, hv) in zip(pinvs, chains)]
        kds = [k_cs[s].astype(F32) * jnp.exp(gc_c[CHUNK - 1:CHUNK, :] - gc_c)
               for gc_c, (s, hv) in zip(gc_cs, chains)]
        kdws = [_mm_tn(kd, uw) for kd, uw in zip(kds, uws)]
        aws = [_mm(qks[s] * dec, uw) for dec, uw, (s, hv) in zip(decays, uws, chains)]
        for kdw, aw, (s, hv) in zip(kdws, aws, chains):
            sl = sls[s]
            a_s[hv, sl, :] = (-kdw[:, dk:2 * dk]).astype(BF16)
            b_s[hv, sl, :] = kdw[:, 0:dk]
            qp_s[hv, sl, :] = (qd_s[hv, sl, :].astype(F32) - aw[:, dk:2 * dk]).astype(BF16)
            oacc_s[hv, sl, :] = aw[:, 0:dk]
        return carry

    lax.fori_loop(0, nchunk // unroll, terms_body, 0)

    def recur_body(ci, states):
        sl = pl.ds(pl.multiple_of(ci * CHUNK, CHUNK), CHUNK)
        g_last = gc_ref[pl.ds(ci * CHUNK + CHUNK - 1, 1), :]
        new_states = []
        for hv in range(2):
            sb = states[hv].astype(BF16)
            oacc_s[hv, sl, :] += _mm(qp_s[hv, sl, :], sb)
            new_states.append(states[hv] * jnp.exp(g_last[:, hv:hv + 1])
                              + _mm(a_s[hv, sl, :], sb) + b_s[hv, sl, :])
        return tuple(new_states)

    zero = jnp.zeros((dk, dk), F32)
    lax.fori_loop(0, nchunk, recur_body, (zero, zero))

    z = z_ref[...].astype(F32)
    nw = nw_ref[...]
    for hv in range(2):
        o = _rms(oacc_s[hv], nw)
        zz = z[:, hv * dk:(hv + 1) * dk]
        o_ref[:, hv * dk:(hv + 1) * dk] = (o * _silu(zz)).astype(o_ref.dtype)


def _gdn(proj3, small3, conv_w, prm, norm_w, *, qkv_off, z_off, n_qk, out_cols):
    bsz, seq, _ = proj3.shape
    assert seq % (CHUNK * GDN_CHUNKS_PER_TRIP) == 0
    dk = GDN_HEAD_DIM
    qb = qkv_off // dk
    kb = qb + n_qk
    vb = (qkv_off + 2 * n_qk * dk) // (2 * dk)
    zb = z_off // (2 * dk)
    cq = 0
    ck = n_qk
    cv = (2 * n_qk * dk) // (2 * dk)
    return pl.pallas_call(
        functools.partial(_gdn_kernel, n_vh=2 * n_qk),
        grid=(bsz, n_qk),
        in_specs=[
            pl.BlockSpec((None, seq, dk), lambda b, h: (b, 0, qb + h)),
            pl.BlockSpec((None, seq, dk), lambda b, h: (b, 0, kb + h)),
            pl.BlockSpec((None, seq, 2 * dk), lambda b, h: (b, 0, vb + h)),
            pl.BlockSpec((None, seq, 2 * dk), lambda b, h: (b, 0, zb + h)),
            pl.BlockSpec((None, seq, LANES), lambda b, h: (b, 0, 0)),
            pl.BlockSpec((CONV_WIDTH, dk), lambda b, h: (0, cq + h)),
            pl.BlockSpec((CONV_WIDTH, dk), lambda b, h: (0, ck + h)),
            pl.BlockSpec((CONV_WIDTH, 2 * dk), lambda b, h: (0, cv + h)),
            pl.BlockSpec((8, LANES), lambda b, h: (0, 0)),
            pl.BlockSpec((1, dk), lambda b, h: (0, 0)),
        ],
        out_specs=pl.BlockSpec((None, seq, 2 * dk), lambda b, h: (b, 0, h)),
        out_shape=jax.ShapeDtypeStruct((bsz, seq, out_cols), BF16),
        scratch_shapes=[
            pltpu.VMEM((seq, LANES), F32),
            pltpu.VMEM((LANES, seq), F32),
            pltpu.VMEM((seq, 2 * dk), F32),
            pltpu.VMEM((seq, dk), BF16),
            pltpu.VMEM((seq, dk), BF16),
            pltpu.VMEM((2, seq, dk), BF16),
            pltpu.VMEM((2, seq, 2 * dk), BF16),
            pltpu.VMEM((2, seq, dk), BF16),
            pltpu.VMEM((2, seq, dk), BF16),
            pltpu.VMEM((2, seq, dk), F32),
            pltpu.VMEM((2, seq, dk), BF16),
            pltpu.VMEM((2, seq, dk), F32),
        ],
        compiler_params=_params(("parallel", "parallel")),
        name="gdn",
    )(proj3, proj3, proj3, proj3, small3, conv_w, conv_w, conv_w, prm, norm_w)


def _ssd_kernel(x_ref, b_ref, c_ref, z_ref, sm_ref, wx_ref, wb_ref, wc_ref, cbx_ref, cbb_ref, cbc_ref,
                prm_ref, drep_ref, nw_ref, o_ref,
                acs_ref, acst_ref, cv_s, xs_s, bt_s, c_s, st_s, y_s, *, dt_col0, heads_per_group):
    grp = pl.program_id(1)
    seq = x_ref.shape[0]
    hp = SSM_HEAD_DIM
    hg = heads_per_group
    npair = hg // 2
    gw = hg * hp
    nchunk = seq // CHUNK

    _causal_conv_silu(x_ref[...].astype(F32), wx_ref[...], cbx_ref[...], xs_s)
    _causal_conv_silu(b_ref[...].astype(F32), wb_ref[...], cbb_ref[...], cv_s)
    bt_s[...] = cv_s[...].T.astype(BF16)
    _causal_conv_silu(c_ref[...].astype(F32), wc_ref[...], cbc_ref[...], cv_s)
    c_s[...] = cv_s[...].astype(BF16)

    sm = sm_ref[...]
    dt_all = _softplus(sm + prm_ref[1:2, :])
    a_all = -jnp.exp(prm_ref[0:1, :]) * dt_all
    r = lax.broadcasted_iota(jnp.int32, (LANES, LANES), 0)
    c = lax.broadcasted_iota(jnp.int32, (LANES, LANES), 1)
    sel = ((c < hg) & (r == dt_col0 + hg * grp + c)).astype(BF16)
    a_g = _select_cols(a_all, sel)
    _chunk_cumsum(a_g, acs_ref)
    acst_ref[...] = acs_ref[...].T
    st_s[...] = jnp.zeros_like(st_s)

    def replicate(width):
        rr_ = lax.broadcasted_iota(jnp.int32, (LANES, hg * width), 0)
        cc_ = lax.broadcasted_iota(jnp.int32, (LANES, hg * width), 1)
        return rr_, cc_ // width

    r64, c64 = replicate(hp)
    rep_a = (r64 == c64).astype(BF16)
    rep_dt = (r64 == dt_col0 + hg * grp + c64).astype(BF16)
    r128, c128 = replicate(CHUNK)
    rep_full = (r128 == c128).astype(BF16)
    lane = lax.broadcasted_iota(jnp.int32, (CHUNK, gw), 1)
    first = (lane % (2 * hp)) < hp
    rr = lax.broadcasted_iota(jnp.int32, (CHUNK, CHUNK), 0)
    cc = lax.broadcasted_iota(jnp.int32, (CHUNK, CHUNK), 1)
    causal = rr >= cc
    dt_bias = prm_ref[1:2, :]

    def chunk_body(ci, carry):
        sl = pl.ds(pl.multiple_of(ci * CHUNK, CHUNK), CHUNK)
        bt_c = bt_s[:, sl]
        c_c = c_s[sl, :]
        cb = jnp.dot(c_c, bt_c, preferred_element_type=F32)
        acs_c = acs_ref[sl, :]
        arow = acst_ref[:, sl]
        acrep = _select_cols(acs_c, rep_a)
        acf = _select_cols(acs_c, rep_full)
        xdt = xs_s[sl, :] * _select_cols(_softplus(sm_ref[sl, :] + dt_bias), rep_dt)
        eac = jnp.exp(acrep)
        e_last = eac[CHUNK - 1:CHUNK, :]
        xdec = (xdt * jnp.exp(acrep[CHUNK - 1:CHUNK, :] - acrep)).astype(BF16)
        xlo = jnp.where(first, xdt, 0.0).astype(BF16)
        xhi = jnp.where(first, 0.0, xdt).astype(BF16)
        lms = [cb * jnp.where(causal, jnp.exp(jnp.minimum(
            acf[:, j * CHUNK:(j + 1) * CHUNK] - arow[j:j + 1, :], 0.0)), 0.0) for j in range(hg)]
        pairs = range(npair)
        cols = [slice(p * 2 * hp, (p + 1) * 2 * hp) for p in pairs]
        prevs = [st_s[p] for p in pairs]
        y_offs = [_mm(c_c, prev) for prev in prevs]
        y_diags = [jnp.dot(jnp.concatenate([lms[2 * p], lms[2 * p + 1]], axis=1).astype(BF16),
                           jnp.concatenate([xlo[:, cols[p]], xhi[:, cols[p]]], axis=0),
                           preferred_element_type=F32) for p in pairs]
        st_news = [jnp.dot(bt_c, xdec[:, cols[p]], preferred_element_type=F32) for p in pairs]
        for p in pairs:
            st_s[p] = prevs[p] * e_last[:, cols[p]] + st_news[p]
            y_s[sl, cols[p]] = y_diags[p] + y_offs[p] * eac[:, cols[p]]
        return carry

    lax.fori_loop(0, nchunk, chunk_body, 0)

    z = z_ref[...].astype(F32)
    y = (y_s[...] + drep_ref[...] * xs_s[...]) * _silu(z)
    o_ref[...] = _rms(y, nw_ref[...]).astype(o_ref.dtype)


def _ssd(proj3, small3, conv_w, conv_b, prm, d_rep, norm_w, *, xbc_off, z_off, inner, dt_col0):
    bsz, seq, _ = proj3.shape
    gw = inner // SSM_GROUPS
    hg = gw // SSM_HEAD_DIM
    ns = SSM_STATE
    xb = xbc_off // gw
    bb = (xbc_off + inner) // ns
    cb = bb + SSM_GROUPS
    zb = z_off // gw
    wbb = inner // ns
    wcb = wbb + SSM_GROUPS
    kern = functools.partial(_ssd_kernel, dt_col0=dt_col0, heads_per_group=hg)
    return pl.pallas_call(
        kern,
        grid=(bsz, SSM_GROUPS),
        in_specs=[
            pl.BlockSpec((None, seq, gw), lambda b, g: (b, 0, xb + g)),
            pl.BlockSpec((None, seq, ns), lambda b, g: (b, 0, bb + g)),
            pl.BlockSpec((None, seq, ns), lambda b, g: (b, 0, cb + g)),
            pl.BlockSpec((None, seq, gw), lambda b, g: (b, 0, zb + g)),
            pl.BlockSpec((None, seq, LANES), lambda b, g: (b, 0, 0)),
            pl.BlockSpec((CONV_WIDTH, gw), lambda b, g: (0, g)),
            pl.BlockSpec((CONV_WIDTH, ns), lambda b, g: (0, wbb + g)),
            pl.BlockSpec((CONV_WIDTH, ns), lambda b, g: (0, wcb + g)),
            pl.BlockSpec((1, gw), lambda b, g: (0, g)),
            pl.BlockSpec((1, ns), lambda b, g: (0, wbb + g)),
            pl.BlockSpec((1, ns), lambda b, g: (0, wcb + g)),
            pl.BlockSpec((8, LANES), lambda b, g: (0, 0)),
            pl.BlockSpec((1, gw), lambda b, g: (0, g)),
            pl.BlockSpec((1, gw), lambda b, g: (0, g)),
        ],
        out_specs=pl.BlockSpec((None, seq, gw), lambda b, g: (b, 0, g)),
        out_shape=jax.ShapeDtypeStruct((bsz, seq, inner), BF16),
        scratch_shapes=[
            pltpu.VMEM((seq, LANES), F32),
            pltpu.VMEM((LANES, seq), F32),
            pltpu.VMEM((seq, ns), F32),
            pltpu.VMEM((seq, gw), F32),
            pltpu.VMEM((ns, seq), BF16),
            pltpu.VMEM((seq, ns), BF16),
            pltpu.VMEM((hg // 2, ns, 2 * SSM_HEAD_DIM), F32),
            pltpu.VMEM((seq, gw), F32),
        ],
        compiler_params=_params(("parallel", "parallel")),
        name="ssd",
    )(proj3, proj3, proj3, proj3, small3, conv_w, conv_w, conv_w, conv_b, conv_b, conv_b,
      prm, d_rep, norm_w)


def _out_proj_kernel(a_ref, b_ref, wa_ref, wb_ref, x_ref, o_ref):
    acc = jnp.dot(a_ref[...], wa_ref[...], preferred_element_type=F32)
    acc = acc + jnp.dot(b_ref[...], wb_ref[...], preferred_element_type=F32)
    o_ref[...] = x_ref[...] + acc


def _out_proj(o_a, o_b, w_a, w_b, x2, tm=512, tn=1024):
    tok, d = x2.shape
    ka = o_a.shape[1]
    kb = o_b.shape[1]
    return pl.pallas_call(
        _out_proj_kernel,
        grid=(tok // tm, d // tn),
        in_specs=[
            pl.BlockSpec((tm, ka), lambda i, j: (i, 0)),
            pl.BlockSpec((tm, kb), lambda i, j: (i, 0)),
            pl.BlockSpec((ka, tn), lambda i, j: (0, j)),
            pl.BlockSpec((kb, tn), lambda i, j: (0, j)),
            pl.BlockSpec((tm, tn), lambda i, j: (i, j)),
        ],
        out_specs=pl.BlockSpec((tm, tn), lambda i, j: (i, j)),
        out_shape=jax.ShapeDtypeStruct((tok, d), F32),
        compiler_params=_params(("parallel", "parallel")),
        name="out_proj",
    )(o_a, o_b, w_a, w_b, x2)


def _router_kernel(h_ref, nw_ref, wr_ref, br_ref, u_ref, idx_ref, gate_ref, rank_ref, cnt_ref, carry_s):
    step = pl.program_id(0)
    tm = h_ref.shape[0]

    @pl.when(step == 0)
    def _():
        carry_s[...] = jnp.zeros_like(carry_s)

    u = _rms(h_ref[...], nw_ref[...])
    u_ref[...] = _pack_bf16_pairs(u)
    logits = _mm_exact(u, wr_ref[...]) + br_ref[...]
    lane = lax.broadcasted_iota(jnp.int32, (tm, LANES), 1).astype(F32)
    work = logits
    vals, idxs = [], []
    for _ in range(TOP_K):
        m = jnp.max(work, axis=1, keepdims=True)
        idx = jnp.min(jnp.where(work == m, lane, float(LANES)), axis=1, keepdims=True)
        vals.append(m)
        idxs.append(idx)
        work = jnp.where(lane == idx, -jnp.inf, work)
    onehot = (work == -jnp.inf).astype(F32)
    exps = [jnp.exp(v - vals[0]) for v in vals]
    denom = exps[0]
    for e in exps[1:]:
        denom = denom + e
    r = lax.broadcasted_iota(jnp.int32, (tm, tm), 0)
    c = lax.broadcasted_iota(jnp.int32, (tm, tm), 1)
    below = (r > c).astype(BF16)
    excl = jnp.dot(below, onehot.astype(BF16), preferred_element_type=F32) + carry_s[...]
    idx_out = jnp.zeros((tm, LANES), F32)
    gate_out = jnp.zeros((tm, LANES), F32)
    rank_out = jnp.zeros((tm, LANES), F32)
    for kk in range(TOP_K):
        rank_k = jnp.sum(jnp.where(lane == idxs[kk], excl, 0.0), axis=1, keepdims=True)
        idx_out = jnp.where(lane == kk, idxs[kk], idx_out)
        gate_out = jnp.where(lane == kk, exps[kk] / denom, gate_out)
        rank_out = jnp.where(lane == kk, rank_k, rank_out)
    idx_ref[...] = idx_out.astype(jnp.int32)
    gate_ref[...] = gate_out
    rank_ref[...] = rank_out.astype(jnp.int32)
    total = carry_s[...] + jnp.sum(onehot, axis=0, keepdims=True)
    carry_s[...] = total
    cnt_ref[...] = jnp.broadcast_to(total, cnt_ref.shape)


def _router(h2d, norm_w, w_r, b_r, tm=512):
    tok, d = h2d.shape
    return pl.pallas_call(
        _router_kernel,
        grid=(tok // tm,),
        in_specs=[
            pl.BlockSpec((tm, d), lambda i: (i, 0)),
            pl.BlockSpec((1, d), lambda i: (0, 0)),
            pl.BlockSpec((d, LANES), lambda i: (0, 0)),
            pl.BlockSpec((1, LANES), lambda i: (0, 0)),
        ],
        out_specs=[
            pl.BlockSpec((tm, d // 2), lambda i: (i, 0)),
            pl.BlockSpec((tm, LANES), lambda i: (i, 0)),
            pl.BlockSpec((tm, LANES), lambda i: (i, 0)),
            pl.BlockSpec((tm, LANES), lambda i: (i, 0)),
            pl.BlockSpec((8, LANES), lambda i: (0, 0)),
        ],
        out_shape=[
            jax.ShapeDtypeStruct((tok, d // 2), jnp.uint32),
            jax.ShapeDtypeStruct((tok, LANES), jnp.int32),
            jax.ShapeDtypeStruct((tok, LANES), F32),
            jax.ShapeDtypeStruct((tok, LANES), jnp.int32),
            jax.ShapeDtypeStruct((8, LANES), F32),
        ],
        scratch_shapes=[pltpu.VMEM((1, LANES), F32)],
        compiler_params=_params(("arbitrary",)),
        name="router",
    )(h2d, norm_w, w_r, b_r)


def _dispatch_kernel(pos_ref, u_ref, xs_in, xs_hbm, sem, *, tm):
    del xs_in

    def row_copy(t, k):
        return pltpu.make_async_copy(u_ref.at[pl.ds(t, 1)],
                                     xs_hbm.at[pl.ds(pos_ref[t * TOP_K + k], 1)], sem)

    def issue(t, carry):
        for k in range(TOP_K):
            row_copy(t, k).start()
        return carry

    def drain(t, carry):
        for k in range(TOP_K):
            row_copy(t, k).wait()
        return carry

    lax.fori_loop(0, tm, issue, 0)
    lax.fori_loop(0, tm, drain, 0)


def _dispatch(pos_flat, u2d, xs_init, tm=256):
    tok, d = u2d.shape
    kern = functools.partial(_dispatch_kernel, tm=tm)
    return pl.pallas_call(
        kern,
        grid=(tok // tm,),
        in_specs=[
            pl.BlockSpec((tm * TOP_K,), lambda i: (i,), memory_space=pltpu.SMEM),
            pl.BlockSpec((tm, d), lambda i: (i, 0)),
            pl.BlockSpec(memory_space=pl.ANY),
        ],
        out_specs=pl.BlockSpec(memory_space=pl.ANY),
        out_shape=jax.ShapeDtypeStruct(xs_init.shape, xs_init.dtype),
        scratch_shapes=[pltpu.SemaphoreType.DMA(())],
        input_output_aliases={2: 0},
        compiler_params=_params(("arbitrary",)),
        name="dispatch",
    )(pos_flat, u2d, xs_init)


def _moe_kernel(bexp_ref, nval_ref, x_ref, w1g_ref, w1l_ref, b1g_ref, b1l_ref, w2_ref, b2_ref,
                o_ref, xb_s, acc_s, *, sub):
    i = pl.program_id(0)
    j = pl.program_id(1)
    nf = pl.num_programs(1)
    nval = nval_ref[i]
    bm, d = xb_s.shape

    @pl.when((nval > 0) & (j == 0))
    def _():
        acc_s[...] = jnp.broadcast_to(b2_ref[...], acc_s.shape)

    for s in range(bm // sub):
        @pl.when((nval > s * sub) & (j == 0))
        def _(s=s):
            rows = pl.ds(s * sub, sub)
            lo, hi = _unpack_bf16_pairs(x_ref[rows, :])
            xb_s[rows, 0:d // 2] = lo.astype(BF16)
            xb_s[rows, d // 2:d] = hi.astype(BF16)

    for s in range(bm // sub):
        @pl.when(nval > s * sub)
        def _(s=s):
            rows = pl.ds(s * sub, sub)
            x = xb_s[rows, :]
            hg = jnp.dot(x, w1g_ref[...].astype(BF16), preferred_element_type=F32) + b1g_ref[...]
            hl = jnp.dot(x, w1l_ref[...].astype(BF16), preferred_element_type=F32) + b1l_ref[...]
            glu = jnp.minimum(hg, SWIGLU_LIMIT)
            lin = jnp.clip(hl, -SWIGLU_LIMIT, SWIGLU_LIMIT)
            act = glu * _sigmoid(SWIGLU_ALPHA * glu) * (lin + 1.0)
            acc_s[rows, :] += jnp.dot(act.astype(BF16), w2_ref[...].astype(BF16),
                                      preferred_element_type=F32)

    @pl.when((nval > 0) & (j == nf - 1))
    def _():
        o_ref[...] = _pack_bf16_pairs(acc_s[...])

    @pl.when((nval == 0) & (j == 0))
    def _():
        o_ref[...] = jnp.zeros_like(o_ref)


def _moe(bexp, nval, xs, w1, b1, w2, b2, *, bm, tf, sub):
    rows, dh = xs.shape
    d = 2 * dh
    ff = w2.shape[1]
    nb = rows // bm
    nf = ff // tf

    def jj(i, j, nv):
        return jnp.where(nv[i] > 0, j, nf - 1)

    grid_spec = pltpu.PrefetchScalarGridSpec(
        num_scalar_prefetch=2,
        grid=(nb, nf),
        in_specs=[
            pl.BlockSpec((bm, dh), lambda i, j, be, nv: (jnp.where(nv[i] > 0, i, 0), 0),
                         pipeline_mode=pl.Buffered(1)),
            pl.BlockSpec((None, d, tf), lambda i, j, be, nv: (be[i], 0, jj(i, j, nv))),
            pl.BlockSpec((None, d, tf), lambda i, j, be, nv: (be[i], 0, nf + jj(i, j, nv))),
            pl.BlockSpec((None, 1, tf), lambda i, j, be, nv: (be[i], 0, jj(i, j, nv))),
            pl.BlockSpec((None, 1, tf), lambda i, j, be, nv: (be[i], 0, nf + jj(i, j, nv))),
            pl.BlockSpec((None, tf, d), lambda i, j, be, nv: (be[i], jj(i, j, nv), 0)),
            pl.BlockSpec((None, 1, d), lambda i, j, be, nv: (be[i], 0, 0)),
        ],
        out_specs=pl.BlockSpec((bm, dh), lambda i, j, be, nv: (i, 0), pipeline_mode=pl.Buffered(1)),
        scratch_shapes=[
            pltpu.VMEM((bm, d), BF16),
            pltpu.VMEM((bm, d), F32),
        ],
    )
    return pl.pallas_call(
        functools.partial(_moe_kernel, sub=sub),
        grid_spec=grid_spec,
        out_shape=jax.ShapeDtypeStruct((rows, dh), jnp.uint32),
        compiler_params=_params(("arbitrary", "arbitrary")),
        name="moe",
    )(bexp, nval, xs, w1, w1, b1, b1, w2, b2)


def _combine_kernel(pos_ref, gate_ref, h_ref, ys_hbm, p_ref, pn_ref, wg_ref, wp_ref, fn_ref, o_ref,
                    gbuf, sem, *, tm):
    def row_copy(t, k):
        return pltpu.make_async_copy(ys_hbm.at[pl.ds(pos_ref[t * TOP_K + k], 1)],
                                     gbuf.at[k, pl.ds(t, 1)], sem)

    def issue(t, carry):
        for k in range(TOP_K):
            row_copy(t, k).start()
        return carry

    def drain(t, carry):
        for k in range(TOP_K):
            row_copy(t, k).wait()
        return carry

    lax.fori_loop(0, tm, issue, 0)
    lax.fori_loop(0, tm, drain, 0)

    gate = gate_ref[...]
    h = h_ref[...]
    dh = h.shape[1] // 2
    m_lo = jnp.zeros((tm, dh), F32)
    m_hi = jnp.zeros((tm, dh), F32)
    for k in range(TOP_K):
        lo, hi = _unpack_bf16_pairs(gbuf[k])
        m_lo = m_lo + gate[:, k:k + 1] * lo
        m_hi = m_hi + gate[:, k:k + 1] * hi
    h = h + jnp.concatenate([m_lo, m_hi], axis=1)
    u = _rms(h, pn_ref[...]).astype(BF16)
    gv = _sigmoid(jnp.dot(u, wg_ref[...], preferred_element_type=F32))
    pp = jnp.dot(p_ref[...].astype(BF16), wp_ref[...], preferred_element_type=F32)
    h = h + pp * gv
    o_ref[...] = _rms(h, fn_ref[...])


def _combine(pos_flat, gate, h2d, ys, p2d, ple_norm, w_gate, w_proj, final_norm, tm=256):
    tok, d = h2d.shape
    pd = p2d.shape[1]
    kern = functools.partial(_combine_kernel, tm=tm)
    return pl.pallas_call(
        kern,
        grid=(tok // tm,),
        in_specs=[
            pl.BlockSpec((tm * TOP_K,), lambda i: (i,), memory_space=pltpu.SMEM),
            pl.BlockSpec((tm, LANES), lambda i: (i, 0)),
            pl.BlockSpec((tm, d), lambda i: (i, 0)),
            pl.BlockSpec(memory_space=pl.ANY),
            pl.BlockSpec((tm, pd), lambda i: (i, 0)),
            pl.BlockSpec((1, d), lambda i: (0, 0)),
            pl.BlockSpec((d, d), lambda i: (0, 0)),
            pl.BlockSpec((pd, d), lambda i: (0, 0)),
            pl.BlockSpec((1, d), lambda i: (0, 0)),
        ],
        out_specs=pl.BlockSpec((tm, d), lambda i: (i, 0)),
        out_shape=jax.ShapeDtypeStruct((tok, d), F32),
        scratch_shapes=[pltpu.VMEM((TOP_K, tm, d // 2), jnp.uint32), pltpu.SemaphoreType.DMA(())],
        compiler_params=_params(("arbitrary",)),
        name="combine",
    )(pos_flat, gate, h2d, ys, p2d, ple_norm, w_gate, w_proj, final_norm)


def _pad_cols(v, off):
    return jnp.pad(v.astype(F32), (off, LANES - off - v.shape[0]))


def _layer(h3, p3, mix_norm, w_in, gdn_conv_w, gdn_a_log, gdn_dt_bias, gdn_norm_w, ssm_conv_w,
           ssm_conv_b, ssm_a_log, ssm_dt_bias, ssm_d, ssm_norm_w, w_out, ffn_norm, w_router, b_router,
           w_mlp1, b_mlp1, w_mlp2, b_mlp2, ple_norm, w_ple_gate, w_ple_proj, final_norm, *, moe_bm, moe_tf, moe_sub):
    bsz, seq, d = h3.shape
    tok = bsz * seq
    n_vh = gdn_a_log.shape[0]
    n_qk = n_vh // 2
    key_dim = n_qk * GDN_HEAD_DIM
    val_dim = n_vh * GDN_HEAD_DIM
    gdn_conv_ch = 2 * key_dim + val_dim
    ssm_heads = ssm_a_log.shape[0]
    inner = ssm_heads * SSM_HEAD_DIM
    ssm_conv_ch = inner + 2 * SSM_GROUPS * SSM_STATE

    o0 = 0
    o_qkv, o0 = o0, o0 + gdn_conv_ch
    o_z, o0 = o0, o0 + val_dim
    o_b, o0 = o0, o0 + n_vh
    o_a, o0 = o0, o0 + n_vh
    o_xbc, o0 = o0, o0 + ssm_conv_ch
    o_sz, o0 = o0, o0 + inner
    o_dt, o0 = o0, o0 + ssm_heads
    w_main = jnp.concatenate([w_in[:, o_qkv:o_b], w_in[:, o_xbc:o_dt]], axis=1).astype(BF16)
    w_small = jnp.concatenate([w_in[:, o_b:o_xbc], w_in[:, o_dt:o0]], axis=1)
    n_small = w_small.shape[1]
    w_small = jnp.pad(w_small, ((0, 0), (0, LANES - n_small))).astype(BF16)
    m_qkv, m_z = 0, gdn_conv_ch
    m_xbc = gdn_conv_ch + val_dim
    m_sz = m_xbc + ssm_conv_ch
    dt_col0 = 2 * n_vh

    x2 = h3.reshape(tok, d)
    proj, small = _in_proj(x2, mix_norm.reshape(1, d), w_main, w_small)
    proj3 = proj.reshape(bsz, seq, proj.shape[1])
    small3 = small.reshape(bsz, seq, LANES)

    zrow = jnp.zeros((LANES,), F32)
    gdn_prm = jnp.stack([_pad_cols(gdn_a_log, n_vh), _pad_cols(gdn_dt_bias, n_vh)] + [zrow] * 6)
    o_gdn = _gdn(proj3, small3, gdn_conv_w, gdn_prm, gdn_norm_w.reshape(1, GDN_HEAD_DIM),
                 qkv_off=m_qkv, z_off=m_z, n_qk=n_qk, out_cols=val_dim)

    ssm_prm = jnp.stack([_pad_cols(ssm_a_log, dt_col0), _pad_cols(ssm_dt_bias, dt_col0)] + [zrow] * 6)
    d_rep = jnp.repeat(ssm_d.astype(F32), SSM_HEAD_DIM).reshape(1, inner)
    o_ssm = _ssd(proj3, small3, ssm_conv_w, ssm_conv_b.reshape(1, ssm_conv_ch), ssm_prm, d_rep,
                 ssm_norm_w.reshape(1, inner), xbc_off=m_xbc, z_off=m_sz, inner=inner, dt_col0=dt_col0)

    w_out_b = w_out.astype(BF16)
    h1 = _out_proj(o_gdn.reshape(tok, val_dim), o_ssm.reshape(tok, inner),
                   w_out_b[:val_dim], w_out_b[val_dim:], x2)

    n_exp = w_router.shape[1]
    w_r = jnp.pad(w_router, ((0, 0), (0, LANES - n_exp)))
    b_r = jnp.pad(b_router.astype(F32), (0, LANES - n_exp), constant_values=NEG_BIG).reshape(1, LANES)
    u2, idx_l, gate_l, rank_l, cnt = _router(h1, ffn_norm.reshape(1, d), w_r, b_r)

    counts = cnt[0, :n_exp].astype(jnp.int32)
    padded = ((counts + moe_bm - 1) // moe_bm) * moe_bm
    pad_end = jnp.cumsum(padded)
    pad_start = pad_end - padded
    top_idx = idx_l[:, :TOP_K]
    pos = (pad_start[top_idx] + rank_l[:, :TOP_K]).astype(jnp.int32).reshape(-1)
    n_blocks = (tok * TOP_K) // moe_bm + n_exp
    block_start = jnp.arange(n_blocks, dtype=jnp.int32) * moe_bm
    bexp = jnp.minimum(jnp.sum(block_start[:, None] >= pad_end[None, :], axis=1), n_exp - 1).astype(jnp.int32)
    nval = jnp.clip(pad_start[bexp] + counts[bexp] - block_start, 0, moe_bm)
    nval = jnp.where(block_start < pad_end[-1], nval, 0).astype(jnp.int32)

    xs = _dispatch(pos, u2, jnp.zeros((n_blocks * moe_bm, d // 2), jnp.uint32))
    ff = w_mlp2.shape[1]
    ys = _moe(bexp, nval, xs, w_mlp1, b_mlp1.reshape(n_exp, 1, 2 * ff), w_mlp2,
              b_mlp2.reshape(n_exp, 1, d), bm=moe_bm, tf=moe_tf, sub=moe_sub)

    out = _combine(pos, gate_l, h1, ys, p3.reshape(tok, p3.shape[-1]), ple_norm.reshape(1, d),
                   w_ple_gate.astype(BF16), w_ple_proj.astype(BF16), final_norm.reshape(1, d))
    return out.reshape(bsz, seq, d)


def kernel(x, p, mix_norm, w_in, gdn_conv_w, gdn_a_log, gdn_dt_bias, gdn_norm_w, ssm_conv_w, ssm_conv_b,
           ssm_a_log, ssm_dt_bias, ssm_d, ssm_norm_w, w_out, ffn_norm, w_router, b_router, w_mlp1, b_mlp1,
           w_mlp2, b_mlp2, ple_norm, w_ple_gate, w_ple_proj, final_norm):
    assert mix_norm.shape[0] == 1, "single-layer trunk"
    return _layer(x, p[0], mix_norm[0], w_in[0], gdn_conv_w[0], gdn_a_log[0], gdn_dt_bias[0], gdn_norm_w[0],
                  ssm_conv_w[0], ssm_conv_b[0], ssm_a_log[0], ssm_dt_bias[0], ssm_d[0], ssm_norm_w[0],
                  w_out[0], ffn_norm[0], w_router[0], b_router[0], w_mlp1[0], b_mlp1[0], w_mlp2[0],
                  b_mlp2[0], ple_norm[0], w_ple_gate[0], w_ple_proj[0], final_norm, moe_bm=1024, moe_tf=512, moe_sub=256)
```

```python
import functools

import jax
import jax.numpy as jnp
from jax import lax
from jax.experimental import pallas as pl
from jax.experimental.pallas import tpu as pltpu

F32 = jnp.float32
BF16 = jnp.bfloat16
HIGHEST = lax.Precision.HIGHEST

NORM_EPS = 1e-6
CONV_WIDTH = 4
GDN_HEAD_DIM = 128
SSM_HEAD_DIM = 64
SSM_GROUPS = 4
SSM_STATE = 128
N_EXPERTS = 32
TOP_K = 4
SWIGLU_ALPHA = 1.702
SWIGLU_LIMIT = 7.0

LANES = 128
CHUNK = 128
GDN_CHUNKS_PER_TRIP = 8
VMEM_LIMIT = 56 * 1024 * 1024
MOE_VMEM_LIMIT = 60 * 1024 * 1024

NEG_BIG = -1e30


def _params(sem, vmem=VMEM_LIMIT):
    return pltpu.CompilerParams(dimension_semantics=sem, vmem_limit_bytes=vmem)


def _rms(x, w):
    ms = jnp.mean(x * x, axis=-1, keepdims=True)
    return x * lax.rsqrt(ms + NORM_EPS) * w


def _sigmoid(x):
    return 0.5 * jnp.tanh(0.5 * x) + 0.5


def _softplus(x):
    return jnp.maximum(x, 0.0) + jnp.log(1.0 + jnp.exp(-jnp.abs(x)))


def _mm(a, b):
    return jnp.dot(a.astype(BF16), b.astype(BF16), preferred_element_type=F32)


def _mm_nt(a, b):
    return lax.dot_general(a.astype(BF16), b.astype(BF16), (((1,), (1,)), ((), ())),
                           preferred_element_type=F32)


def _mm_tn(a, b):
    return lax.dot_general(a.astype(BF16), b.astype(BF16), (((0,), (0,)), ((), ())),
                           preferred_element_type=F32)


def _mm_exact(a, b):
    return jnp.dot(a, b, precision=HIGHEST, preferred_element_type=F32)


def _pack_bf16_pairs(x):
    n = x.shape[1] // 2
    xb = x.astype(BF16).astype(F32)
    lo = lax.bitcast_convert_type(xb[:, :n], jnp.uint32)
    hi = lax.bitcast_convert_type(xb[:, n:], jnp.uint32)
    return (lo >> 16) | (hi & jnp.uint32(0xFFFF0000))


def _unpack_bf16_pairs(w):
    lo = lax.bitcast_convert_type(w << 16, F32)
    hi = lax.bitcast_convert_type(w & jnp.uint32(0xFFFF0000), F32)
    return lo, hi


def _silu(x):
    h = 0.5 * x
    return h * jnp.tanh(h) + h


CONV_HALO = 16


def _causal_conv_silu(x_ref, w, bias, out_ref, chunked):
    seq = x_ref.shape[0]

    def conv(xv, masked):
        acc = xv * w[CONV_WIDTH - 1:CONV_WIDTH, :]
        for k in range(CONV_WIDTH - 1):
            shift = CONV_WIDTH - 1 - k
            xs = pltpu.roll(xv, shift, axis=0)
            if masked:
                row = lax.broadcasted_iota(jnp.int32, xv.shape, 0)
                xs = jnp.where(row >= shift, xs, 0.0)
            acc = acc + xs * w[k:k + 1, :]
        if bias is not None:
            acc = acc + bias
        return _silu(acc)

    if not chunked:
        out_ref[...] = conv(x_ref[...].astype(F32), False).astype(out_ref.dtype)
        out_ref[0:8, :] = conv(x_ref[0:CONV_HALO, :].astype(F32), True)[0:8, :].astype(out_ref.dtype)
        return

    out_ref[0:CHUNK, :] = conv(x_ref[0:CHUNK, :].astype(F32), True).astype(out_ref.dtype)

    def body(ci, carry):
        start = pl.multiple_of(ci * CHUNK, CHUNK)
        win = x_ref[pl.ds(pl.multiple_of(start - CONV_HALO, CONV_HALO), CHUNK + CONV_HALO), :]
        y = conv(win.astype(F32), False)
        out_ref[pl.ds(start, CHUNK), :] = y[CONV_HALO:, :].astype(out_ref.dtype)
        return carry

    lax.fori_loop(1, seq // CHUNK, body, 0)


def _bf16_pieces(x):
    hi = x.astype(BF16)
    r1 = x - hi.astype(F32)
    mid = r1.astype(BF16)
    lo = (r1 - mid.astype(F32)).astype(BF16)
    return hi, mid, lo


def _select_cols(vals, emat):
    out = None
    for piece in _bf16_pieces(vals):
        t = jnp.dot(piece, emat, preferred_element_type=F32)
        out = t if out is None else out + t
    return out


def _chunk_cumsum(vals, out_ref):
    seq = vals.shape[0]
    r = lax.broadcasted_iota(jnp.int32, (CHUNK, CHUNK), 0)
    c = lax.broadcasted_iota(jnp.int32, (CHUNK, CHUNK), 1)
    tri = (r >= c).astype(BF16)
    pieces = _bf16_pieces(vals)
    for i in range(seq // CHUNK):
        rows = slice(i * CHUNK, (i + 1) * CHUNK)
        acc = None
        for piece in pieces:
            t = jnp.dot(tri, piece[rows, :], preferred_element_type=F32)
            acc = t if acc is None else acc + t
        out_ref[rows, :] = acc


def _in_proj_kernel(x_ref, g_ref, w_ref, ws_ref, o_ref, os_ref, u_ref):
    @pl.when(pl.program_id(1) == 0)
    def _():
        u = _rms(x_ref[...], g_ref[...]).astype(BF16)
        u_ref[...] = u
        os_ref[...] = jnp.dot(u, ws_ref[...], preferred_element_type=F32)

    o_ref[...] = jnp.dot(u_ref[...], w_ref[...], preferred_element_type=F32).astype(o_ref.dtype)


def _in_proj(x2, norm_w, w_main, w_small, tm=512, tn=1024):
    tok, d = x2.shape
    n = w_main.shape[1]
    return pl.pallas_call(
        _in_proj_kernel,
        grid=(tok // tm, n // tn),
        in_specs=[
            pl.BlockSpec((tm, d), lambda i, j: (i, 0)),
            pl.BlockSpec((1, d), lambda i, j: (0, 0)),
            pl.BlockSpec((d, tn), lambda i, j: (0, j)),
            pl.BlockSpec((d, LANES), lambda i, j: (0, 0)),
        ],
        out_specs=[
            pl.BlockSpec((tm, tn), lambda i, j: (i, j)),
            pl.BlockSpec((tm, LANES), lambda i, j: (i, 0)),
        ],
        out_shape=[
            jax.ShapeDtypeStruct((tok, n), BF16),
            jax.ShapeDtypeStruct((tok, LANES), F32),
        ],
        scratch_shapes=[pltpu.VMEM((tm, d), BF16)],
        compiler_params=_params(("parallel", "arbitrary")),
        name="in_proj",
    )(x2, norm_w, w_main, w_small)


def _tri_masks(n):
    r = lax.broadcasted_iota(jnp.int32, (n, n), 0)
    c = lax.broadcasted_iota(jnp.int32, (n, n), 1)
    same = lambda s: (r // s) == (c // s)
    bd16 = same(16).astype(F32)
    levels = []
    s = 16
    while s < n:
        levels.append((same(2 * s) & jnp.logical_not(same(s))).astype(F32))
        s *= 2
    return bd16, levels


def _unit_lower_inverse(lmats, eye, bd16, levels):
    lds = [l * bd16 for l in lmats]
    ps = [eye - ld for ld in lds]
    ms = [_mm(ld, ld) for ld in lds]
    for it in range(3):
        ps = [p + _mm(p, m) for p, m in zip(ps, ms)]
        if it < 2:
            ms = [_mm(m, m) for m in ms]
    for msk in levels:
        ts = [_mm(l * msk, p) for l, p in zip(lmats, ps)]
        ps = [p - _mm(p, t) for p, t in zip(ps, ts)]
    return ps


def _gdn_kernel(q_ref, k_ref, v_ref, z_ref, sm_ref, wq_ref, wk_ref, wv_ref, prm_ref, nw_ref,
                o_ref, gc_ref, gct_ref, cv_s, q_s, k_s, kb_s, rhs_s, qd_s, a_s, b_s, qp_s, oacc_s, *, n_vh):
    hq = pl.program_id(1)
    seq = q_ref.shape[0]
    dk = GDN_HEAD_DIM
    nchunk = seq // CHUNK

    _causal_conv_silu(q_ref, wq_ref[...], None, cv_s.at[:, 0:dk], chunked=False)
    q = cv_s[:, 0:dk]
    _causal_conv_silu(k_ref, wk_ref[...], None, cv_s.at[:, 0:dk], chunked=False)
    k = cv_s[:, 0:dk]
    _causal_conv_silu(v_ref, wv_ref[...], None, cv_s, chunked=False)
    v = cv_s[...]
    q = q * lax.rsqrt(jnp.sum(q * q, axis=-1, keepdims=True) + NORM_EPS) * (dk ** -0.5)
    k = k * lax.rsqrt(jnp.sum(k * k, axis=-1, keepdims=True) + NORM_EPS)
    q_s[...] = q.astype(BF16)
    k_s[...] = k.astype(BF16)

    sm = sm_ref[...]
    beta_all = _sigmoid(sm)
    g_all = -jnp.exp(prm_ref[0:1, :]) * _softplus(sm + prm_ref[1:2, :])
    r = lax.broadcasted_iota(jnp.int32, (LANES, LANES), 0)
    c = lax.broadcasted_iota(jnp.int32, (LANES, LANES), 1)
    sel_g = ((c < 2) & (r == n_vh + 2 * hq + c)).astype(BF16)
    sel_b = ((c < 2) & (r == 2 * hq + c)).astype(BF16)
    g_sel = _select_cols(g_all, sel_g)
    beta = _select_cols(beta_all, sel_b)
    _chunk_cumsum(g_sel, gc_ref)
    gc = gc_ref[...]
    gct_ref[...] = gc.T
    eg = jnp.exp(gc)
    for hv in range(2):
        bcol = beta[:, hv:hv + 1]
        ecol = eg[:, hv:hv + 1]
        kb = k * bcol
        kb_s[hv] = kb.astype(BF16)
        rhs_s[hv, :, 0:dk] = (v[:, hv * dk:(hv + 1) * dk] * bcol).astype(BF16)
        rhs_s[hv, :, dk:2 * dk] = (kb * ecol).astype(BF16)
        qd_s[hv] = (q * ecol).astype(BF16)

    rr = lax.broadcasted_iota(jnp.int32, (CHUNK, CHUNK), 0)
    cc = lax.broadcasted_iota(jnp.int32, (CHUNK, CHUNK), 1)
    causal = rr >= cc
    strict = rr > cc
    eye = (rr == cc).astype(F32)
    bd16, levels = _tri_masks(CHUNK)

    unroll = GDN_CHUNKS_PER_TRIP

    def terms_body(ci, carry):
        sls = [pl.ds(pl.multiple_of((ci * unroll + s) * CHUNK, CHUNK), CHUNK) for s in range(unroll)]
        k_cs = [k_s[sl, :] for sl in sls]
        qks = [_mm_nt(q_s[sl, :], k_c) for sl, k_c in zip(sls, k_cs)]
        gcols = [gc_ref[sl, :] for sl in sls]
        grows = [gct_ref[:, sl] for sl in sls]
        chains = [(s, hv) for s in range(unroll) for hv in range(2)]
        gc_cs = [gcols[s][:, hv:hv + 1] for s, hv in chains]
        decays = [jnp.where(causal, jnp.exp(jnp.minimum(gc_c - grows[s][hv:hv + 1, :], 0.0)), 0.0)
                  for gc_c, (s, hv) in zip(gc_cs, chains)]
        lmats = [jnp.where(strict, _mm_nt(kb_s[hv, sls[s], :], k_cs[s]) * dec, 0.0)
                 for dec, (s, hv) in zip(decays, chains)]
        pinvs = _unit_lower_inverse(lmats, eye, bd16, levels)
        uws = [_mm(pinv, rhs_s[hv, sls[s], :]) for pinv, (s, hv) in zip(pinvs, chains)]
        kds = [k_cs[s].astype(F32) * jnp.exp(gc_c[CHUNK - 1:CHUNK, :] - gc_c)
               for gc_c, (s, hv) in zip(gc_cs, chains)]
        kdws = [_mm_tn(kd, uw) for kd, uw in zip(kds, uws)]
        aws = [_mm(qks[s] * dec, uw) for dec, uw, (s, hv) in zip(decays, uws, chains)]
        for kdw, aw, (s, hv) in zip(kdws, aws, chains):
            sl = sls[s]
            a_s[hv, sl, :] = (-kdw[:, dk:2 * dk]).astype(BF16)
            b_s[hv, sl, :] = kdw[:, 0:dk]
            qp_s[hv, sl, :] = (qd_s[hv, sl, :].astype(F32) - aw[:, dk:2 * dk]).astype(BF16)
            oacc_s[hv, sl, :] = aw[:, 0:dk]
        return carry

    lax.fori_loop(0, nchunk // unroll, terms_body, 0)

    def recur_body(ci, states):
        sl = pl.ds(pl.multiple_of(ci * CHUNK, CHUNK), CHUNK)
        g_last = gc_ref[pl.ds(ci * CHUNK + CHUNK - 1, 1), :]
        new_states = []
        for hv in range(2):
            sb = states[hv].astype(BF16)
            oacc_s[hv, sl, :] += _mm(qp_s[hv, sl, :], sb)
            new_states.append(states[hv] * jnp.exp(g_last[:, hv:hv + 1])
                              + _mm(a_s[hv, sl, :], sb) + b_s[hv, sl, :])
        return tuple(new_states)

    zero = jnp.zeros((dk, dk), F32)
    lax.fori_loop(0, nchunk, recur_body, (zero, zero))

    z = z_ref[...].astype(F32)
    nw = nw_ref[...]
    for hv in range(2):
        o = _rms(oacc_s[hv], nw)
        zz = z[:, hv * dk:(hv + 1) * dk]
        o_ref[:, hv * dk:(hv + 1) * dk] = (o * _silu(zz)).astype(o_ref.dtype)


def _gdn(proj3, small3, conv_w, prm, norm_w, *, qkv_off, z_off, n_qk, out_cols):
    bsz, seq, _ = proj3.shape
    assert seq % (CHUNK * GDN_CHUNKS_PER_TRIP) == 0
    dk = GDN_HEAD_DIM
    qb = qkv_off // dk
    kb = qb + n_qk
    vb = (qkv_off + 2 * n_qk * dk) // (2 * dk)
    zb = z_off // (2 * dk)
    cq = 0
    ck = n_qk
    cv = (2 * n_qk * dk) // (2 * dk)
    return pl.pallas_call(
        functools.partial(_gdn_kernel, n_vh=2 * n_qk),
        grid=(bsz, n_qk),
        in_specs=[
            pl.BlockSpec((None, seq, dk), lambda b, h: (b, 0, qb + h)),
            pl.BlockSpec((None, seq, dk), lambda b, h: (b, 0, kb + h)),
            pl.BlockSpec((None, seq, 2 * dk), lambda b, h: (b, 0, vb + h)),
            pl.BlockSpec((None, seq, 2 * dk), lambda b, h: (b, 0, zb + h)),
            pl.BlockSpec((None, seq, LANES), lambda b, h: (b, 0, 0)),
            pl.BlockSpec((CONV_WIDTH, dk), lambda b, h: (0, cq + h)),
            pl.BlockSpec((CONV_WIDTH, dk), lambda b, h: (0, ck + h)),
            pl.BlockSpec((CONV_WIDTH, 2 * dk), lambda b, h: (0, cv + h)),
            pl.BlockSpec((8, LANES), lambda b, h: (0, 0)),
            pl.BlockSpec((1, dk), lambda b, h: (0, 0)),
        ],
        out_specs=pl.BlockSpec((None, seq, 2 * dk), lambda b, h: (b, 0, h)),
        out_shape=jax.ShapeDtypeStruct((bsz, seq, out_cols), BF16),
        scratch_shapes=[
            pltpu.VMEM((seq, LANES), F32),
            pltpu.VMEM((LANES, seq), F32),
            pltpu.VMEM((seq, 2 * dk), F32),
            pltpu.VMEM((seq, dk), BF16),
            pltpu.VMEM((seq, dk), BF16),
            pltpu.VMEM((2, seq, dk), BF16),
            pltpu.VMEM((2, seq, 2 * dk), BF16),
            pltpu.VMEM((2, seq, dk), BF16),
            pltpu.VMEM((2, seq, dk), BF16),
            pltpu.VMEM((2, seq, dk), F32),
            pltpu.VMEM((2, seq, dk), BF16),
            pltpu.VMEM((2, seq, dk), F32),
        ],
        compiler_params=_params(("parallel", "parallel")),
        name="gdn",
    )(proj3, proj3, proj3, proj3, small3, conv_w, conv_w, conv_w, prm, norm_w)


def _ssd_kernel(x_ref, b_ref, c_ref, z_ref, sm_ref, wx_ref, wb_ref, wc_ref, cbx_ref, cbb_ref, cbc_ref,
                prm_ref, drep_ref, nw_ref, o_ref,
                acs_ref, acst_ref, cv_s, xs_s, bt_s, c_s, st_s, *, dt_col0, heads_per_group):
    grp = pl.program_id(1)
    seq = x_ref.shape[0]
    hp = SSM_HEAD_DIM
    hg = heads_per_group
    npair = hg // 2
    gw = hg * hp
    nchunk = seq // CHUNK

    _causal_conv_silu(x_ref, wx_ref[...], cbx_ref[...], xs_s, chunked=True)
    _causal_conv_silu(b_ref, wb_ref[...], cbb_ref[...], cv_s, chunked=True)
    bt_s[...] = cv_s[...].T.astype(BF16)
    _causal_conv_silu(c_ref, wc_ref[...], cbc_ref[...], cv_s, chunked=True)
    c_s[...] = cv_s[...].astype(BF16)

    sm = sm_ref[...]
    dt_all = _softplus(sm + prm_ref[1:2, :])
    a_all = -jnp.exp(prm_ref[0:1, :]) * dt_all
    r = lax.broadcasted_iota(jnp.int32, (LANES, LANES), 0)
    c = lax.broadcasted_iota(jnp.int32, (LANES, LANES), 1)
    sel = ((c < hg) & (r == dt_col0 + hg * grp + c)).astype(BF16)
    a_g = _select_cols(a_all, sel)
    _chunk_cumsum(a_g, acs_ref)
    acst_ref[...] = acs_ref[...].T
    st_s[...] = jnp.zeros_like(st_s)

    def replicate(width):
        rr_ = lax.broadcasted_iota(jnp.int32, (LANES, hg * width), 0)
        cc_ = lax.broadcasted_iota(jnp.int32, (LANES, hg * width), 1)
        return rr_, cc_ // width

    r64, c64 = replicate(hp)
    rep_a = (r64 == c64).astype(BF16)
    rep_dt = (r64 == dt_col0 + hg * grp + c64).astype(BF16)
    r128, c128 = replicate(CHUNK)
    rep_full = (r128 == c128).astype(BF16)
    lane = lax.broadcasted_iota(jnp.int32, (CHUNK, gw), 1)
    first = (lane % (2 * hp)) < hp
    rr = lax.broadcasted_iota(jnp.int32, (CHUNK, CHUNK), 0)
    cc = lax.broadcasted_iota(jnp.int32, (CHUNK, CHUNK), 1)
    causal = rr >= cc
    dt_bias = prm_ref[1:2, :]

    def chunk_body(ci, carry):
        sl = pl.ds(pl.multiple_of(ci * CHUNK, CHUNK), CHUNK)
        bt_c = bt_s[:, sl]
        c_c = c_s[sl, :]
        cb = jnp.dot(c_c, bt_c, preferred_element_type=F32)
        acs_c = acs_ref[sl, :]
        arow = acst_ref[:, sl]
        acrep = _select_cols(acs_c, rep_a)
        acf = _select_cols(acs_c, rep_full)
        xs_c = xs_s[sl, :]
        xdt = xs_c * _select_cols(_softplus(sm_ref[sl, :] + dt_bias), rep_dt)
        eac = jnp.exp(acrep)
        e_last = eac[CHUNK - 1:CHUNK, :]
        xdec = (xdt * jnp.exp(acrep[CHUNK - 1:CHUNK, :] - acrep)).astype(BF16)
        xlo = jnp.where(first, xdt, 0.0).astype(BF16)
        xhi = jnp.where(first, 0.0, xdt).astype(BF16)
        lms = [cb * jnp.where(causal, jnp.exp(jnp.minimum(
            acf[:, j * CHUNK:(j + 1) * CHUNK] - arow[j:j + 1, :], 0.0)), 0.0) for j in range(hg)]
        pairs = range(npair)
        cols = [slice(p * 2 * hp, (p + 1) * 2 * hp) for p in pairs]
        prevs = [st_s[p] for p in pairs]
        y_offs = [_mm(c_c, prev) for prev in prevs]
        y_diags = [jnp.dot(jnp.concatenate([lms[2 * p], lms[2 * p + 1]], axis=1).astype(BF16),
                           jnp.concatenate([xlo[:, cols[p]], xhi[:, cols[p]]], axis=0),
                           preferred_element_type=F32) for p in pairs]
        st_news = [jnp.dot(bt_c, xdec[:, cols[p]], preferred_element_type=F32) for p in pairs]
        for p in pairs:
            st_s[p] = prevs[p] * e_last[:, cols[p]] + st_news[p]
        y = jnp.concatenate([y_diags[p] + y_offs[p] * eac[:, cols[p]] for p in pairs], axis=1)
        y = (y + drep_ref[...] * xs_c) * _silu(z_ref[sl, :].astype(F32))
        o_ref[sl, :] = _rms(y, nw_ref[...]).astype(o_ref.dtype)
        return carry

    lax.fori_loop(0, nchunk, chunk_body, 0)


def _ssd(proj3, small3, conv_w, conv_b, prm, d_rep, norm_w, *, xbc_off, z_off, inner, dt_col0):
    bsz, seq, _ = proj3.shape
    gw = inner // SSM_GROUPS
    hg = gw // SSM_HEAD_DIM
    ns = SSM_STATE
    xb = xbc_off // gw
    bb = (xbc_off + inner) // ns
    cb = bb + SSM_GROUPS
    zb = z_off // gw
    wbb = inner // ns
    wcb = wbb + SSM_GROUPS
    kern = functools.partial(_ssd_kernel, dt_col0=dt_col0, heads_per_group=hg)
    return pl.pallas_call(
        kern,
        grid=(bsz, SSM_GROUPS),
        in_specs=[
            pl.BlockSpec((None, seq, gw), lambda b, g: (b, 0, xb + g)),
            pl.BlockSpec((None, seq, ns), lambda b, g: (b, 0, bb + g)),
            pl.BlockSpec((None, seq, ns), lambda b, g: (b, 0, cb + g)),
            pl.BlockSpec((None, seq, gw), lambda b, g: (b, 0, zb + g)),
            pl.BlockSpec((None, seq, LANES), lambda b, g: (b, 0, 0)),
            pl.BlockSpec((CONV_WIDTH, gw), lambda b, g: (0, g)),
            pl.BlockSpec((CONV_WIDTH, ns), lambda b, g: (0, wbb + g)),
            pl.BlockSpec((CONV_WIDTH, ns), lambda b, g: (0, wcb + g)),
            pl.BlockSpec((1, gw), lambda b, g: (0, g)),
            pl.BlockSpec((1, ns), lambda b, g: (0, wbb + g)),
            pl.BlockSpec((1, ns), lambda b, g: (0, wcb + g)),
            pl.BlockSpec((8, LANES), lambda b, g: (0, 0)),
            pl.BlockSpec((1, gw), lambda b, g: (0, g)),
            pl.BlockSpec((1, gw), lambda b, g: (0, g)),
        ],
        out_specs=pl.BlockSpec((None, seq, gw), lambda b, g: (b, 0, g)),
        out_shape=jax.ShapeDtypeStruct((bsz, seq, inner), BF16),
        scratch_shapes=[
            pltpu.VMEM((seq, LANES), F32),
            pltpu.VMEM((LANES, seq), F32),
            pltpu.VMEM((seq, ns), F32),
            pltpu.VMEM((seq, gw), F32),
            pltpu.VMEM((ns, seq), BF16),
            pltpu.VMEM((seq, ns), BF16),
            pltpu.VMEM((hg // 2, ns, 2 * SSM_HEAD_DIM), F32),
        ],
        compiler_params=_params(("parallel", "parallel")),
        name="ssd",
    )(proj3, proj3, proj3, proj3, small3, conv_w, conv_w, conv_w, conv_b, conv_b, conv_b,
      prm, d_rep, norm_w)


def _out_proj_kernel(a_ref, b_ref, wa_ref, wb_ref, x_ref, o_ref):
    acc = jnp.dot(a_ref[...], wa_ref[...], preferred_element_type=F32)
    acc = acc + jnp.dot(b_ref[...], wb_ref[...], preferred_element_type=F32)
    o_ref[...] = x_ref[...] + acc


def _out_proj(o_a, o_b, w_a, w_b, x2, tm=512, tn=1024):
    tok, d = x2.shape
    ka = o_a.shape[1]
    kb = o_b.shape[1]
    return pl.pallas_call(
        _out_proj_kernel,
        grid=(tok // tm, d // tn),
        in_specs=[
            pl.BlockSpec((tm, ka), lambda i, j: (i, 0)),
            pl.BlockSpec((tm, kb), lambda i, j: (i, 0)),
            pl.BlockSpec((ka, tn), lambda i, j: (0, j)),
            pl.BlockSpec((kb, tn), lambda i, j: (0, j)),
            pl.BlockSpec((tm, tn), lambda i, j: (i, j)),
        ],
        out_specs=pl.BlockSpec((tm, tn), lambda i, j: (i, j)),
        out_shape=jax.ShapeDtypeStruct((tok, d), F32),
        compiler_params=_params(("parallel", "parallel")),
        name="out_proj",
    )(o_a, o_b, w_a, w_b, x2)


def _router_kernel(h_ref, nw_ref, wr_ref, br_ref, u_ref, idx_ref, gate_ref, rank_ref, cnt_ref, carry_s):
    step = pl.program_id(0)
    tm = h_ref.shape[0]

    @pl.when(step == 0)
    def _():
        carry_s[...] = jnp.zeros_like(carry_s)

    u = _rms(h_ref[...], nw_ref[...])
    u_ref[...] = _pack_bf16_pairs(u)
    logits = _mm_exact(u, wr_ref[...]) + br_ref[...]
    lane = lax.broadcasted_iota(jnp.int32, (tm, LANES), 1).astype(F32)
    work = logits
    vals, idxs = [], []
    for _ in range(TOP_K):
        m = jnp.max(work, axis=1, keepdims=True)
        idx = jnp.min(jnp.where(work == m, lane, float(LANES)), axis=1, keepdims=True)
        vals.append(m)
        idxs.append(idx)
        work = jnp.where(lane == idx, -jnp.inf, work)
    onehot = (work == -jnp.inf).astype(F32)
    exps = [jnp.exp(v - vals[0]) for v in vals]
    denom = exps[0]
    for e in exps[1:]:
        denom = denom + e
    r = lax.broadcasted_iota(jnp.int32, (tm, tm), 0)
    c = lax.broadcasted_iota(jnp.int32, (tm, tm), 1)
    below = (r > c).astype(BF16)
    excl = jnp.dot(below, onehot.astype(BF16), preferred_element_type=F32) + carry_s[...]
    idx_out = jnp.zeros((tm, LANES), F32)
    gate_out = jnp.zeros((tm, LANES), F32)
    rank_out = jnp.zeros((tm, LANES), F32)
    for kk in range(TOP_K):
        rank_k = jnp.sum(jnp.where(lane == idxs[kk], excl, 0.0), axis=1, keepdims=True)
        idx_out = jnp.where(lane == kk, idxs[kk], idx_out)
        gate_out = jnp.where(lane == kk, exps[kk] / denom, gate_out)
        rank_out = jnp.where(lane == kk, rank_k, rank_out)
    idx_ref[...] = idx_out.astype(jnp.int32)
    gate_ref[...] = gate_out
    rank_ref[...] = rank_out.astype(jnp.int32)
    total = carry_s[...] + jnp.sum(onehot, axis=0, keepdims=True)
    carry_s[...] = total
    cnt_ref[...] = jnp.broadcast_to(total, cnt_ref.shape)


def _router(h2d, norm_w, w_r, b_r, tm=512):
    tok, d = h2d.shape
    return pl.pallas_call(
        _router_kernel,
        grid=(tok // tm,),
        in_specs=[
            pl.BlockSpec((tm, d), lambda i: (i, 0)),
            pl.BlockSpec((1, d), lambda i: (0, 0)),
            pl.BlockSpec((d, LANES), lambda i: (0, 0)),
            pl.BlockSpec((1, LANES), lambda i: (0, 0)),
        ],
        out_specs=[
            pl.BlockSpec((tm, d // 2), lambda i: (i, 0)),
            pl.BlockSpec((tm, LANES), lambda i: (i, 0)),
            pl.BlockSpec((tm, LANES), lambda i: (i, 0)),
            pl.BlockSpec((tm, LANES), lambda i: (i, 0)),
            pl.BlockSpec((8, LANES), lambda i: (0, 0)),
        ],
        out_shape=[
            jax.ShapeDtypeStruct((tok, d // 2), jnp.uint32),
            jax.ShapeDtypeStruct((tok, LANES), jnp.int32),
            jax.ShapeDtypeStruct((tok, LANES), F32),
            jax.ShapeDtypeStruct((tok, LANES), jnp.int32),
            jax.ShapeDtypeStruct((8, LANES), F32),
        ],
        scratch_shapes=[pltpu.VMEM((1, LANES), F32)],
        compiler_params=_params(("arbitrary",)),
        name="router",
    )(h2d, norm_w, w_r, b_r)


def _dispatch_kernel(pos_ref, u_ref, xs_in, xs_hbm, sem, *, tm):
    del xs_in

    def row_copy(t, k):
        return pltpu.make_async_copy(u_ref.at[pl.ds(t, 1)],
                                     xs_hbm.at[pl.ds(pos_ref[t * TOP_K + k], 1)], sem)

    def issue(t, carry):
        for k in range(TOP_K):
            row_copy(t, k).start()
        return carry

    def drain(t, carry):
        for k in range(TOP_K):
            row_copy(t, k).wait()
        return carry

    lax.fori_loop(0, tm, issue, 0)
    lax.fori_loop(0, tm, drain, 0)


def _dispatch(pos_flat, u2d, xs_init, tm=256):
    tok, d = u2d.shape
    kern = functools.partial(_dispatch_kernel, tm=tm)
    return pl.pallas_call(
        kern,
        grid=(tok // tm,),
        in_specs=[
            pl.BlockSpec((tm * TOP_K,), lambda i: (i,), memory_space=pltpu.SMEM),
            pl.BlockSpec((tm, d), lambda i: (i, 0)),
            pl.BlockSpec(memory_space=pl.ANY),
        ],
        out_specs=pl.BlockSpec(memory_space=pl.ANY),
        out_shape=jax.ShapeDtypeStruct(xs_init.shape, xs_init.dtype),
        scratch_shapes=[pltpu.SemaphoreType.DMA(())],
        input_output_aliases={2: 0},
        compiler_params=_params(("arbitrary",)),
        name="dispatch",
    )(pos_flat, u2d, xs_init)


def _moe_kernel(bexp_ref, nval_ref, x_ref, w1g_ref, w1l_ref, b1g_ref, b1l_ref, w2_ref, b2_ref,
                o_ref, acc_s, *, sub):
    i = pl.program_id(0)
    j = pl.program_id(1)
    nf = pl.num_programs(1)
    nval = nval_ref[i]
    bm, d = acc_s.shape

    @pl.when((nval > 0) & (j == 0))
    def _():
        acc_s[...] = jnp.broadcast_to(b2_ref[...], acc_s.shape)

    for s in range(bm // sub):
        @pl.when(nval > s * sub)
        def _(s=s):
            rows = pl.ds(s * sub, sub)
            lo, hi = _unpack_bf16_pairs(x_ref[rows, :])
            x = jnp.concatenate([lo.astype(BF16), hi.astype(BF16)], axis=1)
            hg = jnp.dot(x, w1g_ref[...].astype(BF16), preferred_element_type=F32) + b1g_ref[...]
            hl = jnp.dot(x, w1l_ref[...].astype(BF16), preferred_element_type=F32) + b1l_ref[...]
            glu = jnp.minimum(hg, SWIGLU_LIMIT)
            lin = jnp.clip(hl, -SWIGLU_LIMIT, SWIGLU_LIMIT)
            act = glu * _sigmoid(SWIGLU_ALPHA * glu) * (lin + 1.0)
            acc_s[rows, :] += jnp.dot(act.astype(BF16), w2_ref[...].astype(BF16),
                                      preferred_element_type=F32)

    @pl.when((nval > 0) & (j == nf - 1))
    def _():
        o_ref[...] = _pack_bf16_pairs(acc_s[...])

    @pl.when((nval == 0) & (j == 0))
    def _():
        o_ref[...] = jnp.zeros_like(o_ref)


def _moe(bexp, nval, xs, w1, b1, w2, b2, *, bm, tf, sub):
    rows, dh = xs.shape
    d = 2 * dh
    ff = w2.shape[1]
    nb = rows // bm
    nf = ff // tf

    def jj(i, j, nv):
        return jnp.where(nv[i] > 0, j, nf - 1)

    grid_spec = pltpu.PrefetchScalarGridSpec(
        num_scalar_prefetch=2,
        grid=(nb, nf),
        in_specs=[
            pl.BlockSpec((bm, dh), lambda i, j, be, nv: (jnp.where(nv[i] > 0, i, 0), 0)),
            pl.BlockSpec((None, d, tf), lambda i, j, be, nv: (be[i], 0, jj(i, j, nv))),
            pl.BlockSpec((None, d, tf), lambda i, j, be, nv: (be[i], 0, nf + jj(i, j, nv))),
            pl.BlockSpec((None, 1, tf), lambda i, j, be, nv: (be[i], 0, jj(i, j, nv))),
            pl.BlockSpec((None, 1, tf), lambda i, j, be, nv: (be[i], 0, nf + jj(i, j, nv))),
            pl.BlockSpec((None, tf, d), lambda i, j, be, nv: (be[i], jj(i, j, nv), 0)),
            pl.BlockSpec((None, 1, d), lambda i, j, be, nv: (be[i], 0, 0)),
        ],
        out_specs=pl.BlockSpec((bm, dh), lambda i, j, be, nv: (i, 0)),
        scratch_shapes=[
            pltpu.VMEM((bm, d), F32),
        ],
    )
    return pl.pallas_call(
        functools.partial(_moe_kernel, sub=sub),
        grid_spec=grid_spec,
        out_shape=jax.ShapeDtypeStruct((rows, dh), jnp.uint32),
        compiler_params=_params(("arbitrary", "arbitrary"), MOE_VMEM_LIMIT),
        name="moe",
    )(bexp, nval, xs, w1, w1, b1, b1, w2, b2)


def _combine_kernel(pos_ref, gate_ref, h_ref, ys_hbm, p_ref, pn_ref, wg_ref, wp_ref, fn_ref, o_ref,
                    gbuf, sem, *, tm):
    def row_copy(t, k):
        return pltpu.make_async_copy(ys_hbm.at[pl.ds(pos_ref[t * TOP_K + k], 1)],
                                     gbuf.at[k, pl.ds(t, 1)], sem)

    def issue(t, carry):
        for k in range(TOP_K):
            row_copy(t, k).start()
        return carry

    def drain(t, carry):
        for k in range(TOP_K):
            row_copy(t, k).wait()
        return carry

    lax.fori_loop(0, tm, issue, 0)
    lax.fori_loop(0, tm, drain, 0)

    gate = gate_ref[...]
    h = h_ref[...]
    dh = h.shape[1] // 2
    m_lo = jnp.zeros((tm, dh), F32)
    m_hi = jnp.zeros((tm, dh), F32)
    for k in range(TOP_K):
        lo, hi = _unpack_bf16_pairs(gbuf[k])
        m_lo = m_lo + gate[:, k:k + 1] * lo
        m_hi = m_hi + gate[:, k:k + 1] * hi
    h = h + jnp.concatenate([m_lo, m_hi], axis=1)
    u = _rms(h, pn_ref[...]).astype(BF16)
    gv = _sigmoid(jnp.dot(u, wg_ref[...], preferred_element_type=F32))
    pp = jnp.dot(p_ref[...].astype(BF16), wp_ref[...], preferred_element_type=F32)
    h = h + pp * gv
    o_ref[...] = _rms(h, fn_ref[...])


def _combine(pos_flat, gate, h2d, ys, p2d, ple_norm, w_gate, w_proj, final_norm, tm=256):
    tok, d = h2d.shape
    pd = p2d.shape[1]
    kern = functools.partial(_combine_kernel, tm=tm)
    return pl.pallas_call(
        kern,
        grid=(tok // tm,),
        in_specs=[
            pl.BlockSpec((tm * TOP_K,), lambda i: (i,), memory_space=pltpu.SMEM),
            pl.BlockSpec((tm, LANES), lambda i: (i, 0)),
            pl.BlockSpec((tm, d), lambda i: (i, 0)),
            pl.BlockSpec(memory_space=pl.ANY),
            pl.BlockSpec((tm, pd), lambda i: (i, 0)),
            pl.BlockSpec((1, d), lambda i: (0, 0)),
            pl.BlockSpec((d, d), lambda i: (0, 0)),
            pl.BlockSpec((pd, d), lambda i: (0, 0)),
            pl.BlockSpec((1, d), lambda i: (0, 0)),
        ],
        out_specs=pl.BlockSpec((tm, d), lambda i: (i, 0)),
        out_shape=jax.ShapeDtypeStruct((tok, d), F32),
        scratch_shapes=[pltpu.VMEM((TOP_K, tm, d // 2), jnp.uint32), pltpu.SemaphoreType.DMA(())],
        compiler_params=_params(("arbitrary",)),
        name="combine",
    )(pos_flat, gate, h2d, ys, p2d, ple_norm, w_gate, w_proj, final_norm)


def _pad_cols(v, off):
    return jnp.pad(v.astype(F32), (off, LANES - off - v.shape[0]))


def _layer(h3, p3, mix_norm, w_in, gdn_conv_w, gdn_a_log, gdn_dt_bias, gdn_norm_w, ssm_conv_w,
           ssm_conv_b, ssm_a_log, ssm_dt_bias, ssm_d, ssm_norm_w, w_out, ffn_norm, w_router, b_router,
           w_mlp1, b_mlp1, w_mlp2, b_mlp2, ple_norm, w_ple_gate, w_ple_proj, final_norm, *, moe_bm, moe_tf, moe_sub):
    bsz, seq, d = h3.shape
    tok = bsz * seq
    n_vh = gdn_a_log.shape[0]
    n_qk = n_vh // 2
    key_dim = n_qk * GDN_HEAD_DIM
    val_dim = n_vh * GDN_HEAD_DIM
    gdn_conv_ch = 2 * key_dim + val_dim
    ssm_heads = ssm_a_log.shape[0]
    inner = ssm_heads * SSM_HEAD_DIM
    ssm_conv_ch = inner + 2 * SSM_GROUPS * SSM_STATE

    o0 = 0
    o_qkv, o0 = o0, o0 + gdn_conv_ch
    o_z, o0 = o0, o0 + val_dim
    o_b, o0 = o0, o0 + n_vh
    o_a, o0 = o0, o0 + n_vh
    o_xbc, o0 = o0, o0 + ssm_conv_ch
    o_sz, o0 = o0, o0 + inner
    o_dt, o0 = o0, o0 + ssm_heads
    w_main = jnp.concatenate([w_in[:, o_qkv:o_b], w_in[:, o_xbc:o_dt]], axis=1).astype(BF16)
    w_small = jnp.concatenate([w_in[:, o_b:o_xbc], w_in[:, o_dt:o0]], axis=1)
    n_small = w_small.shape[1]
    w_small = jnp.pad(w_small, ((0, 0), (0, LANES - n_small))).astype(BF16)
    m_qkv, m_z = 0, gdn_conv_ch
    m_xbc = gdn_conv_ch + val_dim
    m_sz = m_xbc + ssm_conv_ch
    dt_col0 = 2 * n_vh

    x2 = h3.reshape(tok, d)
    proj, small = _in_proj(x2, mix_norm.reshape(1, d), w_main, w_small)
    proj3 = proj.reshape(bsz, seq, proj.shape[1])
    small3 = small.reshape(bsz, seq, LANES)

    zrow = jnp.zeros((LANES,), F32)
    gdn_prm = jnp.stack([_pad_cols(gdn_a_log, n_vh), _pad_cols(gdn_dt_bias, n_vh)] + [zrow] * 6)
    o_gdn = _gdn(proj3, small3, gdn_conv_w, gdn_prm, gdn_norm_w.reshape(1, GDN_HEAD_DIM),
                 qkv_off=m_qkv, z_off=m_z, n_qk=n_qk, out_cols=val_dim)

    ssm_prm = jnp.stack([_pad_cols(ssm_a_log, dt_col0), _pad_cols(ssm_dt_bias, dt_col0)] + [zrow] * 6)
    d_rep = jnp.repeat(ssm_d.astype(F32), SSM_HEAD_DIM).reshape(1, inner)
    o_ssm = _ssd(proj3, small3, ssm_conv_w, ssm_conv_b.reshape(1, ssm_conv_ch), ssm_prm, d_rep,
                 ssm_norm_w.reshape(1, inner), xbc_off=m_xbc, z_off=m_sz, inner=inner, dt_col0=dt_col0)

    w_out_b = w_out.astype(BF16)
    h1 = _out_proj(o_gdn.reshape(tok, val_dim), o_ssm.reshape(tok, inner),
                   w_out_b[:val_dim], w_out_b[val_dim:], x2)

    n_exp = w_router.shape[1]
    w_r = jnp.pad(w_router, ((0, 0), (0, LANES - n_exp)))
    b_r = jnp.pad(b_router.astype(F32), (0, LANES - n_exp), constant_values=NEG_BIG).reshape(1, LANES)
    u2, idx_l, gate_l, rank_l, cnt = _router(h1, ffn_norm.reshape(1, d), w_r, b_r)

    counts = cnt[0, :n_exp].astype(jnp.int32)
    padded = ((counts + moe_bm - 1) // moe_bm) * moe_bm
    pad_end = jnp.cumsum(padded)
    pad_start = pad_end - padded
    top_idx = idx_l[:, :TOP_K]
    pos = (pad_start[top_idx] + rank_l[:, :TOP_K]).astype(jnp.int32).reshape(-1)
    n_blocks = (tok * TOP_K) // moe_bm + n_exp
    block_start = jnp.arange(n_blocks, dtype=jnp.int32) * moe_bm
    bexp = jnp.minimum(jnp.sum(block_start[:, None] >= pad_end[None, :], axis=1), n_exp - 1).astype(jnp.int32)
    nval = jnp.clip(pad_start[bexp] + counts[bexp] - block_start, 0, moe_bm)
    nval = jnp.where(block_start < pad_end[-1], nval, 0).astype(jnp.int32)

    xs = _dispatch(pos, u2, jnp.zeros((n_blocks * moe_bm, d // 2), jnp.uint32))
    ff = w_mlp2.shape[1]
    ys = _moe(bexp, nval, xs, w_mlp1, b_mlp1.reshape(n_exp, 1, 2 * ff), w_mlp2,
              b_mlp2.reshape(n_exp, 1, d), bm=moe_bm, tf=moe_tf, sub=moe_sub)

    out = _combine(pos, gate_l, h1, ys, p3.reshape(tok, p3.shape[-1]), ple_norm.reshape(1, d),
                   w_ple_gate.astype(BF16), w_ple_proj.astype(BF16), final_norm.reshape(1, d))
    return out.reshape(bsz, seq, d)


def kernel(x, p, mix_norm, w_in, gdn_conv_w, gdn_a_log, gdn_dt_bias, gdn_norm_w, ssm_conv_w, ssm_conv_b,
           ssm_a_log, ssm_dt_bias, ssm_d, ssm_norm_w, w_out, ffn_norm, w_router, b_router, w_mlp1, b_mlp1,
           w_mlp2, b_mlp2, ple_norm, w_ple_gate, w_ple_proj, final_norm):
    assert mix_norm.shape[0] == 1, "single-layer trunk"
    return _layer(x, p[0], mix_norm[0], w_in[0], gdn_conv_w[0], gdn_a_log[0], gdn_dt_bias[0], gdn_norm_w[0],
                  ssm_conv_w[0], ssm_conv_b[0], ssm_a_log[0], ssm_dt_bias[0], ssm_d[0], ssm_norm_w[0],
                  w_out[0], ffn_norm[0], w_router[0], b_router[0], w_mlp1[0], b_mlp1[0], w_mlp2[0],
                  b_mlp2[0], ple_norm[0], w_ple_gate[0], w_ple_proj[0], final_norm, moe_bm=1024, moe_tf=512, moe_sub=256)
```

```python
import functools

import jax
import jax.numpy as jnp
from jax import lax
from jax.experimental import pallas as pl
from jax.experimental.pallas import tpu as pltpu

F32 = jnp.float32
BF16 = jnp.bfloat16
HIGHEST = lax.Precision.HIGHEST

NORM_EPS = 1e-6
CONV_WIDTH = 4
GDN_HEAD_DIM = 128
SSM_HEAD_DIM = 64
SSM_GROUPS = 4
SSM_STATE = 128
N_EXPERTS = 32
TOP_K = 4
SWIGLU_ALPHA = 1.702
SWIGLU_LIMIT = 7.0

LANES = 128
CHUNK = 128
GDN_CHUNKS_PER_TRIP = 8
VMEM_LIMIT = 56 * 1024 * 1024
MOE_VMEM_LIMIT = 60 * 1024 * 1024

NEG_BIG = -1e30


def _params(sem, vmem=VMEM_LIMIT):
    return pltpu.CompilerParams(dimension_semantics=sem, vmem_limit_bytes=vmem)


def _rms(x, w):
    ms = jnp.mean(x * x, axis=-1, keepdims=True)
    return x * lax.rsqrt(ms + NORM_EPS) * w


def _sigmoid(x):
    return 0.5 * jnp.tanh(0.5 * x) + 0.5


def _softplus(x):
    return jnp.maximum(x, 0.0) + jnp.log(1.0 + jnp.exp(-jnp.abs(x)))


def _mm(a, b):
    return jnp.dot(a.astype(BF16), b.astype(BF16), preferred_element_type=F32)


def _mm_nt(a, b):
    return lax.dot_general(a.astype(BF16), b.astype(BF16), (((1,), (1,)), ((), ())),
                           preferred_element_type=F32)


def _mm_tn(a, b):
    return lax.dot_general(a.astype(BF16), b.astype(BF16), (((0,), (0,)), ((), ())),
                           preferred_element_type=F32)


def _mm_exact(a, b):
    return jnp.dot(a, b, precision=HIGHEST, preferred_element_type=F32)


def _pack_bf16_pairs(x):
    n = x.shape[1] // 2
    xb = x.astype(BF16).astype(F32)
    lo = lax.bitcast_convert_type(xb[:, :n], jnp.uint32)
    hi = lax.bitcast_convert_type(xb[:, n:], jnp.uint32)
    return (lo >> 16) | (hi & jnp.uint32(0xFFFF0000))


def _unpack_bf16_pairs(w):
    lo = lax.bitcast_convert_type(w << 16, F32)
    hi = lax.bitcast_convert_type(w & jnp.uint32(0xFFFF0000), F32)
    return lo, hi


def _silu(x):
    h = 0.5 * x
    return h * jnp.tanh(h) + h


CONV_HALO = 16


def _causal_conv_silu(x_ref, w, bias, out_ref, chunked):
    seq = x_ref.shape[0]

    def conv(xv, masked):
        acc = xv * w[CONV_WIDTH - 1:CONV_WIDTH, :]
        for k in range(CONV_WIDTH - 1):
            shift = CONV_WIDTH - 1 - k
            xs = pltpu.roll(xv, shift, axis=0)
            if masked:
                row = lax.broadcasted_iota(jnp.int32, xv.shape, 0)
                xs = jnp.where(row >= shift, xs, 0.0)
            acc = acc + xs * w[k:k + 1, :]
        if bias is not None:
            acc = acc + bias
        return _silu(acc)

    if not chunked:
        out_ref[...] = conv(x_ref[...].astype(F32), False).astype(out_ref.dtype)
        out_ref[0:8, :] = conv(x_ref[0:CONV_HALO, :].astype(F32), True)[0:8, :].astype(out_ref.dtype)
        return

    out_ref[0:CHUNK, :] = conv(x_ref[0:CHUNK, :].astype(F32), True).astype(out_ref.dtype)

    def body(ci, carry):
        start = pl.multiple_of(ci * CHUNK, CHUNK)
        win = x_ref[pl.ds(pl.multiple_of(start - CONV_HALO, CONV_HALO), CHUNK + CONV_HALO), :]
        y = conv(win.astype(F32), False)
        out_ref[pl.ds(start, CHUNK), :] = y[CONV_HALO:, :].astype(out_ref.dtype)
        return carry

    lax.fori_loop(1, seq // CHUNK, body, 0)


def _bf16_pieces(x):
    hi = x.astype(BF16)
    r1 = x - hi.astype(F32)
    mid = r1.astype(BF16)
    lo = (r1 - mid.astype(F32)).astype(BF16)
    return hi, mid, lo


def _select_cols(vals, emat):
    out = None
    for piece in _bf16_pieces(vals):
        t = jnp.dot(piece, emat, preferred_element_type=F32)
        out = t if out is None else out + t
    return out


def _chunk_cumsum(vals, out_ref):
    seq = vals.shape[0]
    r = lax.broadcasted_iota(jnp.int32, (CHUNK, CHUNK), 0)
    c = lax.broadcasted_iota(jnp.int32, (CHUNK, CHUNK), 1)
    tri = (r >= c).astype(BF16)
    pieces = _bf16_pieces(vals)
    for i in range(seq // CHUNK):
        rows = slice(i * CHUNK, (i + 1) * CHUNK)
        acc = None
        for piece in pieces:
            t = jnp.dot(tri, piece[rows, :], preferred_element_type=F32)
            acc = t if acc is None else acc + t
        out_ref[rows, :] = acc


def _in_proj_kernel(x_ref, g_ref, w_ref, ws_ref, o_ref, os_ref, u_ref):
    @pl.when(pl.program_id(1) == 0)
    def _():
        u = _rms(x_ref[...], g_ref[...]).astype(BF16)
        u_ref[...] = u
        os_ref[...] = jnp.dot(u, ws_ref[...], preferred_element_type=F32)

    o_ref[...] = jnp.dot(u_ref[...], w_ref[...], preferred_element_type=F32).astype(o_ref.dtype)


def _in_proj(x2, norm_w, w_main, w_small, tm=512, tn=1024):
    tok, d = x2.shape
    n = w_main.shape[1]
    return pl.pallas_call(
        _in_proj_kernel,
        grid=(tok // tm, n // tn),
        in_specs=[
            pl.BlockSpec((tm, d), lambda i, j: (i, 0)),
            pl.BlockSpec((1, d), lambda i, j: (0, 0)),
            pl.BlockSpec((d, tn), lambda i, j: (0, j)),
            pl.BlockSpec((d, LANES), lambda i, j: (0, 0)),
        ],
        out_specs=[
            pl.BlockSpec((tm, tn), lambda i, j: (i, j)),
            pl.BlockSpec((tm, LANES), lambda i, j: (i, 0)),
        ],
        out_shape=[
            jax.ShapeDtypeStruct((tok, n), BF16),
            jax.ShapeDtypeStruct((tok, LANES), F32),
        ],
        scratch_shapes=[pltpu.VMEM((tm, d), BF16)],
        compiler_params=_params(("parallel", "arbitrary")),
        name="in_proj",
    )(x2, norm_w, w_main, w_small)


def _tri_masks(n):
    r = lax.broadcasted_iota(jnp.int32, (n, n), 0)
    c = lax.broadcasted_iota(jnp.int32, (n, n), 1)
    same = lambda s: (r // s) == (c // s)
    bd16 = same(16).astype(F32)
    levels = []
    s = 16
    while s < n:
        levels.append((same(2 * s) & jnp.logical_not(same(s))).astype(F32))
        s *= 2
    return bd16, levels


def _unit_lower_inverse(lmats, eye, bd16, levels):
    lds = [l * bd16 for l in lmats]
    ps = [eye - ld for ld in lds]
    ms = [_mm(ld, ld) for ld in lds]
    for it in range(3):
        ps = [p + _mm(p, m) for p, m in zip(ps, ms)]
        if it < 2:
            ms = [_mm(m, m) for m in ms]
    for msk in levels:
        ts = [_mm(l * msk, p) for l, p in zip(lmats, ps)]
        ps = [p - _mm(p, t) for p, t in zip(ps, ts)]
    return ps


def _gdn_kernel(q_ref, k_ref, v_ref, z_ref, sm_ref, wq_ref, wk_ref, wv_ref, prm_ref, nw_ref,
                o_ref, gc_ref, gct_ref, cv_s, q_s, k_s, kb_s, rhs_s, qd_s, a_s, b_s, qp_s, oacc_s, *, n_vh):
    hq = pl.program_id(1)
    seq = q_ref.shape[0]
    dk = GDN_HEAD_DIM
    nchunk = seq // CHUNK

    _causal_conv_silu(q_ref, wq_ref[...], None, cv_s.at[:, 0:dk], chunked=False)
    q = cv_s[:, 0:dk]
    _causal_conv_silu(k_ref, wk_ref[...], None, cv_s.at[:, 0:dk], chunked=False)
    k = cv_s[:, 0:dk]
    _causal_conv_silu(v_ref, wv_ref[...], None, cv_s, chunked=False)
    v = cv_s[...]
    q = q * lax.rsqrt(jnp.sum(q * q, axis=-1, keepdims=True) + NORM_EPS) * (dk ** -0.5)
    k = k * lax.rsqrt(jnp.sum(k * k, axis=-1, keepdims=True) + NORM_EPS)
    q_s[...] = q.astype(BF16)
    k_s[...] = k.astype(BF16)

    sm = sm_ref[...]
    beta_all = _sigmoid(sm)
    g_all = -jnp.exp(prm_ref[0:1, :]) * _softplus(sm + prm_ref[1:2, :])
    r = lax.broadcasted_iota(jnp.int32, (LANES, LANES), 0)
    c = lax.broadcasted_iota(jnp.int32, (LANES, LANES), 1)
    sel_g = ((c < 2) & (r == n_vh + 2 * hq + c)).astype(BF16)
    sel_b = ((c < 2) & (r == 2 * hq + c)).astype(BF16)
    g_sel = _select_cols(g_all, sel_g)
    beta = _select_cols(beta_all, sel_b)
    _chunk_cumsum(g_sel, gc_ref)
    gc = gc_ref[...]
    gct_ref[...] = gc.T
    eg = jnp.exp(gc)
    for hv in range(2):
        bcol = beta[:, hv:hv + 1]
        ecol = eg[:, hv:hv + 1]
        kb = k * bcol
        kb_s[hv] = kb.astype(BF16)
        rhs_s[hv, :, 0:dk] = (v[:, hv * dk:(hv + 1) * dk] * bcol).astype(BF16)
        rhs_s[hv, :, dk:2 * dk] = (kb * ecol).astype(BF16)
        qd_s[hv] = (q * ecol).astype(BF16)

    rr = lax.broadcasted_iota(jnp.int32, (CHUNK, CHUNK), 0)
    cc = lax.broadcasted_iota(jnp.int32, (CHUNK, CHUNK), 1)
    causal = rr >= cc
    strict = rr > cc
    eye = (rr == cc).astype(F32)
    bd16, levels = _tri_masks(CHUNK)

    unroll = GDN_CHUNKS_PER_TRIP

    def terms_body(ci, carry):
        sls = [pl.ds(pl.multiple_of((ci * unroll + s) * CHUNK, CHUNK), CHUNK) for s in range(unroll)]
        k_cs = [k_s[sl, :] for sl in sls]
        qks = [_mm_nt(q_s[sl, :], k_c) for sl, k_c in zip(sls, k_cs)]
        gcols = [gc_ref[sl, :] for sl in sls]
        grows = [gct_ref[:, sl] for sl in sls]
        chains = [(s, hv) for s in range(unroll) for hv in range(2)]
        gc_cs = [gcols[s][:, hv:hv + 1] for s, hv in chains]
        decays = [jnp.where(causal, jnp.exp(jnp.minimum(gc_c - grows[s][hv:hv + 1, :], 0.0)), 0.0)
                  for gc_c, (s, hv) in zip(gc_cs, chains)]
        lmats = [jnp.where(strict, _mm_nt(kb_s[hv, sls[s], :], k_cs[s]) * dec, 0.0)
                 for dec, (s, hv) in zip(decays, chains)]
        pinvs = _unit_lower_inverse(lmats, eye, bd16, levels)
        uws = [_mm(pinv, rhs_s[hv, sls[s], :]) for pinv, (s, hv) in zip(pinvs, chains)]
        kds = [k_cs[s].astype(F32) * jnp.exp(gc_c[CHUNK - 1:CHUNK, :] - gc_c)
               for gc_c, (s, hv) in zip(gc_cs, chains)]
        kdws = [_mm_tn(kd, uw) for kd, uw in zip(kds, uws)]
        aws = [_mm(qks[s] * dec, uw) for dec, uw, (s, hv) in zip(decays, uws, chains)]
        for kdw, aw, (s, hv) in zip(kdws, aws, chains):
            sl = sls[s]
            a_s[hv, sl, :] = (-kdw[:, dk:2 * dk]).astype(BF16)
            b_s[hv, sl, :] = kdw[:, 0:dk]
            qp_s[hv, sl, :] = (qd_s[hv, sl, :].astype(F32) - aw[:, dk:2 * dk]).astype(BF16)
            oacc_s[hv, sl, :] = aw[:, 0:dk]
        return carry

    lax.fori_loop(0, nchunk // unroll, terms_body, 0)

    def recur_body(ci, states):
        sl = pl.ds(pl.multiple_of(ci * CHUNK, CHUNK), CHUNK)
        g_last = gc_ref[pl.ds(ci * CHUNK + CHUNK - 1, 1), :]
        new_states = []
        for hv in range(2):
            sb = states[hv].astype(BF16)
            oacc_s[hv, sl, :] += _mm(qp_s[hv, sl, :], sb)
            new_states.append(states[hv] * jnp.exp(g_last[:, hv:hv + 1])
                              + _mm(a_s[hv, sl, :], sb) + b_s[hv, sl, :])
        return tuple(new_states)

    zero = jnp.zeros((dk, dk), F32)
    lax.fori_loop(0, nchunk, recur_body, (zero, zero))

    z = z_ref[...].astype(F32)
    nw = nw_ref[...]
    for hv in range(2):
        o = _rms(oacc_s[hv], nw)
        zz = z[:, hv * dk:(hv + 1) * dk]
        o_ref[:, hv * dk:(hv + 1) * dk] = (o * _silu(zz)).astype(o_ref.dtype)


def _gdn(proj3, small3, conv_w, prm, norm_w, *, qkv_off, z_off, n_qk, out_cols):
    bsz, seq, _ = proj3.shape
    assert seq % (CHUNK * GDN_CHUNKS_PER_TRIP) == 0
    dk = GDN_HEAD_DIM
    qb = qkv_off // dk
    kb = qb + n_qk
    vb = (qkv_off + 2 * n_qk * dk) // (2 * dk)
    zb = z_off // (2 * dk)
    cq = 0
    ck = n_qk
    cv = (2 * n_qk * dk) // (2 * dk)
    return pl.pallas_call(
        functools.partial(_gdn_kernel, n_vh=2 * n_qk),
        grid=(bsz, n_qk),
        in_specs=[
            pl.BlockSpec((None, seq, dk), lambda b, h: (b, 0, qb + h)),
            pl.BlockSpec((None, seq, dk), lambda b, h: (b, 0, kb + h)),
            pl.BlockSpec((None, seq, 2 * dk), lambda b, h: (b, 0, vb + h)),
            pl.BlockSpec((None, seq, 2 * dk), lambda b, h: (b, 0, zb + h)),
            pl.BlockSpec((None, seq, LANES), lambda b, h: (b, 0, 0)),
            pl.BlockSpec((CONV_WIDTH, dk), lambda b, h: (0, cq + h)),
            pl.BlockSpec((CONV_WIDTH, dk), lambda b, h: (0, ck + h)),
            pl.BlockSpec((CONV_WIDTH, 2 * dk), lambda b, h: (0, cv + h)),
            pl.BlockSpec((8, LANES), lambda b, h: (0, 0)),
            pl.BlockSpec((1, dk), lambda b, h: (0, 0)),
        ],
        out_specs=pl.BlockSpec((None, seq, 2 * dk), lambda b, h: (b, 0, h)),
        out_shape=jax.ShapeDtypeStruct((bsz, seq, out_cols), BF16),
        scratch_shapes=[
            pltpu.VMEM((seq, LANES), F32),
            pltpu.VMEM((LANES, seq), F32),
            pltpu.VMEM((seq, 2 * dk), F32),
            pltpu.VMEM((seq, dk), BF16),
            pltpu.VMEM((seq, dk), BF16),
            pltpu.VMEM((2, seq, dk), BF16),
            pltpu.VMEM((2, seq, 2 * dk), BF16),
            pltpu.VMEM((2, seq, dk), BF16),
            pltpu.VMEM((2, seq, dk), BF16),
            pltpu.VMEM((2, seq, dk), F32),
            pltpu.VMEM((2, seq, dk), BF16),
            pltpu.VMEM((2, seq, dk), F32),
        ],
        compiler_params=_params(("parallel", "parallel")),
        name="gdn",
    )(proj3, proj3, proj3, proj3, small3, conv_w, conv_w, conv_w, prm, norm_w)


def _ssd_kernel(x_ref, b_ref, c_ref, z_ref, sm_ref, wx_ref, wb_ref, wc_ref, cbx_ref, cbb_ref, cbc_ref,
                prm_ref, drep_ref, nw_ref, o_ref,
                acs_ref, acst_ref, cv_s, xs_s, bt_s, c_s, st_s, *, dt_col0, heads_per_group):
    grp = pl.program_id(1)
    seq = x_ref.shape[0]
    hp = SSM_HEAD_DIM
    hg = heads_per_group
    npair = hg // 2
    gw = hg * hp
    nchunk = seq // CHUNK

    _causal_conv_silu(x_ref, wx_ref[...], cbx_ref[...], xs_s, chunked=True)
    _causal_conv_silu(b_ref, wb_ref[...], cbb_ref[...], cv_s, chunked=True)
    bt_s[...] = cv_s[...].T.astype(BF16)
    _causal_conv_silu(c_ref, wc_ref[...], cbc_ref[...], cv_s, chunked=True)
    c_s[...] = cv_s[...].astype(BF16)

    sm = sm_ref[...]
    dt_all = _softplus(sm + prm_ref[1:2, :])
    a_all = -jnp.exp(prm_ref[0:1, :]) * dt_all
    r = lax.broadcasted_iota(jnp.int32, (LANES, LANES), 0)
    c = lax.broadcasted_iota(jnp.int32, (LANES, LANES), 1)
    sel = ((c < hg) & (r == dt_col0 + hg * grp + c)).astype(BF16)
    a_g = _select_cols(a_all, sel)
    _chunk_cumsum(a_g, acs_ref)
    acst_ref[...] = acs_ref[...].T
    st_s[...] = jnp.zeros_like(st_s)

    def replicate(width):
        rr_ = lax.broadcasted_iota(jnp.int32, (LANES, hg * width), 0)
        cc_ = lax.broadcasted_iota(jnp.int32, (LANES, hg * width), 1)
        return rr_, cc_ // width

    r64, c64 = replicate(hp)
    rep_a = (r64 == c64).astype(BF16)
    rep_dt = (r64 == dt_col0 + hg * grp + c64).astype(BF16)
    r128, c128 = replicate(CHUNK)
    rep_full = (r128 == c128).astype(BF16)
    lane = lax.broadcasted_iota(jnp.int32, (CHUNK, gw), 1)
    first = (lane % (2 * hp)) < hp
    rr = lax.broadcasted_iota(jnp.int32, (CHUNK, CHUNK), 0)
    cc = lax.broadcasted_iota(jnp.int32, (CHUNK, CHUNK), 1)
    causal = rr >= cc
    dt_bias = prm_ref[1:2, :]

    def chunk_body(ci, carry):
        sl = pl.ds(pl.multiple_of(ci * CHUNK, CHUNK), CHUNK)
        bt_c = bt_s[:, sl]
        c_c = c_s[sl, :]
        cb = jnp.dot(c_c, bt_c, preferred_element_type=F32)
        acs_c = acs_ref[sl, :]
        arow = acst_ref[:, sl]
        acrep = _select_cols(acs_c, rep_a)
        acf = _select_cols(acs_c, rep_full)
        xs_c = xs_s[sl, :]
        xdt = xs_c * _select_cols(_softplus(sm_ref[sl, :] + dt_bias), rep_dt)
        eac = jnp.exp(acrep)
        e_last = eac[CHUNK - 1:CHUNK, :]
        xdec = (xdt * jnp.exp(acrep[CHUNK - 1:CHUNK, :] - acrep)).astype(BF16)
        xlo = jnp.where(first, xdt, 0.0).astype(BF16)
        xhi = jnp.where(first, 0.0, xdt).astype(BF16)
        lms = [cb * jnp.where(causal, jnp.exp(jnp.minimum(
            acf[:, j * CHUNK:(j + 1) * CHUNK] - arow[j:j + 1, :], 0.0)), 0.0) for j in range(hg)]
        pairs = range(npair)
        cols = [slice(p * 2 * hp, (p + 1) * 2 * hp) for p in pairs]
        prevs = [st_s[p] for p in pairs]
        y_offs = [_mm(c_c, prev) for prev in prevs]
        y_diags = [jnp.dot(jnp.concatenate([lms[2 * p], lms[2 * p + 1]], axis=1).astype(BF16),
                           jnp.concatenate([xlo[:, cols[p]], xhi[:, cols[p]]], axis=0),
                           preferred_element_type=F32) for p in pairs]
        st_news = [jnp.dot(bt_c, xdec[:, cols[p]], preferred_element_type=F32) for p in pairs]
        for p in pairs:
            st_s[p] = prevs[p] * e_last[:, cols[p]] + st_news[p]
        y = jnp.concatenate([y_diags[p] + y_offs[p] * eac[:, cols[p]] for p in pairs], axis=1)
        y = (y + drep_ref[...] * xs_c) * _silu(z_ref[sl, :].astype(F32))
        o_ref[sl, :] = _rms(y, nw_ref[...]).astype(o_ref.dtype)
        return carry

    lax.fori_loop(0, nchunk, chunk_body, 0)


def _ssd(proj3, small3, conv_w, conv_b, prm, d_rep, norm_w, *, xbc_off, z_off, inner, dt_col0):
    bsz, seq, _ = proj3.shape
    gw = inner // SSM_GROUPS
    hg = gw // SSM_HEAD_DIM
    ns = SSM_STATE
    xb = xbc_off // gw
    bb = (xbc_off + inner) // ns
    cb = bb + SSM_GROUPS
    zb = z_off // gw
    wbb = inner // ns
    wcb = wbb + SSM_GROUPS
    kern = functools.partial(_ssd_kernel, dt_col0=dt_col0, heads_per_group=hg)
    return pl.pallas_call(
        kern,
        grid=(bsz, SSM_GROUPS),
        in_specs=[
            pl.BlockSpec((None, seq, gw), lambda b, g: (b, 0, xb + g)),
            pl.BlockSpec((None, seq, ns), lambda b, g: (b, 0, bb + g)),
            pl.BlockSpec((None, seq, ns), lambda b, g: (b, 0, cb + g)),
            pl.BlockSpec((None, seq, gw), lambda b, g: (b, 0, zb + g)),
            pl.BlockSpec((None, seq, LANES), lambda b, g: (b, 0, 0)),
            pl.BlockSpec((CONV_WIDTH, gw), lambda b, g: (0, g)),
            pl.BlockSpec((CONV_WIDTH, ns), lambda b, g: (0, wbb + g)),
            pl.BlockSpec((CONV_WIDTH, ns), lambda b, g: (0, wcb + g)),
            pl.BlockSpec((1, gw), lambda b, g: (0, g)),
            pl.BlockSpec((1, ns), lambda b, g: (0, wbb + g)),
            pl.BlockSpec((1, ns), lambda b, g: (0, wcb + g)),
            pl.BlockSpec((8, LANES), lambda b, g: (0, 0)),
            pl.BlockSpec((1, gw), lambda b, g: (0, g)),
            pl.BlockSpec((1, gw), lambda b, g: (0, g)),
        ],
        out_specs=pl.BlockSpec((None, seq, gw), lambda b, g: (b, 0, g)),
        out_shape=jax.ShapeDtypeStruct((bsz, seq, inner), BF16),
        scratch_shapes=[
            pltpu.VMEM((seq, LANES), F32),
            pltpu.VMEM((LANES, seq), F32),
            pltpu.VMEM((seq, ns), F32),
            pltpu.VMEM((seq, gw), F32),
            pltpu.VMEM((ns, seq), BF16),
            pltpu.VMEM((seq, ns), BF16),
            pltpu.VMEM((hg // 2, ns, 2 * SSM_HEAD_DIM), F32),
        ],
        compiler_params=_params(("parallel", "parallel")),
        name="ssd",
    )(proj3, proj3, proj3, proj3, small3, conv_w, conv_w, conv_w, conv_b, conv_b, conv_b,
      prm, d_rep, norm_w)


def _out_proj_kernel(a_ref, b_ref, wa_ref, wb_ref, x_ref, o_ref):
    acc = jnp.dot(a_ref[...], wa_ref[...], preferred_element_type=F32)
    acc = acc + jnp.dot(b_ref[...], wb_ref[...], preferred_element_type=F32)
    o_ref[...] = x_ref[...] + acc


def _out_proj(o_a, o_b, w_a, w_b, x2, tm=512, tn=1024):
    tok, d = x2.shape
    ka = o_a.shape[1]
    kb = o_b.shape[1]
    return pl.pallas_call(
        _out_proj_kernel,
        grid=(tok // tm, d // tn),
        in_specs=[
            pl.BlockSpec((tm, ka), lambda i, j: (i, 0)),
            pl.BlockSpec((tm, kb), lambda i, j: (i, 0)),
            pl.BlockSpec((ka, tn), lambda i, j: (0, j)),
            pl.BlockSpec((kb, tn), lambda i, j: (0, j)),
            pl.BlockSpec((tm, tn), lambda i, j: (i, j)),
        ],
        out_specs=pl.BlockSpec((tm, tn), lambda i, j: (i, j)),
        out_shape=jax.ShapeDtypeStruct((tok, d), F32),
        compiler_params=_params(("parallel", "parallel")),
        name="out_proj",
    )(o_a, o_b, w_a, w_b, x2)


def _router_kernel(h_ref, nw_ref, wr_ref, br_ref, u_ref, idx_ref, gate_ref, rank_ref, cnt_ref, carry_s):
    step = pl.program_id(0)
    tm = h_ref.shape[0]

    @pl.when(step == 0)
    def _():
        carry_s[...] = jnp.zeros_like(carry_s)

    u = _rms(h_ref[...], nw_ref[...])
    u_ref[...] = _pack_bf16_pairs(u)
    logits = _mm_exact(u, wr_ref[...]) + br_ref[...]
    lane = lax.broadcasted_iota(jnp.int32, (tm, LANES), 1).astype(F32)
    work = logits
    vals, idxs = [], []
    for _ in range(TOP_K):
        m = jnp.max(work, axis=1, keepdims=True)
        idx = jnp.min(jnp.where(work == m, lane, float(LANES)), axis=1, keepdims=True)
        vals.append(m)
        idxs.append(idx)
        work = jnp.where(lane == idx, -jnp.inf, work)
    onehot = (work == -jnp.inf).astype(F32)
    exps = [jnp.exp(v - vals[0]) for v in vals]
    denom = exps[0]
    for e in exps[1:]:
        denom = denom + e
    r = lax.broadcasted_iota(jnp.int32, (tm, tm), 0)
    c = lax.broadcasted_iota(jnp.int32, (tm, tm), 1)
    below = (r > c).astype(BF16)
    excl = jnp.dot(below, onehot.astype(BF16), preferred_element_type=F32) + carry_s[...]
    idx_out = jnp.zeros((tm, LANES), F32)
    gate_out = jnp.zeros((tm, LANES), F32)
    rank_out = jnp.zeros((tm, LANES), F32)
    for kk in range(TOP_K):
        rank_k = jnp.sum(jnp.where(lane == idxs[kk], excl, 0.0), axis=1, keepdims=True)
        idx_out = jnp.where(lane == kk, idxs[kk], idx_out)
        gate_out = jnp.where(lane == kk, exps[kk] / denom, gate_out)
        rank_out = jnp.where(lane == kk, rank_k, rank_out)
    idx_ref[...] = idx_out.astype(jnp.int32)
    gate_ref[...] = gate_out
    rank_ref[...] = rank_out.astype(jnp.int32)
    total = carry_s[...] + jnp.sum(onehot, axis=0, keepdims=True)
    carry_s[...] = total
    cnt_ref[...] = jnp.broadcast_to(total, cnt_ref.shape)


def _router(h2d, norm_w, w_r, b_r, tm=512):
    tok, d = h2d.shape
    return pl.pallas_call(
        _router_kernel,
        grid=(tok // tm,),
        in_specs=[
            pl.BlockSpec((tm, d), lambda i: (i, 0)),
            pl.BlockSpec((1, d), lambda i: (0, 0)),
            pl.BlockSpec((d, LANES), lambda i: (0, 0)),
            pl.BlockSpec((1, LANES), lambda i: (0, 0)),
        ],
        out_specs=[
            pl.BlockSpec((tm, d // 2), lambda i: (i, 0)),
            pl.BlockSpec((tm, LANES), lambda i: (i, 0)),
            pl.BlockSpec((tm, LANES), lambda i: (i, 0)),
            pl.BlockSpec((tm, LANES), lambda i: (i, 0)),
            pl.BlockSpec((8, LANES), lambda i: (0, 0)),
        ],
        out_shape=[
            jax.ShapeDtypeStruct((tok, d // 2), jnp.uint32),
            jax.ShapeDtypeStruct((tok, LANES), jnp.int32),
            jax.ShapeDtypeStruct((tok, LANES), F32),
            jax.ShapeDtypeStruct((tok, LANES), jnp.int32),
            jax.ShapeDtypeStruct((8, LANES), F32),
        ],
        scratch_shapes=[pltpu.VMEM((1, LANES), F32)],
        compiler_params=_params(("arbitrary",)),
        name="router",
    )(h2d, norm_w, w_r, b_r)


def _dispatch_kernel(pos_ref, u_ref, xs_in, xs_hbm, sem, *, tm):
    del xs_in

    def row_copy(t, k):
        return pltpu.make_async_copy(u_ref.at[pl.ds(t, 1)],
                                     xs_hbm.at[pl.ds(pos_ref[t * TOP_K + k], 1)], sem)

    def issue(t, carry):
        for k in range(TOP_K):
            row_copy(t, k).start(priority=k % 2)
        return carry

    def drain(t, carry):
        for k in range(TOP_K):
            row_copy(t, k).wait()
        return carry

    lax.fori_loop(0, tm, issue, 0)
    lax.fori_loop(0, tm, drain, 0)


def _dispatch(pos_flat, u2d, xs_init, tm=256):
    tok, d = u2d.shape
    kern = functools.partial(_dispatch_kernel, tm=tm)
    return pl.pallas_call(
        kern,
        grid=(tok // tm,),
        in_specs=[
            pl.BlockSpec((tm * TOP_K,), lambda i: (i,), memory_space=pltpu.SMEM),
            pl.BlockSpec((tm, d), lambda i: (i, 0)),
            pl.BlockSpec(memory_space=pl.ANY),
        ],
        out_specs=pl.BlockSpec(memory_space=pl.ANY),
        out_shape=jax.ShapeDtypeStruct(xs_init.shape, xs_init.dtype),
        scratch_shapes=[pltpu.SemaphoreType.DMA(())],
        input_output_aliases={2: 0},
        compiler_params=_params(("arbitrary",)),
        name="dispatch",
    )(pos_flat, u2d, xs_init)


def _moe_kernel(bexp_ref, nval_ref, x_ref, w1g_ref, w1l_ref, b1g_ref, b1l_ref, w2_ref, b2_ref,
                o_ref, acc_s, *, sub):
    i = pl.program_id(0)
    j = pl.program_id(1)
    nf = pl.num_programs(1)
    nval = nval_ref[i]
    bm, d = acc_s.shape

    @pl.when((nval > 0) & (j == 0))
    def _():
        acc_s[...] = jnp.broadcast_to(b2_ref[...], acc_s.shape)

    def process(subs):
        wg = w1g_ref[...].astype(BF16)
        wl = w1l_ref[...].astype(BF16)
        w2 = w2_ref[...].astype(BF16)
        rows = [pl.ds(s * sub, sub) for s in subs]
        xs = []
        for r in rows:
            lo, hi = _unpack_bf16_pairs(x_ref[r, :])
            xs.append(jnp.concatenate([lo.astype(BF16), hi.astype(BF16)], axis=1))
        hgs = [jnp.dot(x, wg, preferred_element_type=F32) + b1g_ref[...] for x in xs]
        hls = [jnp.dot(x, wl, preferred_element_type=F32) + b1l_ref[...] for x in xs]
        acts = []
        for hg, hl in zip(hgs, hls):
            glu = jnp.minimum(hg, SWIGLU_LIMIT)
            lin = jnp.clip(hl, -SWIGLU_LIMIT, SWIGLU_LIMIT)
            acts.append((glu * _sigmoid(SWIGLU_ALPHA * glu) * (lin + 1.0)).astype(BF16))
        for r, act in zip(rows, acts):
            acc_s[r, :] += jnp.dot(act, w2, preferred_element_type=F32)

    for pair in range(bm // (2 * sub)):
        first, second = 2 * pair, 2 * pair + 1

        @pl.when(nval > second * sub)
        def _(first=first, second=second):
            process([first, second])

        @pl.when((nval > first * sub) & (nval <= second * sub))
        def _(first=first):
            process([first])

    @pl.when((nval > 0) & (j == nf - 1))
    def _():
        o_ref[...] = _pack_bf16_pairs(acc_s[...])

    @pl.when((nval == 0) & (j == 0))
    def _():
        o_ref[...] = jnp.zeros_like(o_ref)


def _moe(bexp, nval, xs, w1, b1, w2, b2, *, bm, tf, sub):
    rows, dh = xs.shape
    d = 2 * dh
    ff = w2.shape[1]
    nb = rows // bm
    nf = ff // tf

    def jj(i, j, nv):
        return jnp.where(nv[i] > 0, j, nf - 1)

    grid_spec = pltpu.PrefetchScalarGridSpec(
        num_scalar_prefetch=2,
        grid=(nb, nf),
        in_specs=[
            pl.BlockSpec((bm, dh), lambda i, j, be, nv: (jnp.where(nv[i] > 0, i, 0), 0)),
            pl.BlockSpec((None, d, tf), lambda i, j, be, nv: (be[i], 0, jj(i, j, nv))),
            pl.BlockSpec((None, d, tf), lambda i, j, be, nv: (be[i], 0, nf + jj(i, j, nv))),
            pl.BlockSpec((None, 1, tf), lambda i, j, be, nv: (be[i], 0, jj(i, j, nv))),
            pl.BlockSpec((None, 1, tf), lambda i, j, be, nv: (be[i], 0, nf + jj(i, j, nv))),
            pl.BlockSpec((None, tf, d), lambda i, j, be, nv: (be[i], jj(i, j, nv), 0)),
            pl.BlockSpec((None, 1, d), lambda i, j, be, nv: (be[i], 0, 0)),
        ],
        out_specs=pl.BlockSpec((bm, dh), lambda i, j, be, nv: (i, 0)),
        scratch_shapes=[
            pltpu.VMEM((bm, d), F32),
        ],
    )
    return pl.pallas_call(
        functools.partial(_moe_kernel, sub=sub),
        grid_spec=grid_spec,
        out_shape=jax.ShapeDtypeStruct((rows, dh), jnp.uint32),
        compiler_params=_params(("arbitrary", "arbitrary"), MOE_VMEM_LIMIT),
        name="moe",
    )(bexp, nval, xs, w1, w1, b1, b1, w2, b2)


def _combine_kernel(pos_ref, gate_ref, h_ref, ys_hbm, p_ref, pn_ref, wg_ref, wp_ref, fn_ref, o_ref,
                    gbuf, sem, *, tm):
    def row_copy(t, k):
        return pltpu.make_async_copy(ys_hbm.at[pl.ds(pos_ref[t * TOP_K + k], 1)],
                                     gbuf.at[k, pl.ds(t, 1)], sem)

    def issue(t, carry):
        for k in range(TOP_K):
            row_copy(t, k).start(priority=k % 2)
        return carry

    def drain(t, carry):
        for k in range(TOP_K):
            row_copy(t, k).wait()
        return carry

    lax.fori_loop(0, tm, issue, 0)
    lax.fori_loop(0, tm, drain, 0)

    gate = gate_ref[...]
    h = h_ref[...]
    dh = h.shape[1] // 2
    m_lo = jnp.zeros((tm, dh), F32)
    m_hi = jnp.zeros((tm, dh), F32)
    for k in range(TOP_K):
        lo, hi = _unpack_bf16_pairs(gbuf[k])
        m_lo = m_lo + gate[:, k:k + 1] * lo
        m_hi = m_hi + gate[:, k:k + 1] * hi
    h = h + jnp.concatenate([m_lo, m_hi], axis=1)
    u = _rms(h, pn_ref[...]).astype(BF16)
    gv = _sigmoid(jnp.dot(u, wg_ref[...], preferred_element_type=F32))
    pp = jnp.dot(p_ref[...].astype(BF16), wp_ref[...], preferred_element_type=F32)
    h = h + pp * gv
    o_ref[...] = _rms(h, fn_ref[...])


def _combine(pos_flat, gate, h2d, ys, p2d, ple_norm, w_gate, w_proj, final_norm, tm=256):
    tok, d = h2d.shape
    pd = p2d.shape[1]
    kern = functools.partial(_combine_kernel, tm=tm)
    return pl.pallas_call(
        kern,
        grid=(tok // tm,),
        in_specs=[
            pl.BlockSpec((tm * TOP_K,), lambda i: (i,), memory_space=pltpu.SMEM),
            pl.BlockSpec((tm, LANES), lambda i: (i, 0)),
            pl.BlockSpec((tm, d), lambda i: (i, 0)),
            pl.BlockSpec(memory_space=pl.ANY),
            pl.BlockSpec((tm, pd), lambda i: (i, 0)),
            pl.BlockSpec((1, d), lambda i: (0, 0)),
            pl.BlockSpec((d, d), lambda i: (0, 0)),
            pl.BlockSpec((pd, d), lambda i: (0, 0)),
            pl.BlockSpec((1, d), lambda i: (0, 0)),
        ],
        out_specs=pl.BlockSpec((tm, d), lambda i: (i, 0)),
        out_shape=jax.ShapeDtypeStruct((tok, d), F32),
        scratch_shapes=[pltpu.VMEM((TOP_K, tm, d // 2), jnp.uint32), pltpu.SemaphoreType.DMA(())],
        compiler_params=_params(("arbitrary",)),
        name="combine",
    )(pos_flat, gate, h2d, ys, p2d, ple_norm, w_gate, w_proj, final_norm)


def _pad_cols(v, off):
    return jnp.pad(v.astype(F32), (off, LANES - off - v.shape[0]))


def _layer(h3, p3, mix_norm, w_in, gdn_conv_w, gdn_a_log, gdn_dt_bias, gdn_norm_w, ssm_conv_w,
           ssm_conv_b, ssm_a_log, ssm_dt_bias, ssm_d, ssm_norm_w, w_out, ffn_norm, w_router, b_router,
           w_mlp1, b_mlp1, w_mlp2, b_mlp2, ple_norm, w_ple_gate, w_ple_proj, final_norm, *, moe_bm, moe_tf, moe_sub):
    bsz, seq, d = h3.shape
    tok = bsz * seq
    n_vh = gdn_a_log.shape[0]
    n_qk = n_vh // 2
    key_dim = n_qk * GDN_HEAD_DIM
    val_dim = n_vh * GDN_HEAD_DIM
    gdn_conv_ch = 2 * key_dim + val_dim
    ssm_heads = ssm_a_log.shape[0]
    inner = ssm_heads * SSM_HEAD_DIM
    ssm_conv_ch = inner + 2 * SSM_GROUPS * SSM_STATE

    o0 = 0
    o_qkv, o0 = o0, o0 + gdn_conv_ch
    o_z, o0 = o0, o0 + val_dim
    o_b, o0 = o0, o0 + n_vh
    o_a, o0 = o0, o0 + n_vh
    o_xbc, o0 = o0, o0 + ssm_conv_ch
    o_sz, o0 = o0, o0 + inner
    o_dt, o0 = o0, o0 + ssm_heads
    w_main = jnp.concatenate([w_in[:, o_qkv:o_b], w_in[:, o_xbc:o_dt]], axis=1).astype(BF16)
    w_small = jnp.concatenate([w_in[:, o_b:o_xbc], w_in[:, o_dt:o0]], axis=1)
    n_small = w_small.shape[1]
    w_small = jnp.pad(w_small, ((0, 0), (0, LANES - n_small))).astype(BF16)
    m_qkv, m_z = 0, gdn_conv_ch
    m_xbc = gdn_conv_ch + val_dim
    m_sz = m_xbc + ssm_conv_ch
    dt_col0 = 2 * n_vh

    x2 = h3.reshape(tok, d)
    proj, small = _in_proj(x2, mix_norm.reshape(1, d), w_main, w_small)
    proj3 = proj.reshape(bsz, seq, proj.shape[1])
    small3 = small.reshape(bsz, seq, LANES)

    zrow = jnp.zeros((LANES,), F32)
    gdn_prm = jnp.stack([_pad_cols(gdn_a_log, n_vh), _pad_cols(gdn_dt_bias, n_vh)] + [zrow] * 6)
    o_gdn = _gdn(proj3, small3, gdn_conv_w, gdn_prm, gdn_norm_w.reshape(1, GDN_HEAD_DIM),
                 qkv_off=m_qkv, z_off=m_z, n_qk=n_qk, out_cols=val_dim)

    ssm_prm = jnp.stack([_pad_cols(ssm_a_log, dt_col0), _pad_cols(ssm_dt_bias, dt_col0)] + [zrow] * 6)
    d_rep = jnp.repeat(ssm_d.astype(F32), SSM_HEAD_DIM).reshape(1, inner)
    o_ssm = _ssd(proj3, small3, ssm_conv_w, ssm_conv_b.reshape(1, ssm_conv_ch), ssm_prm, d_rep,
                 ssm_norm_w.reshape(1, inner), xbc_off=m_xbc, z_off=m_sz, inner=inner, dt_col0=dt_col0)

    w_out_b = w_out.astype(BF16)
    h1 = _out_proj(o_gdn.reshape(tok, val_dim), o_ssm.reshape(tok, inner),
                   w_out_b[:val_dim], w_out_b[val_dim:], x2)

    n_exp = w_router.shape[1]
    w_r = jnp.pad(w_router, ((0, 0), (0, LANES - n_exp)))
    b_r = jnp.pad(b_router.astype(F32), (0, LANES - n_exp), constant_values=NEG_BIG).reshape(1, LANES)
    u2, idx_l, gate_l, rank_l, cnt = _router(h1, ffn_norm.reshape(1, d), w_r, b_r)

    counts = cnt[0, :n_exp].astype(jnp.int32)
    padded = ((counts + moe_bm - 1) // moe_bm) * moe_bm
    pad_end = jnp.cumsum(padded)
    pad_start = pad_end - padded
    top_idx = idx_l[:, :TOP_K]
    pos = (pad_start[top_idx] + rank_l[:, :TOP_K]).astype(jnp.int32).reshape(-1)
    n_blocks = (tok * TOP_K) // moe_bm + n_exp
    block_start = jnp.arange(n_blocks, dtype=jnp.int32) * moe_bm
    bexp = jnp.minimum(jnp.sum(block_start[:, None] >= pad_end[None, :], axis=1), n_exp - 1).astype(jnp.int32)
    nval = jnp.clip(pad_start[bexp] + counts[bexp] - block_start, 0, moe_bm)
    nval = jnp.where(block_start < pad_end[-1], nval, 0).astype(jnp.int32)

    xs = _dispatch(pos, u2, jnp.zeros((n_blocks * moe_bm, d // 2), jnp.uint32))
    ff = w_mlp2.shape[1]
    ys = _moe(bexp, nval, xs, w_mlp1, b_mlp1.reshape(n_exp, 1, 2 * ff), w_mlp2,
              b_mlp2.reshape(n_exp, 1, d), bm=moe_bm, tf=moe_tf, sub=moe_sub)

    out = _combine(pos, gate_l, h1, ys, p3.reshape(tok, p3.shape[-1]), ple_norm.reshape(1, d),
                   w_ple_gate.astype(BF16), w_ple_proj.astype(BF16), final_norm.reshape(1, d))
    return out.reshape(bsz, seq, d)


def kernel(x, p, mix_norm, w_in, gdn_conv_w, gdn_a_log, gdn_dt_bias, gdn_norm_w, ssm_conv_w, ssm_conv_b,
           ssm_a_log, ssm_dt_bias, ssm_d, ssm_norm_w, w_out, ffn_norm, w_router, b_router, w_mlp1, b_mlp1,
           w_mlp2, b_mlp2, ple_norm, w_ple_gate, w_ple_proj, final_norm):
    assert mix_norm.shape[0] == 1, "single-layer trunk"
    return _layer(x, p[0], mix_norm[0], w_in[0], gdn_conv_w[0], gdn_a_log[0], gdn_dt_bias[0], gdn_norm_w[0],
                  ssm_conv_w[0], ssm_conv_b[0], ssm_a_log[0], ssm_dt_bias[0], ssm_d[0], ssm_norm_w[0],
                  w_out[0], ffn_norm[0], w_router[0], b_router[0], w_mlp1[0], b_mlp1[0], w_mlp2[0],
                  b_mlp2[0], ple_norm[0], w_ple_gate[0], w_ple_proj[0], final_norm, moe_bm=1024, moe_tf=512, moe_sub=256)
```

```python
import functools

import jax
import jax.numpy as jnp
from jax import lax
from jax.experimental import pallas as pl
from jax.experimental.pallas import tpu as pltpu

F32 = jnp.float32
BF16 = jnp.bfloat16
HIGHEST = lax.Precision.HIGHEST

NORM_EPS = 1e-6
CONV_WIDTH = 4
GDN_HEAD_DIM = 128
SSM_HEAD_DIM = 64
SSM_GROUPS = 4
SSM_STATE = 128
N_EXPERTS = 32
TOP_K = 4
SWIGLU_ALPHA = 1.702
SWIGLU_LIMIT = 7.0

LANES = 128
CHUNK = 128
GDN_CHUNKS_PER_TRIP = 8
VMEM_LIMIT = 56 * 1024 * 1024
MOE_VMEM_LIMIT = 60 * 1024 * 1024

NEG_BIG = -1e30


def _params(sem, vmem=VMEM_LIMIT):
    return pltpu.CompilerParams(dimension_semantics=sem, vmem_limit_bytes=vmem)


def _rms(x, w):
    ms = jnp.mean(x * x, axis=-1, keepdims=True)
    return x * lax.rsqrt(ms + NORM_EPS) * w


def _sigmoid(x):
    return 0.5 * jnp.tanh(0.5 * x) + 0.5


def _softplus(x):
    return jnp.maximum(x, 0.0) + jnp.log(1.0 + jnp.exp(-jnp.abs(x)))


def _mm(a, b):
    return jnp.dot(a.astype(BF16), b.astype(BF16), preferred_element_type=F32)


def _mm_nt(a, b):
    return lax.dot_general(a.astype(BF16), b.astype(BF16), (((1,), (1,)), ((), ())),
                           preferred_element_type=F32)


def _mm_tn(a, b):
    return lax.dot_general(a.astype(BF16), b.astype(BF16), (((0,), (0,)), ((), ())),
                           preferred_element_type=F32)


def _mm_exact(a, b):
    return jnp.dot(a, b, precision=HIGHEST, preferred_element_type=F32)


def _pack_bf16_pairs(x):
    n = x.shape[1] // 2
    xb = x.astype(BF16).astype(F32)
    lo = lax.bitcast_convert_type(xb[:, :n], jnp.uint32)
    hi = lax.bitcast_convert_type(xb[:, n:], jnp.uint32)
    return (lo >> 16) | (hi & jnp.uint32(0xFFFF0000))


def _unpack_bf16_pairs(w):
    lo = lax.bitcast_convert_type(w << 16, F32)
    hi = lax.bitcast_convert_type(w & jnp.uint32(0xFFFF0000), F32)
    return lo, hi


ROW_TILE = 8


def _store_row_tiles(ref, packed):
    m = packed.shape[0]
    for s in range(ROW_TILE):
        ref[pl.ds(s, m, stride=ROW_TILE), :] = packed[:, s * LANES:(s + 1) * LANES]


def _load_row_tiles(ref, row0, m):
    return [ref[pl.ds(row0 * ROW_TILE + s, m, stride=ROW_TILE), :] for s in range(ROW_TILE)]


def _unpack_row_tiles(pieces):
    halves = [_unpack_bf16_pairs(p) for p in pieces]
    return jnp.concatenate([lo.astype(BF16) for lo, _ in halves] + [hi.astype(BF16) for _, hi in halves],
                           axis=1)


def _silu(x):
    h = 0.5 * x
    return h * jnp.tanh(h) + h


CONV_HALO = 16


def _causal_conv_silu(x_ref, w, bias, out_ref, chunked):
    seq = x_ref.shape[0]

    def conv(xv, masked):
        acc = xv * w[CONV_WIDTH - 1:CONV_WIDTH, :]
        for k in range(CONV_WIDTH - 1):
            shift = CONV_WIDTH - 1 - k
            xs = pltpu.roll(xv, shift, axis=0)
            if masked:
                row = lax.broadcasted_iota(jnp.int32, xv.shape, 0)
                xs = jnp.where(row >= shift, xs, 0.0)
            acc = acc + xs * w[k:k + 1, :]
        if bias is not None:
            acc = acc + bias
        return _silu(acc)

    if not chunked:
        out_ref[...] = conv(x_ref[...].astype(F32), False).astype(out_ref.dtype)
        out_ref[0:8, :] = conv(x_ref[0:CONV_HALO, :].astype(F32), True)[0:8, :].astype(out_ref.dtype)
        return

    out_ref[0:CHUNK, :] = conv(x_ref[0:CHUNK, :].astype(F32), True).astype(out_ref.dtype)

    def body(ci, carry):
        start = pl.multiple_of(ci * CHUNK, CHUNK)
        win = x_ref[pl.ds(pl.multiple_of(start - CONV_HALO, CONV_HALO), CHUNK + CONV_HALO), :]
        y = conv(win.astype(F32), False)
        out_ref[pl.ds(start, CHUNK), :] = y[CONV_HALO:, :].astype(out_ref.dtype)
        return carry

    lax.fori_loop(1, seq // CHUNK, body, 0)


def _bf16_pieces(x):
    hi = x.astype(BF16)
    r1 = x - hi.astype(F32)
    mid = r1.astype(BF16)
    lo = (r1 - mid.astype(F32)).astype(BF16)
    return hi, mid, lo


def _select_cols(vals, emat):
    out = None
    for piece in _bf16_pieces(vals):
        t = jnp.dot(piece, emat, preferred_element_type=F32)
        out = t if out is None else out + t
    return out


def _chunk_cumsum(vals, out_ref):
    seq = vals.shape[0]
    r = lax.broadcasted_iota(jnp.int32, (CHUNK, CHUNK), 0)
    c = lax.broadcasted_iota(jnp.int32, (CHUNK, CHUNK), 1)
    tri = (r >= c).astype(BF16)
    pieces = _bf16_pieces(vals)
    for i in range(seq // CHUNK):
        rows = slice(i * CHUNK, (i + 1) * CHUNK)
        acc = None
        for piece in pieces:
            t = jnp.dot(tri, piece[rows, :], preferred_element_type=F32)
            acc = t if acc is None else acc + t
        out_ref[rows, :] = acc


def _in_proj_kernel(x_ref, g_ref, w_ref, ws_ref, o_ref, os_ref, u_ref):
    @pl.when(pl.program_id(1) == 0)
    def _():
        u = _rms(x_ref[...], g_ref[...]).astype(BF16)
        u_ref[...] = u
        os_ref[...] = jnp.dot(u, ws_ref[...], preferred_element_type=F32)

    o_ref[...] = jnp.dot(u_ref[...], w_ref[...], preferred_element_type=F32).astype(o_ref.dtype)


def _in_proj(x2, norm_w, w_main, w_small, tm=512, tn=1024):
    tok, d = x2.shape
    n = w_main.shape[1]
    return pl.pallas_call(
        _in_proj_kernel,
        grid=(tok // tm, n // tn),
        in_specs=[
            pl.BlockSpec((tm, d), lambda i, j: (i, 0)),
            pl.BlockSpec((1, d), lambda i, j: (0, 0)),
            pl.BlockSpec((d, tn), lambda i, j: (0, j)),
            pl.BlockSpec((d, LANES), lambda i, j: (0, 0)),
        ],
        out_specs=[
            pl.BlockSpec((tm, tn), lambda i, j: (i, j)),
            pl.BlockSpec((tm, LANES), lambda i, j: (i, 0)),
        ],
        out_shape=[
            jax.ShapeDtypeStruct((tok, n), BF16),
            jax.ShapeDtypeStruct((tok, LANES), F32),
        ],
        scratch_shapes=[pltpu.VMEM((tm, d), BF16)],
        compiler_params=_params(("parallel", "arbitrary")),
        name="in_proj",
    )(x2, norm_w, w_main, w_small)


def _tri_masks(n):
    r = lax.broadcasted_iota(jnp.int32, (n, n), 0)
    c = lax.broadcasted_iota(jnp.int32, (n, n), 1)
    same = lambda s: (r // s) == (c // s)
    bd16 = same(16).astype(F32)
    levels = []
    s = 16
    while s < n:
        levels.append((same(2 * s) & jnp.logical_not(same(s))).astype(F32))
        s *= 2
    return bd16, levels


def _unit_lower_inverse(lmats, eye, bd16, levels):
    lds = [l * bd16 for l in lmats]
    ps = [eye - ld for ld in lds]
    ms = [_mm(ld, ld) for ld in lds]
    for it in range(3):
        ps = [p + _mm(p, m) for p, m in zip(ps, ms)]
        if it < 2:
            ms = [_mm(m, m) for m in ms]
    for msk in levels:
        ts = [_mm(l * msk, p) for l, p in zip(lmats, ps)]
        ps = [p - _mm(p, t) for p, t in zip(ps, ts)]
    return ps


def _gdn_kernel(q_ref, k_ref, v_ref, z_ref, sm_ref, wq_ref, wk_ref, wv_ref, prm_ref, nw_ref,
                o_ref, gc_ref, gct_ref, cv_s, q_s, k_s, kb_s, rhs_s, qd_s, a_s, b_s, qp_s, oacc_s, *, n_vh):
    hq = pl.program_id(1)
    seq = q_ref.shape[0]
    dk = GDN_HEAD_DIM
    nchunk = seq // CHUNK

    _causal_conv_silu(q_ref, wq_ref[...], None, cv_s.at[:, 0:dk], chunked=False)
    q = cv_s[:, 0:dk]
    _causal_conv_silu(k_ref, wk_ref[...], None, cv_s.at[:, 0:dk], chunked=False)
    k = cv_s[:, 0:dk]
    _causal_conv_silu(v_ref, wv_ref[...], None, cv_s, chunked=False)
    v = cv_s[...]
    q = q * lax.rsqrt(jnp.sum(q * q, axis=-1, keepdims=True) + NORM_EPS) * (dk ** -0.5)
    k = k * lax.rsqrt(jnp.sum(k * k, axis=-1, keepdims=True) + NORM_EPS)
    q_s[...] = q.astype(BF16)
    k_s[...] = k.astype(BF16)

    sm = sm_ref[...]
    beta_all = _sigmoid(sm)
    g_all = -jnp.exp(prm_ref[0:1, :]) * _softplus(sm + prm_ref[1:2, :])
    r = lax.broadcasted_iota(jnp.int32, (LANES, LANES), 0)
    c = lax.broadcasted_iota(jnp.int32, (LANES, LANES), 1)
    sel_g = ((c < 2) & (r == n_vh + 2 * hq + c)).astype(BF16)
    sel_b = ((c < 2) & (r == 2 * hq + c)).astype(BF16)
    g_sel = _select_cols(g_all, sel_g)
    beta = _select_cols(beta_all, sel_b)
    _chunk_cumsum(g_sel, gc_ref)
    gc = gc_ref[...]
    gct_ref[...] = gc.T
    eg = jnp.exp(gc)
    for hv in range(2):
        bcol = beta[:, hv:hv + 1]
        ecol = eg[:, hv:hv + 1]
        kb = k * bcol
        kb_s[hv] = kb.astype(BF16)
        rhs_s[hv, :, 0:dk] = (v[:, hv * dk:(hv + 1) * dk] * bcol).astype(BF16)
        rhs_s[hv, :, dk:2 * dk] = (kb * ecol).astype(BF16)
        qd_s[hv] = (q * ecol).astype(BF16)

    rr = lax.broadcasted_iota(jnp.int32, (CHUNK, CHUNK), 0)
    cc = lax.broadcasted_iota(jnp.int32, (CHUNK, CHUNK), 1)
    causal = rr >= cc
    strict = rr > cc
    eye = (rr == cc).astype(F32)
    bd16, levels = _tri_masks(CHUNK)

    unroll = GDN_CHUNKS_PER_TRIP

    def terms_body(ci, carry):
        sls = [pl.ds(pl.multiple_of((ci * unroll + s) * CHUNK, CHUNK), CHUNK) for s in range(unroll)]
        k_cs = [k_s[sl, :] for sl in sls]
        qks = [_mm_nt(q_s[sl, :], k_c) for sl, k_c in zip(sls, k_cs)]
        gcols = [gc_ref[sl, :] for sl in sls]
        grows = [gct_ref[:, sl] for sl in sls]
        chains = [(s, hv) for s in range(unroll) for hv in range(2)]
        gc_cs = [gcols[s][:, hv:hv + 1] for s, hv in chains]
        decays = [jnp.where(causal, jnp.exp(jnp.minimum(gc_c - grows[s][hv:hv + 1, :], 0.0)), 0.0)
                  for gc_c, (s, hv) in zip(gc_cs, chains)]
        lmats = [jnp.where(strict, _mm_nt(kb_s[hv, sls[s], :], k_cs[s]) * dec, 0.0)
                 for dec, (s, hv) in zip(decays, chains)]
        pinvs = _unit_lower_inverse(lmats, eye, bd16, levels)
        uws = [_mm(pinv, rhs_s[hv, sls[s], :]) for pinv, (s, hv) in zip(pinvs, chains)]
        kds = [k_cs[s].astype(F32) * jnp.exp(gc_c[CHUNK - 1:CHUNK, :] - gc_c)
               for gc_c, (s, hv) in zip(gc_cs, chains)]
        kdws = [_mm_tn(kd, uw) for kd, uw in zip(kds, uws)]
        aws = [_mm(qks[s] * dec, uw) for dec, uw, (s, hv) in zip(decays, uws, chains)]
        for kdw, aw, (s, hv) in zip(kdws, aws, chains):
            sl = sls[s]
            a_s[hv, sl, :] = (-kdw[:, dk:2 * dk]).astype(BF16)
            b_s[hv, sl, :] = kdw[:, 0:dk]
            qp_s[hv, sl, :] = (qd_s[hv, sl, :].astype(F32) - aw[:, dk:2 * dk]).astype(BF16)
            oacc_s[hv, sl, :] = aw[:, 0:dk]
        return carry

    lax.fori_loop(0, nchunk // unroll, terms_body, 0)

    def recur_body(ci, states):
        sl = pl.ds(pl.multiple_of(ci * CHUNK, CHUNK), CHUNK)
        g_last = gc_ref[pl.ds(ci * CHUNK + CHUNK - 1, 1), :]
        new_states = []
        for hv in range(2):
            sb = states[hv].astype(BF16)
            oacc_s[hv, sl, :] += _mm(qp_s[hv, sl, :], sb)
            new_states.append(states[hv] * jnp.exp(g_last[:, hv:hv + 1])
                              + _mm(a_s[hv, sl, :], sb) + b_s[hv, sl, :])
        return tuple(new_states)

    zero = jnp.zeros((dk, dk), F32)
    lax.fori_loop(0, nchunk, recur_body, (zero, zero))

    z = z_ref[...].astype(F32)
    nw = nw_ref[...]
    for hv in range(2):
        o = _rms(oacc_s[hv], nw)
        zz = z[:, hv * dk:(hv + 1) * dk]
        o_ref[:, hv * dk:(hv + 1) * dk] = (o * _silu(zz)).astype(o_ref.dtype)


def _gdn(proj3, small3, conv_w, prm, norm_w, *, qkv_off, z_off, n_qk, out_cols):
    bsz, seq, _ = proj3.shape
    assert seq % (CHUNK * GDN_CHUNKS_PER_TRIP) == 0
    dk = GDN_HEAD_DIM
    qb = qkv_off // dk
    kb = qb + n_qk
    vb = (qkv_off + 2 * n_qk * dk) // (2 * dk)
    zb = z_off // (2 * dk)
    cq = 0
    ck = n_qk
    cv = (2 * n_qk * dk) // (2 * dk)
    return pl.pallas_call(
        functools.partial(_gdn_kernel, n_vh=2 * n_qk),
        grid=(bsz, n_qk),
        in_specs=[
            pl.BlockSpec((None, seq, dk), lambda b, h: (b, 0, qb + h)),
            pl.BlockSpec((None, seq, dk), lambda b, h: (b, 0, kb + h)),
            pl.BlockSpec((None, seq, 2 * dk), lambda b, h: (b, 0, vb + h)),
            pl.BlockSpec((None, seq, 2 * dk), lambda b, h: (b, 0, zb + h)),
            pl.BlockSpec((None, seq, LANES), lambda b, h: (b, 0, 0)),
            pl.BlockSpec((CONV_WIDTH, dk), lambda b, h: (0, cq + h)),
            pl.BlockSpec((CONV_WIDTH, dk), lambda b, h: (0, ck + h)),
            pl.BlockSpec((CONV_WIDTH, 2 * dk), lambda b, h: (0, cv + h)),
            pl.BlockSpec((8, LANES), lambda b, h: (0, 0)),
            pl.BlockSpec((1, dk), lambda b, h: (0, 0)),
        ],
        out_specs=pl.BlockSpec((None, seq, 2 * dk), lambda b, h: (b, 0, h)),
        out_shape=jax.ShapeDtypeStruct((bsz, seq, out_cols), BF16),
        scratch_shapes=[
            pltpu.VMEM((seq, LANES), F32),
            pltpu.VMEM((LANES, seq), F32),
            pltpu.VMEM((seq, 2 * dk), F32),
            pltpu.VMEM((seq, dk), BF16),
            pltpu.VMEM((seq, dk), BF16),
            pltpu.VMEM((2, seq, dk), BF16),
            pltpu.VMEM((2, seq, 2 * dk), BF16),
            pltpu.VMEM((2, seq, dk), BF16),
            pltpu.VMEM((2, seq, dk), BF16),
            pltpu.VMEM((2, seq, dk), F32),
            pltpu.VMEM((2, seq, dk), BF16),
            pltpu.VMEM((2, seq, dk), F32),
        ],
        compiler_params=_params(("parallel", "parallel")),
        name="gdn",
    )(proj3, proj3, proj3, proj3, small3, conv_w, conv_w, conv_w, prm, norm_w)


def _ssd_kernel(x_ref, b_ref, c_ref, z_ref, sm_ref, wx_ref, wb_ref, wc_ref, cbx_ref, cbb_ref, cbc_ref,
                prm_ref, drep_ref, nw_ref, o_ref,
                acs_ref, acst_ref, cv_s, xs_s, bt_s, c_s, st_s, *, dt_col0, heads_per_group):
    grp = pl.program_id(1)
    seq = x_ref.shape[0]
    hp = SSM_HEAD_DIM
    hg = heads_per_group
    npair = hg // 2
    gw = hg * hp
    nchunk = seq // CHUNK

    _causal_conv_silu(x_ref, wx_ref[...], cbx_ref[...], xs_s, chunked=True)
    _causal_conv_silu(b_ref, wb_ref[...], cbb_ref[...], cv_s, chunked=True)
    bt_s[...] = cv_s[...].T.astype(BF16)
    _causal_conv_silu(c_ref, wc_ref[...], cbc_ref[...], cv_s, chunked=True)
    c_s[...] = cv_s[...].astype(BF16)

    sm = sm_ref[...]
    dt_all = _softplus(sm + prm_ref[1:2, :])
    a_all = -jnp.exp(prm_ref[0:1, :]) * dt_all
    r = lax.broadcasted_iota(jnp.int32, (LANES, LANES), 0)
    c = lax.broadcasted_iota(jnp.int32, (LANES, LANES), 1)
    sel = ((c < hg) & (r == dt_col0 + hg * grp + c)).astype(BF16)
    a_g = _select_cols(a_all, sel)
    _chunk_cumsum(a_g, acs_ref)
    acst_ref[...] = acs_ref[...].T
    st_s[...] = jnp.zeros_like(st_s)

    def replicate(width):
        rr_ = lax.broadcasted_iota(jnp.int32, (LANES, hg * width), 0)
        cc_ = lax.broadcasted_iota(jnp.int32, (LANES, hg * width), 1)
        return rr_, cc_ // width

    r64, c64 = replicate(hp)
    rep_a = (r64 == c64).astype(BF16)
    rep_dt = (r64 == dt_col0 + hg * grp + c64).astype(BF16)
    r128, c128 = replicate(CHUNK)
    rep_full = (r128 == c128).astype(BF16)
    lane = lax.broadcasted_iota(jnp.int32, (CHUNK, gw), 1)
    first = (lane % (2 * hp)) < hp
    rr = lax.broadcasted_iota(jnp.int32, (CHUNK, CHUNK), 0)
    cc = lax.broadcasted_iota(jnp.int32, (CHUNK, CHUNK), 1)
    causal = rr >= cc
    dt_bias = prm_ref[1:2, :]

    def chunk_body(ci, carry):
        sl = pl.ds(pl.multiple_of(ci * CHUNK, CHUNK), CHUNK)
        bt_c = bt_s[:, sl]
        c_c = c_s[sl, :]
        cb = jnp.dot(c_c, bt_c, preferred_element_type=F32)
        acs_c = acs_ref[sl, :]
        arow = acst_ref[:, sl]
        acrep = _select_cols(acs_c, rep_a)
        acf = _select_cols(acs_c, rep_full)
        xs_c = xs_s[sl, :]
        xdt = xs_c * _select_cols(_softplus(sm_ref[sl, :] + dt_bias), rep_dt)
        eac = jnp.exp(acrep)
        e_last = eac[CHUNK - 1:CHUNK, :]
        xdec = (xdt * jnp.exp(acrep[CHUNK - 1:CHUNK, :] - acrep)).astype(BF16)
        xlo = jnp.where(first, xdt, 0.0).astype(BF16)
        xhi = jnp.where(first, 0.0, xdt).astype(BF16)
        lms = [cb * jnp.where(causal, jnp.exp(jnp.minimum(
            acf[:, j * CHUNK:(j + 1) * CHUNK] - arow[j:j + 1, :], 0.0)), 0.0) for j in range(hg)]
        pairs = range(npair)
        cols = [slice(p * 2 * hp, (p + 1) * 2 * hp) for p in pairs]
        prevs = [st_s[p] for p in pairs]
        y_offs = [_mm(c_c, prev) for prev in prevs]
        y_diags = [jnp.dot(jnp.concatenate([lms[2 * p], lms[2 * p + 1]], axis=1).astype(BF16),
                           jnp.concatenate([xlo[:, cols[p]], xhi[:, cols[p]]], axis=0),
                           preferred_element_type=F32) for p in pairs]
        st_news = [jnp.dot(bt_c, xdec[:, cols[p]], preferred_element_type=F32) for p in pairs]
        for p in pairs:
            st_s[p] = prevs[p] * e_last[:, cols[p]] + st_news[p]
        y = jnp.concatenate([y_diags[p] + y_offs[p] * eac[:, cols[p]] for p in pairs], axis=1)
        y = (y + drep_ref[...] * xs_c) * _silu(z_ref[sl, :].astype(F32))
        o_ref[sl, :] = _rms(y, nw_ref[...]).astype(o_ref.dtype)
        return carry

    lax.fori_loop(0, nchunk, chunk_body, 0)


def _ssd(proj3, small3, conv_w, conv_b, prm, d_rep, norm_w, *, xbc_off, z_off, inner, dt_col0):
    bsz, seq, _ = proj3.shape
    gw = inner // SSM_GROUPS
    hg = gw // SSM_HEAD_DIM
    ns = SSM_STATE
    xb = xbc_off // gw
    bb = (xbc_off + inner) // ns
    cb = bb + SSM_GROUPS
    zb = z_off // gw
    wbb = inner // ns
    wcb = wbb + SSM_GROUPS
    kern = functools.partial(_ssd_kernel, dt_col0=dt_col0, heads_per_group=hg)
    return pl.pallas_call(
        kern,
        grid=(bsz, SSM_GROUPS),
        in_specs=[
            pl.BlockSpec((None, seq, gw), lambda b, g: (b, 0, xb + g)),
            pl.BlockSpec((None, seq, ns), lambda b, g: (b, 0, bb + g)),
            pl.BlockSpec((None, seq, ns), lambda b, g: (b, 0, cb + g)),
            pl.BlockSpec((None, seq, gw), lambda b, g: (b, 0, zb + g)),
            pl.BlockSpec((None, seq, LANES), lambda b, g: (b, 0, 0)),
            pl.BlockSpec((CONV_WIDTH, gw), lambda b, g: (0, g)),
            pl.BlockSpec((CONV_WIDTH, ns), lambda b, g: (0, wbb + g)),
            pl.BlockSpec((CONV_WIDTH, ns), lambda b, g: (0, wcb + g)),
            pl.BlockSpec((1, gw), lambda b, g: (0, g)),
            pl.BlockSpec((1, ns), lambda b, g: (0, wbb + g)),
            pl.BlockSpec((1, ns), lambda b, g: (0, wcb + g)),
            pl.BlockSpec((8, LANES), lambda b, g: (0, 0)),
            pl.BlockSpec((1, gw), lambda b, g: (0, g)),
            pl.BlockSpec((1, gw), lambda b, g: (0, g)),
        ],
        out_specs=pl.BlockSpec((None, seq, gw), lambda b, g: (b, 0, g)),
        out_shape=jax.ShapeDtypeStruct((bsz, seq, inner), BF16),
        scratch_shapes=[
            pltpu.VMEM((seq, LANES), F32),
            pltpu.VMEM((LANES, seq), F32),
            pltpu.VMEM((seq, ns), F32),
            pltpu.VMEM((seq, gw), F32),
            pltpu.VMEM((ns, seq), BF16),
            pltpu.VMEM((seq, ns), BF16),
            pltpu.VMEM((hg // 2, ns, 2 * SSM_HEAD_DIM), F32),
        ],
        compiler_params=_params(("parallel", "parallel")),
        name="ssd",
    )(proj3, proj3, proj3, proj3, small3, conv_w, conv_w, conv_w, conv_b, conv_b, conv_b,
      prm, d_rep, norm_w)


def _out_proj_kernel(a_ref, b_ref, wa_ref, wb_ref, x_ref, o_ref):
    acc = jnp.dot(a_ref[...], wa_ref[...], preferred_element_type=F32)
    acc = acc + jnp.dot(b_ref[...], wb_ref[...], preferred_element_type=F32)
    o_ref[...] = x_ref[...] + acc


def _out_proj(o_a, o_b, w_a, w_b, x2, tm=512, tn=1024):
    tok, d = x2.shape
    ka = o_a.shape[1]
    kb = o_b.shape[1]
    return pl.pallas_call(
        _out_proj_kernel,
        grid=(tok // tm, d // tn),
        in_specs=[
            pl.BlockSpec((tm, ka), lambda i, j: (i, 0)),
            pl.BlockSpec((tm, kb), lambda i, j: (i, 0)),
            pl.BlockSpec((ka, tn), lambda i, j: (0, j)),
            pl.BlockSpec((kb, tn), lambda i, j: (0, j)),
            pl.BlockSpec((tm, tn), lambda i, j: (i, j)),
        ],
        out_specs=pl.BlockSpec((tm, tn), lambda i, j: (i, j)),
        out_shape=jax.ShapeDtypeStruct((tok, d), F32),
        compiler_params=_params(("parallel", "parallel")),
        name="out_proj",
    )(o_a, o_b, w_a, w_b, x2)


def _router_kernel(h_ref, nw_ref, wr_ref, br_ref, u_ref, idx_ref, gate_ref, rank_ref, cnt_ref, carry_s):
    step = pl.program_id(0)
    tm = h_ref.shape[0]

    @pl.when(step == 0)
    def _():
        carry_s[...] = jnp.zeros_like(carry_s)

    u = _rms(h_ref[...], nw_ref[...])
    _store_row_tiles(u_ref, _pack_bf16_pairs(u))
    logits = _mm_exact(u, wr_ref[...]) + br_ref[...]
    lane = lax.broadcasted_iota(jnp.int32, (tm, LANES), 1).astype(F32)
    work = logits
    vals, idxs = [], []
    for _ in range(TOP_K):
        m = jnp.max(work, axis=1, keepdims=True)
        idx = jnp.min(jnp.where(work == m, lane, float(LANES)), axis=1, keepdims=True)
        vals.append(m)
        idxs.append(idx)
        work = jnp.where(lane == idx, -jnp.inf, work)
    onehot = (work == -jnp.inf).astype(F32)
    exps = [jnp.exp(v - vals[0]) for v in vals]
    denom = exps[0]
    for e in exps[1:]:
        denom = denom + e
    r = lax.broadcasted_iota(jnp.int32, (tm, tm), 0)
    c = lax.broadcasted_iota(jnp.int32, (tm, tm), 1)
    below = (r > c).astype(BF16)
    excl = jnp.dot(below, onehot.astype(BF16), preferred_element_type=F32) + carry_s[...]
    idx_out = jnp.zeros((tm, LANES), F32)
    gate_out = jnp.zeros((tm, LANES), F32)
    rank_out = jnp.zeros((tm, LANES), F32)
    for kk in range(TOP_K):
        rank_k = jnp.sum(jnp.where(lane == idxs[kk], excl, 0.0), axis=1, keepdims=True)
        idx_out = jnp.where(lane == kk, idxs[kk], idx_out)
        gate_out = jnp.where(lane == kk, exps[kk] / denom, gate_out)
        rank_out = jnp.where(lane == kk, rank_k, rank_out)
    idx_ref[...] = idx_out.astype(jnp.int32)
    gate_ref[...] = gate_out
    rank_ref[...] = rank_out.astype(jnp.int32)
    total = carry_s[...] + jnp.sum(onehot, axis=0, keepdims=True)
    carry_s[...] = total
    cnt_ref[...] = jnp.broadcast_to(total, cnt_ref.shape)


def _router(h2d, norm_w, w_r, b_r, tm=512):
    tok, d = h2d.shape
    return pl.pallas_call(
        _router_kernel,
        grid=(tok // tm,),
        in_specs=[
            pl.BlockSpec((tm, d), lambda i: (i, 0)),
            pl.BlockSpec((1, d), lambda i: (0, 0)),
            pl.BlockSpec((d, LANES), lambda i: (0, 0)),
            pl.BlockSpec((1, LANES), lambda i: (0, 0)),
        ],
        out_specs=[
            pl.BlockSpec((tm * ROW_TILE, LANES), lambda i: (i, 0)),
            pl.BlockSpec((tm, LANES), lambda i: (i, 0)),
            pl.BlockSpec((tm, LANES), lambda i: (i, 0)),
            pl.BlockSpec((tm, LANES), lambda i: (i, 0)),
            pl.BlockSpec((8, LANES), lambda i: (0, 0)),
        ],
        out_shape=[
            jax.ShapeDtypeStruct((tok * ROW_TILE, LANES), jnp.uint32),
            jax.ShapeDtypeStruct((tok, LANES), jnp.int32),
            jax.ShapeDtypeStruct((tok, LANES), F32),
            jax.ShapeDtypeStruct((tok, LANES), jnp.int32),
            jax.ShapeDtypeStruct((8, LANES), F32),
        ],
        scratch_shapes=[pltpu.VMEM((1, LANES), F32)],
        compiler_params=_params(("arbitrary",)),
        name="router",
    )(h2d, norm_w, w_r, b_r)


def _dispatch_kernel(pos_ref, u_ref, xs_in, xs_hbm, sem, *, tm):
    del xs_in

    def row_copy(t, k):
        dst = pl.multiple_of(pos_ref[t * TOP_K + k], ROW_TILE)
        return pltpu.make_async_copy(u_ref.at[pl.ds(pl.multiple_of(t * ROW_TILE, ROW_TILE), ROW_TILE)],
                                     xs_hbm.at[pl.ds(dst, ROW_TILE)], sem)

    def issue(t, carry):
        for k in range(TOP_K):
            row_copy(t, k).start(priority=k % 2)
        return carry

    def drain(t, carry):
        for k in range(TOP_K):
            row_copy(t, k).wait()
        return carry

    lax.fori_loop(0, tm, issue, 0)
    lax.fori_loop(0, tm, drain, 0)


def _dispatch(pos_flat, u_tiles, xs_init, tm=256):
    tok = u_tiles.shape[0] // ROW_TILE
    kern = functools.partial(_dispatch_kernel, tm=tm)
    return pl.pallas_call(
        kern,
        grid=(tok // tm,),
        in_specs=[
            pl.BlockSpec((tm * TOP_K,), lambda i: (i,), memory_space=pltpu.SMEM),
            pl.BlockSpec((tm * ROW_TILE, LANES), lambda i: (i, 0)),
            pl.BlockSpec(memory_space=pl.ANY),
        ],
        out_specs=pl.BlockSpec(memory_space=pl.ANY),
        out_shape=jax.ShapeDtypeStruct(xs_init.shape, xs_init.dtype),
        scratch_shapes=[pltpu.SemaphoreType.DMA(())],
        input_output_aliases={2: 0},
        compiler_params=_params(("arbitrary",)),
        name="dispatch",
    )(pos_flat, u_tiles, xs_init)


def _moe_kernel(bexp_ref, nval_ref, x_ref, w1g_ref, w1l_ref, b1g_ref, b1l_ref, w2_ref, b2_ref,
                o_ref, acc_s, *, sub):
    i = pl.program_id(0)
    j = pl.program_id(1)
    nf = pl.num_programs(1)
    nval = nval_ref[i]
    bm, d = acc_s.shape

    @pl.when((nval > 0) & (j == 0))
    def _():
        acc_s[...] = jnp.broadcast_to(b2_ref[...], acc_s.shape)

    def process(subs):
        wg = w1g_ref[...].astype(BF16)
        wl = w1l_ref[...].astype(BF16)
        w2 = w2_ref[...].astype(BF16)
        rows = [pl.ds(s * sub, sub) for s in subs]
        xs = [_unpack_row_tiles(_load_row_tiles(x_ref, s * sub, sub)) for s in subs]
        hgs = [jnp.dot(x, wg, preferred_element_type=F32) + b1g_ref[...] for x in xs]
        hls = [jnp.dot(x, wl, preferred_element_type=F32) + b1l_ref[...] for x in xs]
        acts = []
        for hg, hl in zip(hgs, hls):
            glu = jnp.minimum(hg, SWIGLU_LIMIT)
            lin = jnp.clip(hl, -SWIGLU_LIMIT, SWIGLU_LIMIT)
            acts.append((glu * _sigmoid(SWIGLU_ALPHA * glu) * (lin + 1.0)).astype(BF16))
        for r, act in zip(rows, acts):
            acc_s[r, :] += jnp.dot(act, w2, preferred_element_type=F32)

    for pair in range(bm // (2 * sub)):
        first, second = 2 * pair, 2 * pair + 1

        @pl.when(nval > second * sub)
        def _(first=first, second=second):
            process([first, second])

        @pl.when((nval > first * sub) & (nval <= second * sub))
        def _(first=first):
            process([first])

    @pl.when((nval > 0) & (j == nf - 1))
    def _():
        _store_row_tiles(o_ref, _pack_bf16_pairs(acc_s[...]))

    @pl.when((nval == 0) & (j == 0))
    def _():
        o_ref[...] = jnp.zeros_like(o_ref)


def _moe(bexp, nval, xs, w1, b1, w2, b2, *, bm, tf, sub):
    rows = xs.shape[0] // ROW_TILE
    d = w1.shape[1]
    assert d == 2 * ROW_TILE * LANES, "a packed row must fill exactly one (8, 128) tile"
    ff = w2.shape[1]
    nb = rows // bm
    nf = ff // tf
    bt = bm * ROW_TILE

    def jj(i, j, nv):
        return jnp.where(nv[i] > 0, j, nf - 1)

    grid_spec = pltpu.PrefetchScalarGridSpec(
        num_scalar_prefetch=2,
        grid=(nb, nf),
        in_specs=[
            pl.BlockSpec((bt, LANES), lambda i, j, be, nv: (jnp.where(nv[i] > 0, i, 0), 0)),
            pl.BlockSpec((None, d, tf), lambda i, j, be, nv: (be[i], 0, jj(i, j, nv))),
            pl.BlockSpec((None, d, tf), lambda i, j, be, nv: (be[i], 0, nf + jj(i, j, nv))),
            pl.BlockSpec((None, 1, tf), lambda i, j, be, nv: (be[i], 0, jj(i, j, nv))),
            pl.BlockSpec((None, 1, tf), lambda i, j, be, nv: (be[i], 0, nf + jj(i, j, nv))),
            pl.BlockSpec((None, tf, d), lambda i, j, be, nv: (be[i], jj(i, j, nv), 0)),
            pl.BlockSpec((None, 1, d), lambda i, j, be, nv: (be[i], 0, 0)),
        ],
        out_specs=pl.BlockSpec((bt, LANES), lambda i, j, be, nv: (i, 0)),
        scratch_shapes=[
            pltpu.VMEM((bm, d), F32),
        ],
    )
    return pl.pallas_call(
        functools.partial(_moe_kernel, sub=sub),
        grid_spec=grid_spec,
        out_shape=jax.ShapeDtypeStruct((rows * ROW_TILE, LANES), jnp.uint32),
        compiler_params=_params(("arbitrary", "arbitrary"), MOE_VMEM_LIMIT),
        name="moe",
    )(bexp, nval, xs, w1, w1, b1, b1, w2, b2)


def _combine_kernel(pos_ref, gate_ref, h_ref, ys_hbm, p_ref, pn_ref, wg_ref, wp_ref, fn_ref, o_ref,
                    gbuf, sem, *, tm):
    def row_copy(t, k):
        src = pl.multiple_of(pos_ref[t * TOP_K + k], ROW_TILE)
        return pltpu.make_async_copy(
            ys_hbm.at[pl.ds(src, ROW_TILE)],
            gbuf.at[k, pl.ds(pl.multiple_of(t * ROW_TILE, ROW_TILE), ROW_TILE)], sem)

    def issue(t, carry):
        for k in range(TOP_K):
            row_copy(t, k).start(priority=k % 2)
        return carry

    def drain(t, carry):
        for k in range(TOP_K):
            row_copy(t, k).wait()
        return carry

    lax.fori_loop(0, tm, issue, 0)
    lax.fori_loop(0, tm, drain, 0)

    gate = gate_ref[...]
    h = h_ref[...]
    moe = None
    for k in range(TOP_K):
        halves = [_unpack_bf16_pairs(p) for p in _load_row_tiles(gbuf.at[k], 0, tm)]
        row = jnp.concatenate([lo for lo, _ in halves] + [hi for _, hi in halves], axis=1)
        term = gate[:, k:k + 1] * row
        moe = term if moe is None else moe + term
    h = h + moe
    u = _rms(h, pn_ref[...]).astype(BF16)
    gv = _sigmoid(jnp.dot(u, wg_ref[...], preferred_element_type=F32))
    pp = jnp.dot(p_ref[...].astype(BF16), wp_ref[...], preferred_element_type=F32)
    h = h + pp * gv
    o_ref[...] = _rms(h, fn_ref[...])


def _combine(pos_flat, gate, h2d, ys, p2d, ple_norm, w_gate, w_proj, final_norm, tm=256):
    tok, d = h2d.shape
    pd = p2d.shape[1]
    kern = functools.partial(_combine_kernel, tm=tm)
    return pl.pallas_call(
        kern,
        grid=(tok // tm,),
        in_specs=[
            pl.BlockSpec((tm * TOP_K,), lambda i: (i,), memory_space=pltpu.SMEM),
            pl.BlockSpec((tm, LANES), lambda i: (i, 0)),
            pl.BlockSpec((tm, d), lambda i: (i, 0)),
            pl.BlockSpec(memory_space=pl.ANY),
            pl.BlockSpec((tm, pd), lambda i: (i, 0)),
            pl.BlockSpec((1, d), lambda i: (0, 0)),
            pl.BlockSpec((d, d), lambda i: (0, 0)),
            pl.BlockSpec((pd, d), lambda i: (0, 0)),
            pl.BlockSpec((1, d), lambda i: (0, 0)),
        ],
        out_specs=pl.BlockSpec((tm, d), lambda i: (i, 0)),
        out_shape=jax.ShapeDtypeStruct((tok, d), F32),
        scratch_shapes=[pltpu.VMEM((TOP_K, tm * ROW_TILE, LANES), jnp.uint32),
                        pltpu.SemaphoreType.DMA(())],
        compiler_params=_params(("arbitrary",)),
        name="combine",
    )(pos_flat, gate, h2d, ys, p2d, ple_norm, w_gate, w_proj, final_norm)


def _pad_cols(v, off):
    return jnp.pad(v.astype(F32), (off, LANES - off - v.shape[0]))


def _layer(h3, p3, mix_norm, w_in, gdn_conv_w, gdn_a_log, gdn_dt_bias, gdn_norm_w, ssm_conv_w,
           ssm_conv_b, ssm_a_log, ssm_dt_bias, ssm_d, ssm_norm_w, w_out, ffn_norm, w_router, b_router,
           w_mlp1, b_mlp1, w_mlp2, b_mlp2, ple_norm, w_ple_gate, w_ple_proj, final_norm, *, moe_bm, moe_tf, moe_sub):
    bsz, seq, d = h3.shape
    tok = bsz * seq
    n_vh = gdn_a_log.shape[0]
    n_qk = n_vh // 2
    key_dim = n_qk * GDN_HEAD_DIM
    val_dim = n_vh * GDN_HEAD_DIM
    gdn_conv_ch = 2 * key_dim + val_dim
    ssm_heads = ssm_a_log.shape[0]
    inner = ssm_heads * SSM_HEAD_DIM
    ssm_conv_ch = inner + 2 * SSM_GROUPS * SSM_STATE

    o0 = 0
    o_qkv, o0 = o0, o0 + gdn_conv_ch
    o_z, o0 = o0, o0 + val_dim
    o_b, o0 = o0, o0 + n_vh
    o_a, o0 = o0, o0 + n_vh
    o_xbc, o0 = o0, o0 + ssm_conv_ch
    o_sz, o0 = o0, o0 + inner
    o_dt, o0 = o0, o0 + ssm_heads
    w_main = jnp.concatenate([w_in[:, o_qkv:o_b], w_in[:, o_xbc:o_dt]], axis=1).astype(BF16)
    w_small = jnp.concatenate([w_in[:, o_b:o_xbc], w_in[:, o_dt:o0]], axis=1)
    n_small = w_small.shape[1]
    w_small = jnp.pad(w_small, ((0, 0), (0, LANES - n_small))).astype(BF16)
    m_qkv, m_z = 0, gdn_conv_ch
    m_xbc = gdn_conv_ch + val_dim
    m_sz = m_xbc + ssm_conv_ch
    dt_col0 = 2 * n_vh

    x2 = h3.reshape(tok, d)
    proj, small = _in_proj(x2, mix_norm.reshape(1, d), w_main, w_small)
    proj3 = proj.reshape(bsz, seq, proj.shape[1])
    small3 = small.reshape(bsz, seq, LANES)

    zrow = jnp.zeros((LANES,), F32)
    gdn_prm = jnp.stack([_pad_cols(gdn_a_log, n_vh), _pad_cols(gdn_dt_bias, n_vh)] + [zrow] * 6)
    o_gdn = _gdn(proj3, small3, gdn_conv_w, gdn_prm, gdn_norm_w.reshape(1, GDN_HEAD_DIM),
                 qkv_off=m_qkv, z_off=m_z, n_qk=n_qk, out_cols=val_dim)

    ssm_prm = jnp.stack([_pad_cols(ssm_a_log, dt_col0), _pad_cols(ssm_dt_bias, dt_col0)] + [zrow] * 6)
    d_rep = jnp.repeat(ssm_d.astype(F32), SSM_HEAD_DIM).reshape(1, inner)
    o_ssm = _ssd(proj3, small3, ssm_conv_w, ssm_conv_b.reshape(1, ssm_conv_ch), ssm_prm, d_rep,
                 ssm_norm_w.reshape(1, inner), xbc_off=m_xbc, z_off=m_sz, inner=inner, dt_col0=dt_col0)

    w_out_b = w_out.astype(BF16)
    h1 = _out_proj(o_gdn.reshape(tok, val_dim), o_ssm.reshape(tok, inner),
                   w_out_b[:val_dim], w_out_b[val_dim:], x2)

    n_exp = w_router.shape[1]
    w_r = jnp.pad(w_router, ((0, 0), (0, LANES - n_exp)))
    b_r = jnp.pad(b_router.astype(F32), (0, LANES - n_exp), constant_values=NEG_BIG).reshape(1, LANES)
    u2, idx_l, gate_l, rank_l, cnt = _router(h1, ffn_norm.reshape(1, d), w_r, b_r)

    counts = cnt[0, :n_exp].astype(jnp.int32)
    padded = ((counts + moe_bm - 1) // moe_bm) * moe_bm
    pad_end = jnp.cumsum(padded)
    pad_start = pad_end - padded
    top_idx = idx_l[:, :TOP_K]
    pos = ((pad_start[top_idx] + rank_l[:, :TOP_K]) * ROW_TILE).astype(jnp.int32).reshape(-1)
    n_blocks = (tok * TOP_K) // moe_bm + n_exp
    block_start = jnp.arange(n_blocks, dtype=jnp.int32) * moe_bm
    bexp = jnp.minimum(jnp.sum(block_start[:, None] >= pad_end[None, :], axis=1), n_exp - 1).astype(jnp.int32)
    nval = jnp.clip(pad_start[bexp] + counts[bexp] - block_start, 0, moe_bm)
    nval = jnp.where(block_start < pad_end[-1], nval, 0).astype(jnp.int32)

    xs = _dispatch(pos, u2, jnp.zeros((n_blocks * moe_bm * ROW_TILE, LANES), jnp.uint32))
    ff = w_mlp2.shape[1]
    ys = _moe(bexp, nval, xs, w_mlp1, b_mlp1.reshape(n_exp, 1, 2 * ff), w_mlp2,
              b_mlp2.reshape(n_exp, 1, d), bm=moe_bm, tf=moe_tf, sub=moe_sub)

    out = _combine(pos, gate_l, h1, ys, p3.reshape(tok, p3.shape[-1]), ple_norm.reshape(1, d),
                   w_ple_gate.astype(BF16), w_ple_proj.astype(BF16), final_norm.reshape(1, d))
    return out.reshape(bsz, seq, d)


def kernel(x, p, mix_norm, w_in, gdn_conv_w, gdn_a_log, gdn_dt_bias, gdn_norm_w, ssm_conv_w, ssm_conv_b,
           ssm_a_log, ssm_dt_bias, ssm_d, ssm_norm_w, w_out, ffn_norm, w_router, b_router, w_mlp1, b_mlp1,
           w_mlp2, b_mlp2, ple_norm, w_ple_gate, w_ple_proj, final_norm):
    assert mix_norm.shape[0] == 1, "single-layer trunk"
    return _layer(x, p[0], mix_norm[0], w_in[0], gdn_conv_w[0], gdn_a_log[0], gdn_dt_bias[0], gdn_norm_w[0],
                  ssm_conv_w[0], ssm_conv_b[0], ssm_a_log[0], ssm_dt_bias[0], ssm_d[0], ssm_norm_w[0],
                  w_out[0], ffn_norm[0], w_router[0], b_router[0], w_mlp1[0], b_mlp1[0], w_mlp2[0],
                  b_mlp2[0], ple_norm[0], w_ple_gate[0], w_ple_proj[0], final_norm, moe_bm=1024, moe_tf=512, moe_sub=256)
```

```python
import functools

import jax
import jax.numpy as jnp
from jax import lax
from jax.experimental import pallas as pl
from jax.experimental.pallas import tpu as pltpu

F32 = jnp.float32
BF16 = jnp.bfloat16
HIGHEST = lax.Precision.HIGHEST

NORM_EPS = 1e-6
CONV_WIDTH = 4
GDN_HEAD_DIM = 128
SSM_HEAD_DIM = 64
SSM_GROUPS = 4
SSM_STATE = 128
N_EXPERTS = 32
TOP_K = 4
SWIGLU_ALPHA = 1.702
SWIGLU_LIMIT = 7.0

LANES = 128
CHUNK = 128
GDN_CHUNKS_PER_TRIP = 8
VMEM_LIMIT = 56 * 1024 * 1024
MOE_VMEM_LIMIT = 60 * 1024 * 1024

NEG_BIG = -1e30


def _params(sem, vmem=VMEM_LIMIT):
    return pltpu.CompilerParams(dimension_semantics=sem, vmem_limit_bytes=vmem)


def _rms(x, w):
    ms = jnp.mean(x * x, axis=-1, keepdims=True)
    return x * lax.rsqrt(ms + NORM_EPS) * w


def _sigmoid(x):
    return 0.5 * jnp.tanh(0.5 * x) + 0.5


def _softplus(x):
    return jnp.maximum(x, 0.0) + jnp.log(1.0 + jnp.exp(-jnp.abs(x)))


def _mm(a, b):
    return jnp.dot(a.astype(BF16), b.astype(BF16), preferred_element_type=F32)


def _mm_nt(a, b):
    return lax.dot_general(a.astype(BF16), b.astype(BF16), (((1,), (1,)), ((), ())),
                           preferred_element_type=F32)


def _mm_tn(a, b):
    return lax.dot_general(a.astype(BF16), b.astype(BF16), (((0,), (0,)), ((), ())),
                           preferred_element_type=F32)


def _mm_exact(a, b):
    return jnp.dot(a, b, precision=HIGHEST, preferred_element_type=F32)


def _pack_bf16_pairs(x):
    n = x.shape[1] // 2
    xb = x.astype(BF16).astype(F32)
    lo = lax.bitcast_convert_type(xb[:, :n], jnp.uint32)
    hi = lax.bitcast_convert_type(xb[:, n:], jnp.uint32)
    return (lo >> 16) | (hi & jnp.uint32(0xFFFF0000))


def _unpack_bf16_pairs(w):
    lo = lax.bitcast_convert_type(w << 16, F32)
    hi = lax.bitcast_convert_type(w & jnp.uint32(0xFFFF0000), F32)
    return lo, hi


ROW_TILE = 8


def _store_row_tiles(ref, packed):
    m = packed.shape[0]
    for s in range(ROW_TILE):
        ref[pl.ds(s, m, stride=ROW_TILE), :] = packed[:, s * LANES:(s + 1) * LANES]


def _load_row_tiles(ref, row0, m):
    return [ref[pl.ds(row0 * ROW_TILE + s, m, stride=ROW_TILE), :] for s in range(ROW_TILE)]


def _unpack_row_tiles(pieces):
    halves = [_unpack_bf16_pairs(p) for p in pieces]
    return jnp.concatenate([lo.astype(BF16) for lo, _ in halves] + [hi.astype(BF16) for _, hi in halves],
                           axis=1)


def _silu(x):
    h = 0.5 * x
    return h * jnp.tanh(h) + h


CONV_HALO = 16


def _causal_conv_silu(x_ref, w, bias, out_ref, chunked):
    seq = x_ref.shape[0]

    def conv(xv, masked):
        acc = xv * w[CONV_WIDTH - 1:CONV_WIDTH, :]
        for k in range(CONV_WIDTH - 1):
            shift = CONV_WIDTH - 1 - k
            xs = pltpu.roll(xv, shift, axis=0)
            if masked:
                row = lax.broadcasted_iota(jnp.int32, xv.shape, 0)
                xs = jnp.where(row >= shift, xs, 0.0)
            acc = acc + xs * w[k:k + 1, :]
        if bias is not None:
            acc = acc + bias
        return _silu(acc)

    if not chunked:
        out_ref[...] = conv(x_ref[...].astype(F32), False).astype(out_ref.dtype)
        out_ref[0:8, :] = conv(x_ref[0:CONV_HALO, :].astype(F32), True)[0:8, :].astype(out_ref.dtype)
        return

    out_ref[0:CHUNK, :] = conv(x_ref[0:CHUNK, :].astype(F32), True).astype(out_ref.dtype)

    def body(ci, carry):
        start = pl.multiple_of(ci * CHUNK, CHUNK)
        win = x_ref[pl.ds(pl.multiple_of(start - CONV_HALO, CONV_HALO), CHUNK + CONV_HALO), :]
        y = conv(win.astype(F32), False)
        out_ref[pl.ds(start, CHUNK), :] = y[CONV_HALO:, :].astype(out_ref.dtype)
        return carry

    lax.fori_loop(1, seq // CHUNK, body, 0)


def _bf16_pieces(x):
    hi = x.astype(BF16)
    r1 = x - hi.astype(F32)
    mid = r1.astype(BF16)
    lo = (r1 - mid.astype(F32)).astype(BF16)
    return hi, mid, lo


def _select_cols(vals, emat):
    out = None
    for piece in _bf16_pieces(vals):
        t = jnp.dot(piece, emat, preferred_element_type=F32)
        out = t if out is None else out + t
    return out


def _chunk_cumsum(vals, out_ref):
    seq = vals.shape[0]
    r = lax.broadcasted_iota(jnp.int32, (CHUNK, CHUNK), 0)
    c = lax.broadcasted_iota(jnp.int32, (CHUNK, CHUNK), 1)
    tri = (r >= c).astype(BF16)
    pieces = _bf16_pieces(vals)
    for i in range(seq // CHUNK):
        rows = slice(i * CHUNK, (i + 1) * CHUNK)
        acc = None
        for piece in pieces:
            t = jnp.dot(tri, piece[rows, :], preferred_element_type=F32)
            acc = t if acc is None else acc + t
        out_ref[rows, :] = acc


def _in_proj_kernel(x_ref, g_ref, w_ref, ws_ref, o_ref, os_ref, u_ref):
    @pl.when(pl.program_id(1) == 0)
    def _():
        u = _rms(x_ref[...], g_ref[...]).astype(BF16)
        u_ref[...] = u
        os_ref[...] = jnp.dot(u, ws_ref[...], preferred_element_type=F32)

    o_ref[...] = jnp.dot(u_ref[...], w_ref[...], preferred_element_type=F32).astype(o_ref.dtype)


def _in_proj(x2, norm_w, w_main, w_small, tm=1024, tn=1024):
    tok, d = x2.shape
    n = w_main.shape[1]
    return pl.pallas_call(
        _in_proj_kernel,
        grid=(tok // tm, n // tn),
        in_specs=[
            pl.BlockSpec((tm, d), lambda i, j: (i, 0)),
            pl.BlockSpec((1, d), lambda i, j: (0, 0)),
            pl.BlockSpec((d, tn), lambda i, j: (0, j)),
            pl.BlockSpec((d, LANES), lambda i, j: (0, 0)),
        ],
        out_specs=[
            pl.BlockSpec((tm, tn), lambda i, j: (i, j)),
            pl.BlockSpec((tm, LANES), lambda i, j: (i, 0)),
        ],
        out_shape=[
            jax.ShapeDtypeStruct((tok, n), BF16),
            jax.ShapeDtypeStruct((tok, LANES), F32),
        ],
        scratch_shapes=[pltpu.VMEM((tm, d), BF16)],
        compiler_params=_params(("parallel", "arbitrary")),
        name="in_proj",
    )(x2, norm_w, w_main, w_small)


def _tri_masks(n):
    r = lax.broadcasted_iota(jnp.int32, (n, n), 0)
    c = lax.broadcasted_iota(jnp.int32, (n, n), 1)
    same = lambda s: (r // s) == (c // s)
    bd16 = same(16).astype(F32)
    levels = []
    s = 16
    while s < n:
        levels.append((same(2 * s) & jnp.logical_not(same(s))).astype(F32))
        s *= 2
    return bd16, levels


def _unit_lower_inverse(lmats, eye, bd16, levels):
    lds = [l * bd16 for l in lmats]
    ps = [eye - ld for ld in lds]
    ms = [_mm(ld, ld) for ld in lds]
    for it in range(3):
        ps = [p + _mm(p, m) for p, m in zip(ps, ms)]
        if it < 2:
            ms = [_mm(m, m) for m in ms]
    for msk in levels:
        ts = [_mm(l * msk, p) for l, p in zip(lmats, ps)]
        ps = [p - _mm(p, t) for p, t in zip(ps, ts)]
    return ps


def _gdn_kernel(q_ref, k_ref, v_ref, z_ref, sm_ref, wq_ref, wk_ref, wv_ref, prm_ref, nw_ref,
                o_ref, gc_ref, gct_ref, cv_s, q_s, k_s, kb_s, rhs_s, qd_s, a_s, b_s, qp_s, oacc_s, *, n_vh):
    hq = pl.program_id(1)
    seq = q_ref.shape[0]
    dk = GDN_HEAD_DIM
    nchunk = seq // CHUNK

    _causal_conv_silu(q_ref, wq_ref[...], None, cv_s.at[:, 0:dk], chunked=False)
    q = cv_s[:, 0:dk]
    _causal_conv_silu(k_ref, wk_ref[...], None, cv_s.at[:, 0:dk], chunked=False)
    k = cv_s[:, 0:dk]
    _causal_conv_silu(v_ref, wv_ref[...], None, cv_s, chunked=False)
    v = cv_s[...]
    q = q * lax.rsqrt(jnp.sum(q * q, axis=-1, keepdims=True) + NORM_EPS) * (dk ** -0.5)
    k = k * lax.rsqrt(jnp.sum(k * k, axis=-1, keepdims=True) + NORM_EPS)
    q_s[...] = q.astype(BF16)
    k_s[...] = k.astype(BF16)

    sm = sm_ref[...]
    beta_all = _sigmoid(sm)
    g_all = -jnp.exp(prm_ref[0:1, :]) * _softplus(sm + prm_ref[1:2, :])
    r = lax.broadcasted_iota(jnp.int32, (LANES, LANES), 0)
    c = lax.broadcasted_iota(jnp.int32, (LANES, LANES), 1)
    sel_g = ((c < 2) & (r == n_vh + 2 * hq + c)).astype(BF16)
    sel_b = ((c < 2) & (r == 2 * hq + c)).astype(BF16)
    g_sel = _select_cols(g_all, sel_g)
    beta = _select_cols(beta_all, sel_b)
    _chunk_cumsum(g_sel, gc_ref)
    gc = gc_ref[...]
    gct_ref[...] = gc.T
    eg = jnp.exp(gc)
    for hv in range(2):
        bcol = beta[:, hv:hv + 1]
        ecol = eg[:, hv:hv + 1]
        kb = k * bcol
        kb_s[hv] = kb.astype(BF16)
        rhs_s[hv, :, 0:dk] = (v[:, hv * dk:(hv + 1) * dk] * bcol).astype(BF16)
        rhs_s[hv, :, dk:2 * dk] = (kb * ecol).astype(BF16)
        qd_s[hv] = (q * ecol).astype(BF16)

    rr = lax.broadcasted_iota(jnp.int32, (CHUNK, CHUNK), 0)
    cc = lax.broadcasted_iota(jnp.int32, (CHUNK, CHUNK), 1)
    causal = rr >= cc
    strict = rr > cc
    eye = (rr == cc).astype(F32)
    bd16, levels = _tri_masks(CHUNK)

    unroll = GDN_CHUNKS_PER_TRIP

    def terms_body(ci, carry):
        sls = [pl.ds(pl.multiple_of((ci * unroll + s) * CHUNK, CHUNK), CHUNK) for s in range(unroll)]
        k_cs = [k_s[sl, :] for sl in sls]
        qks = [_mm_nt(q_s[sl, :], k_c) for sl, k_c in zip(sls, k_cs)]
        gcols = [gc_ref[sl, :] for sl in sls]
        grows = [gct_ref[:, sl] for sl in sls]
        chains = [(s, hv) for s in range(unroll) for hv in range(2)]
        gc_cs = [gcols[s][:, hv:hv + 1] for s, hv in chains]
        decays = [jnp.where(causal, jnp.exp(jnp.minimum(gc_c - grows[s][hv:hv + 1, :], 0.0)), 0.0)
                  for gc_c, (s, hv) in zip(gc_cs, chains)]
        lmats = [jnp.where(strict, _mm_nt(kb_s[hv, sls[s], :], k_cs[s]) * dec, 0.0)
                 for dec, (s, hv) in zip(decays, chains)]
        pinvs = _unit_lower_inverse(lmats, eye, bd16, levels)
        uws = [_mm(pinv, rhs_s[hv, sls[s], :]) for pinv, (s, hv) in zip(pinvs, chains)]
        kds = [k_cs[s].astype(F32) * jnp.exp(gc_c[CHUNK - 1:CHUNK, :] - gc_c)
               for gc_c, (s, hv) in zip(gc_cs, chains)]
        kdws = [_mm_tn(kd, uw) for kd, uw in zip(kds, uws)]
        aws = [_mm(qks[s] * dec, uw) for dec, uw, (s, hv) in zip(decays, uws, chains)]
        for kdw, aw, (s, hv) in zip(kdws, aws, chains):
            sl = sls[s]
            a_s[hv, sl, :] = (-kdw[:, dk:2 * dk]).astype(BF16)
            b_s[hv, sl, :] = kdw[:, 0:dk]
            qp_s[hv, sl, :] = (qd_s[hv, sl, :].astype(F32) - aw[:, dk:2 * dk]).astype(BF16)
            oacc_s[hv, sl, :] = aw[:, 0:dk]
        return carry

    lax.fori_loop(0, nchunk // unroll, terms_body, 0)

    def recur_body(ci, states):
        sl = pl.ds(pl.multiple_of(ci * CHUNK, CHUNK), CHUNK)
        g_last = gc_ref[pl.ds(ci * CHUNK + CHUNK - 1, 1), :]
        new_states = []
        for hv in range(2):
            sb = states[hv].astype(BF16)
            oacc_s[hv, sl, :] += _mm(qp_s[hv, sl, :], sb)
            new_states.append(states[hv] * jnp.exp(g_last[:, hv:hv + 1])
                              + _mm(a_s[hv, sl, :], sb) + b_s[hv, sl, :])
        return tuple(new_states)

    zero = jnp.zeros((dk, dk), F32)
    lax.fori_loop(0, nchunk, recur_body, (zero, zero))

    z = z_ref[...].astype(F32)
    nw = nw_ref[...]
    for hv in range(2):
        o = _rms(oacc_s[hv], nw)
        zz = z[:, hv * dk:(hv + 1) * dk]
        o_ref[:, hv * dk:(hv + 1) * dk] = (o * _silu(zz)).astype(o_ref.dtype)


def _gdn(proj3, small3, conv_w, prm, norm_w, *, qkv_off, z_off, n_qk, out_cols):
    bsz, seq, _ = proj3.shape
    assert seq % (CHUNK * GDN_CHUNKS_PER_TRIP) == 0
    dk = GDN_HEAD_DIM
    qb = qkv_off // dk
    kb = qb + n_qk
    vb = (qkv_off + 2 * n_qk * dk) // (2 * dk)
    zb = z_off // (2 * dk)
    cq = 0
    ck = n_qk
    cv = (2 * n_qk * dk) // (2 * dk)
    return pl.pallas_call(
        functools.partial(_gdn_kernel, n_vh=2 * n_qk),
        grid=(bsz, n_qk),
        in_specs=[
            pl.BlockSpec((None, seq, dk), lambda b, h: (b, 0, qb + h)),
            pl.BlockSpec((None, seq, dk), lambda b, h: (b, 0, kb + h)),
            pl.BlockSpec((None, seq, 2 * dk), lambda b, h: (b, 0, vb + h)),
            pl.BlockSpec((None, seq, 2 * dk), lambda b, h: (b, 0, zb + h)),
            pl.BlockSpec((None, seq, LANES), lambda b, h: (b, 0, 0)),
            pl.BlockSpec((CONV_WIDTH, dk), lambda b, h: (0, cq + h)),
            pl.BlockSpec((CONV_WIDTH, dk), lambda b, h: (0, ck + h)),
            pl.BlockSpec((CONV_WIDTH, 2 * dk), lambda b, h: (0, cv + h)),
            pl.BlockSpec((8, LANES), lambda b, h: (0, 0)),
            pl.BlockSpec((1, dk), lambda b, h: (0, 0)),
        ],
        out_specs=pl.BlockSpec((None, seq, 2 * dk), lambda b, h: (b, 0, h)),
        out_shape=jax.ShapeDtypeStruct((bsz, seq, out_cols), BF16),
        scratch_shapes=[
            pltpu.VMEM((seq, LANES), F32),
            pltpu.VMEM((LANES, seq), F32),
            pltpu.VMEM((seq, 2 * dk), F32),
            pltpu.VMEM((seq, dk), BF16),
            pltpu.VMEM((seq, dk), BF16),
            pltpu.VMEM((2, seq, dk), BF16),
            pltpu.VMEM((2, seq, 2 * dk), BF16),
            pltpu.VMEM((2, seq, dk), BF16),
            pltpu.VMEM((2, seq, dk), BF16),
            pltpu.VMEM((2, seq, dk), F32),
            pltpu.VMEM((2, seq, dk), BF16),
            pltpu.VMEM((2, seq, dk), F32),
        ],
        compiler_params=_params(("parallel", "parallel")),
        name="gdn",
    )(proj3, proj3, proj3, proj3, small3, conv_w, conv_w, conv_w, prm, norm_w)


def _ssd_kernel(x_ref, b_ref, c_ref, z_ref, sm_ref, wx_ref, wb_ref, wc_ref, cbx_ref, cbb_ref, cbc_ref,
                prm_ref, drep_ref, nw_ref, o_ref,
                acs_ref, acst_ref, cv_s, xs_s, bt_s, c_s, st_s, *, dt_col0, heads_per_group):
    grp = pl.program_id(1)
    seq = x_ref.shape[0]
    hp = SSM_HEAD_DIM
    hg = heads_per_group
    npair = hg // 2
    gw = hg * hp
    nchunk = seq // CHUNK

    _causal_conv_silu(x_ref, wx_ref[...], cbx_ref[...], xs_s, chunked=True)
    _causal_conv_silu(b_ref, wb_ref[...], cbb_ref[...], cv_s, chunked=True)
    bt_s[...] = cv_s[...].T.astype(BF16)
    _causal_conv_silu(c_ref, wc_ref[...], cbc_ref[...], cv_s, chunked=True)
    c_s[...] = cv_s[...].astype(BF16)

    sm = sm_ref[...]
    dt_all = _softplus(sm + prm_ref[1:2, :])
    a_all = -jnp.exp(prm_ref[0:1, :]) * dt_all
    r = lax.broadcasted_iota(jnp.int32, (LANES, LANES), 0)
    c = lax.broadcasted_iota(jnp.int32, (LANES, LANES), 1)
    sel = ((c < hg) & (r == dt_col0 + hg * grp + c)).astype(BF16)
    a_g = _select_cols(a_all, sel)
    _chunk_cumsum(a_g, acs_ref)
    acst_ref[...] = acs_ref[...].T
    st_s[...] = jnp.zeros_like(st_s)

    def replicate(width):
        rr_ = lax.broadcasted_iota(jnp.int32, (LANES, hg * width), 0)
        cc_ = lax.broadcasted_iota(jnp.int32, (LANES, hg * width), 1)
        return rr_, cc_ // width

    r64, c64 = replicate(hp)
    rep_a = (r64 == c64).astype(BF16)
    rep_dt = (r64 == dt_col0 + hg * grp + c64).astype(BF16)
    r128, c128 = replicate(CHUNK)
    rep_full = (r128 == c128).astype(BF16)
    lane = lax.broadcasted_iota(jnp.int32, (CHUNK, gw), 1)
    first = (lane % (2 * hp)) < hp
    rr = lax.broadcasted_iota(jnp.int32, (CHUNK, CHUNK), 0)
    cc = lax.broadcasted_iota(jnp.int32, (CHUNK, CHUNK), 1)
    causal = rr >= cc
    dt_bias = prm_ref[1:2, :]

    def chunk_body(ci, carry):
        sl = pl.ds(pl.multiple_of(ci * CHUNK, CHUNK), CHUNK)
        bt_c = bt_s[:, sl]
        c_c = c_s[sl, :]
        cb = jnp.dot(c_c, bt_c, preferred_element_type=F32)
        acs_c = acs_ref[sl, :]
        arow = acst_ref[:, sl]
        acrep = _select_cols(acs_c, rep_a)
        acf = _select_cols(acs_c, rep_full)
        xs_c = xs_s[sl, :]
        xdt = xs_c * _select_cols(_softplus(sm_ref[sl, :] + dt_bias), rep_dt)
        eac = jnp.exp(acrep)
        e_last = eac[CHUNK - 1:CHUNK, :]
        xdec = (xdt * jnp.exp(acrep[CHUNK - 1:CHUNK, :] - acrep)).astype(BF16)
        xlo = jnp.where(first, xdt, 0.0).astype(BF16)
        xhi = jnp.where(first, 0.0, xdt).astype(BF16)
        lms = [cb * jnp.where(causal, jnp.exp(jnp.minimum(
            acf[:, j * CHUNK:(j + 1) * CHUNK] - arow[j:j + 1, :], 0.0)), 0.0) for j in range(hg)]
        pairs = range(npair)
        cols = [slice(p * 2 * hp, (p + 1) * 2 * hp) for p in pairs]
        prevs = [st_s[p] for p in pairs]
        y_offs = [_mm(c_c, prev) for prev in prevs]
        y_diags = [jnp.dot(jnp.concatenate([lms[2 * p], lms[2 * p + 1]], axis=1).astype(BF16),
                           jnp.concatenate([xlo[:, cols[p]], xhi[:, cols[p]]], axis=0),
                           preferred_element_type=F32) for p in pairs]
        st_news = [jnp.dot(bt_c, xdec[:, cols[p]], preferred_element_type=F32) for p in pairs]
        for p in pairs:
            st_s[p] = prevs[p] * e_last[:, cols[p]] + st_news[p]
        y = jnp.concatenate([y_diags[p] + y_offs[p] * eac[:, cols[p]] for p in pairs], axis=1)
        y = (y + drep_ref[...] * xs_c) * _silu(z_ref[sl, :].astype(F32))
        o_ref[sl, :] = _rms(y, nw_ref[...]).astype(o_ref.dtype)
        return carry

    lax.fori_loop(0, nchunk, chunk_body, 0)


def _ssd(proj3, small3, conv_w, conv_b, prm, d_rep, norm_w, *, xbc_off, z_off, inner, dt_col0):
    bsz, seq, _ = proj3.shape
    gw = inner // SSM_GROUPS
    hg = gw // SSM_HEAD_DIM
    ns = SSM_STATE
    xb = xbc_off // gw
    bb = (xbc_off + inner) // ns
    cb = bb + SSM_GROUPS
    zb = z_off // gw
    wbb = inner // ns
    wcb = wbb + SSM_GROUPS
    kern = functools.partial(_ssd_kernel, dt_col0=dt_col0, heads_per_group=hg)
    return pl.pallas_call(
        kern,
        grid=(bsz, SSM_GROUPS),
        in_specs=[
            pl.BlockSpec((None, seq, gw), lambda b, g: (b, 0, xb + g)),
            pl.BlockSpec((None, seq, ns), lambda b, g: (b, 0, bb + g)),
            pl.BlockSpec((None, seq, ns), lambda b, g: (b, 0, cb + g)),
            pl.BlockSpec((None, seq, gw), lambda b, g: (b, 0, zb + g)),
            pl.BlockSpec((None, seq, LANES), lambda b, g: (b, 0, 0)),
            pl.BlockSpec((CONV_WIDTH, gw), lambda b, g: (0, g)),
            pl.BlockSpec((CONV_WIDTH, ns), lambda b, g: (0, wbb + g)),
            pl.BlockSpec((CONV_WIDTH, ns), lambda b, g: (0, wcb + g)),
            pl.BlockSpec((1, gw), lambda b, g: (0, g)),
            pl.BlockSpec((1, ns), lambda b, g: (0, wbb + g)),
            pl.BlockSpec((1, ns), lambda b, g: (0, wcb + g)),
            pl.BlockSpec((8, LANES), lambda b, g: (0, 0)),
            pl.BlockSpec((1, gw), lambda b, g: (0, g)),
            pl.BlockSpec((1, gw), lambda b, g: (0, g)),
        ],
        out_specs=pl.BlockSpec((None, seq, gw), lambda b, g: (b, 0, g)),
        out_shape=jax.ShapeDtypeStruct((bsz, seq, inner), BF16),
        scratch_shapes=[
            pltpu.VMEM((seq, LANES), F32),
            pltpu.VMEM((LANES, seq), F32),
            pltpu.VMEM((seq, ns), F32),
            pltpu.VMEM((seq, gw), F32),
            pltpu.VMEM((ns, seq), BF16),
            pltpu.VMEM((seq, ns), BF16),
            pltpu.VMEM((hg // 2, ns, 2 * SSM_HEAD_DIM), F32),
        ],
        compiler_params=_params(("parallel", "parallel")),
        name="ssd",
    )(proj3, proj3, proj3, proj3, small3, conv_w, conv_w, conv_w, conv_b, conv_b, conv_b,
      prm, d_rep, norm_w)


def _out_proj_kernel(a_ref, b_ref, wa_ref, wb_ref, x_ref, o_ref):
    acc = jnp.dot(a_ref[...], wa_ref[...], preferred_element_type=F32)
    acc = acc + jnp.dot(b_ref[...], wb_ref[...], preferred_element_type=F32)
    o_ref[...] = x_ref[...] + acc


def _out_proj(o_a, o_b, w_a, w_b, x2, tm=512, tn=1024):
    tok, d = x2.shape
    ka = o_a.shape[1]
    kb = o_b.shape[1]
    return pl.pallas_call(
        _out_proj_kernel,
        grid=(tok // tm, d // tn),
        in_specs=[
            pl.BlockSpec((tm, ka), lambda i, j: (i, 0)),
            pl.BlockSpec((tm, kb), lambda i, j: (i, 0)),
            pl.BlockSpec((ka, tn), lambda i, j: (0, j)),
            pl.BlockSpec((kb, tn), lambda i, j: (0, j)),
            pl.BlockSpec((tm, tn), lambda i, j: (i, j)),
        ],
        out_specs=pl.BlockSpec((tm, tn), lambda i, j: (i, j)),
        out_shape=jax.ShapeDtypeStruct((tok, d), F32),
        compiler_params=_params(("parallel", "parallel")),
        name="out_proj",
    )(o_a, o_b, w_a, w_b, x2)


def _router_kernel(h_ref, nw_ref, wr_ref, br_ref, u_ref, idx_ref, gate_ref, rank_ref, cnt_ref, carry_s):
    step = pl.program_id(0)
    tm = h_ref.shape[0]

    @pl.when(step == 0)
    def _():
        carry_s[...] = jnp.zeros_like(carry_s)

    u = _rms(h_ref[...], nw_ref[...])
    _store_row_tiles(u_ref, _pack_bf16_pairs(u))
    logits = _mm_exact(u, wr_ref[...]) + br_ref[...]
    lane = lax.broadcasted_iota(jnp.int32, (tm, LANES), 1).astype(F32)
    work = logits
    vals, idxs = [], []
    for _ in range(TOP_K):
        m = jnp.max(work, axis=1, keepdims=True)
        idx = jnp.min(jnp.where(work == m, lane, float(LANES)), axis=1, keepdims=True)
        vals.append(m)
        idxs.append(idx)
        work = jnp.where(lane == idx, -jnp.inf, work)
    onehot = (work == -jnp.inf).astype(F32)
    exps = [jnp.exp(v - vals[0]) for v in vals]
    denom = exps[0]
    for e in exps[1:]:
        denom = denom + e
    r = lax.broadcasted_iota(jnp.int32, (tm, tm), 0)
    c = lax.broadcasted_iota(jnp.int32, (tm, tm), 1)
    below = (r > c).astype(BF16)
    excl = jnp.dot(below, onehot.astype(BF16), preferred_element_type=F32) + carry_s[...]
    idx_out = jnp.zeros((tm, LANES), F32)
    gate_out = jnp.zeros((tm, LANES), F32)
    rank_out = jnp.zeros((tm, LANES), F32)
    for kk in range(TOP_K):
        rank_k = jnp.sum(jnp.where(lane == idxs[kk], excl, 0.0), axis=1, keepdims=True)
        idx_out = jnp.where(lane == kk, idxs[kk], idx_out)
        gate_out = jnp.where(lane == kk, exps[kk] / denom, gate_out)
        rank_out = jnp.where(lane == kk, rank_k, rank_out)
    idx_ref[...] = idx_out.astype(jnp.int32)
    gate_ref[...] = gate_out
    rank_ref[...] = rank_out.astype(jnp.int32)
    total = carry_s[...] + jnp.sum(onehot, axis=0, keepdims=True)
    carry_s[...] = total
    cnt_ref[...] = jnp.broadcast_to(total, cnt_ref.shape)


def _router(h2d, norm_w, w_r, b_r, tm=512):
    tok, d = h2d.shape
    return pl.pallas_call(
        _router_kernel,
        grid=(tok // tm,),
        in_specs=[
            pl.BlockSpec((tm, d), lambda i: (i, 0)),
            pl.BlockSpec((1, d), lambda i: (0, 0)),
            pl.BlockSpec((d, LANES), lambda i: (0, 0)),
            pl.BlockSpec((1, LANES), lambda i: (0, 0)),
        ],
        out_specs=[
            pl.BlockSpec((tm * ROW_TILE, LANES), lambda i: (i, 0)),
            pl.BlockSpec((tm, LANES), lambda i: (i, 0)),
            pl.BlockSpec((tm, LANES), lambda i: (i, 0)),
            pl.BlockSpec((tm, LANES), lambda i: (i, 0)),
            pl.BlockSpec((8, LANES), lambda i: (0, 0)),
        ],
        out_shape=[
            jax.ShapeDtypeStruct((tok * ROW_TILE, LANES), jnp.uint32),
            jax.ShapeDtypeStruct((tok, LANES), jnp.int32),
            jax.ShapeDtypeStruct((tok, LANES), F32),
            jax.ShapeDtypeStruct((tok, LANES), jnp.int32),
            jax.ShapeDtypeStruct((8, LANES), F32),
        ],
        scratch_shapes=[pltpu.VMEM((1, LANES), F32)],
        compiler_params=_params(("arbitrary",)),
        name="router",
    )(h2d, norm_w, w_r, b_r)


def _dispatch_kernel(pos_ref, u_ref, xs_in, xs_hbm, sem, *, tm):
    del xs_in

    def row_copy(t, k):
        dst = pl.multiple_of(pos_ref[t * TOP_K + k], ROW_TILE)
        return pltpu.make_async_copy(u_ref.at[pl.ds(pl.multiple_of(t * ROW_TILE, ROW_TILE), ROW_TILE)],
                                     xs_hbm.at[pl.ds(dst, ROW_TILE)], sem)

    def issue(t, carry):
        for k in range(TOP_K):
            row_copy(t, k).start(priority=k % 2)
        return carry

    def drain(t, carry):
        for k in range(TOP_K):
            row_copy(t, k).wait()
        return carry

    lax.fori_loop(0, tm, issue, 0)
    lax.fori_loop(0, tm, drain, 0)


def _dispatch(pos_flat, u_tiles, xs_init, tm=256):
    tok = u_tiles.shape[0] // ROW_TILE
    kern = functools.partial(_dispatch_kernel, tm=tm)
    return pl.pallas_call(
        kern,
        grid=(tok // tm,),
        in_specs=[
            pl.BlockSpec((tm * TOP_K,), lambda i: (i,), memory_space=pltpu.SMEM),
            pl.BlockSpec((tm * ROW_TILE, LANES), lambda i: (i, 0)),
            pl.BlockSpec(memory_space=pl.ANY),
        ],
        out_specs=pl.BlockSpec(memory_space=pl.ANY),
        out_shape=jax.ShapeDtypeStruct(xs_init.shape, xs_init.dtype),
        scratch_shapes=[pltpu.SemaphoreType.DMA(())],
        input_output_aliases={2: 0},
        compiler_params=_params(("arbitrary",)),
        name="dispatch",
    )(pos_flat, u_tiles, xs_init)


def _moe_kernel(bexp_ref, nval_ref, x_ref, w1g_ref, w1l_ref, b1g_ref, b1l_ref, w2_ref, b2_ref,
                o_ref, acc_s, xb_s, *, sub):
    i = pl.program_id(0)
    j = pl.program_id(1)
    nf = pl.num_programs(1)
    nval = nval_ref[i]
    bm, d = acc_s.shape

    @pl.when((nval > 0) & (j == 0))
    def _():
        acc_s[...] = jnp.broadcast_to(b2_ref[...], acc_s.shape)

    def process(subs, first_slice):
        wg = w1g_ref[...].astype(BF16)
        wl = w1l_ref[...].astype(BF16)
        w2 = w2_ref[...].astype(BF16)
        rows = [pl.ds(s * sub, sub) for s in subs]
        if first_slice:
            xs = [_unpack_row_tiles(_load_row_tiles(x_ref, s * sub, sub)) for s in subs]
            for r, x in zip(rows, xs):
                xb_s[r, :] = x
        else:
            xs = [xb_s[r, :] for r in rows]
        hgs = [jnp.dot(x, wg, preferred_element_type=F32) + b1g_ref[...] for x in xs]
        hls = [jnp.dot(x, wl, preferred_element_type=F32) + b1l_ref[...] for x in xs]
        acts = []
        for hg, hl in zip(hgs, hls):
            glu = jnp.minimum(hg, SWIGLU_LIMIT)
            lin = jnp.clip(hl, -SWIGLU_LIMIT, SWIGLU_LIMIT)
            acts.append((glu * _sigmoid(SWIGLU_ALPHA * glu) * (lin + 1.0)).astype(BF16))
        for r, act in zip(rows, acts):
            acc_s[r, :] += jnp.dot(act, w2, preferred_element_type=F32)

    for pair in range(bm // (2 * sub)):
        first, second = 2 * pair, 2 * pair + 1

        for first_slice in (True, False):
            on_slice = (j == 0) if first_slice else (j > 0)

            @pl.when((nval > second * sub) & on_slice)
            def _(first=first, second=second, first_slice=first_slice):
                process([first, second], first_slice)

            @pl.when((nval > first * sub) & (nval <= second * sub) & on_slice)
            def _(first=first, first_slice=first_slice):
                process([first], first_slice)

    @pl.when((nval > 0) & (j == nf - 1))
    def _():
        _store_row_tiles(o_ref, _pack_bf16_pairs(acc_s[...]))

    @pl.when((nval == 0) & (j == 0))
    def _():
        o_ref[...] = jnp.zeros_like(o_ref)


def _moe(bexp, nval, xs, w1, b1, w2, b2, *, bm, tf, sub):
    rows = xs.shape[0] // ROW_TILE
    d = w1.shape[1]
    assert d == 2 * ROW_TILE * LANES, "a packed row must fill exactly one (8, 128) tile"
    ff = w2.shape[1]
    nb = rows // bm
    nf = ff // tf
    bt = bm * ROW_TILE

    def jj(i, j, nv):
        return jnp.where(nv[i] > 0, j, nf - 1)

    grid_spec = pltpu.PrefetchScalarGridSpec(
        num_scalar_prefetch=2,
        grid=(nb, nf),
        in_specs=[
            pl.BlockSpec((bt, LANES), lambda i, j, be, nv: (jnp.where(nv[i] > 0, i, 0), 0)),
            pl.BlockSpec((None, d, tf), lambda i, j, be, nv: (be[i], 0, jj(i, j, nv))),
            pl.BlockSpec((None, d, tf), lambda i, j, be, nv: (be[i], 0, nf + jj(i, j, nv))),
            pl.BlockSpec((None, 1, tf), lambda i, j, be, nv: (be[i], 0, jj(i, j, nv))),
            pl.BlockSpec((None, 1, tf), lambda i, j, be, nv: (be[i], 0, nf + jj(i, j, nv))),
            pl.BlockSpec((None, tf, d), lambda i, j, be, nv: (be[i], jj(i, j, nv), 0)),
            pl.BlockSpec((None, 1, d), lambda i, j, be, nv: (be[i], 0, 0)),
        ],
        out_specs=pl.BlockSpec((bt, LANES), lambda i, j, be, nv: (i, 0)),
        scratch_shapes=[
            pltpu.VMEM((bm, d), F32),
            pltpu.VMEM((bm, d), BF16),
        ],
    )
    return pl.pallas_call(
        functools.partial(_moe_kernel, sub=sub),
        grid_spec=grid_spec,
        out_shape=jax.ShapeDtypeStruct((rows * ROW_TILE, LANES), jnp.uint32),
        compiler_params=_params(("arbitrary", "arbitrary"), MOE_VMEM_LIMIT),
        name="moe",
    )(bexp, nval, xs, w1, w1, b1, b1, w2, b2)


def _combine_kernel(pos_ref, gate_ref, h_ref, ys_hbm, p_ref, pn_ref, wg_ref, wp_ref, fn_ref, o_ref,
                    gbuf, sem, *, tm):
    def row_copy(t, k):
        src = pl.multiple_of(pos_ref[t * TOP_K + k], ROW_TILE)
        return pltpu.make_async_copy(
            ys_hbm.at[pl.ds(src, ROW_TILE)],
            gbuf.at[k, pl.ds(pl.multiple_of(t * ROW_TILE, ROW_TILE), ROW_TILE)], sem)

    def issue(t, carry):
        for k in range(TOP_K):
            row_copy(t, k).start(priority=k % 2)
        return carry

    def drain(t, carry):
        for k in range(TOP_K):
            row_copy(t, k).wait()
        return carry

    lax.fori_loop(0, tm, issue, 0)
    lax.fori_loop(0, tm, drain, 0)

    gate = gate_ref[...]
    h = h_ref[...]
    moe = None
    for k in range(TOP_K):
        halves = [_unpack_bf16_pairs(p) for p in _load_row_tiles(gbuf.at[k], 0, tm)]
        row = jnp.concatenate([lo for lo, _ in halves] + [hi for _, hi in halves], axis=1)
        term = gate[:, k:k + 1] * row
        moe = term if moe is None else moe + term
    h = h + moe
    u = _rms(h, pn_ref[...]).astype(BF16)
    gv = _sigmoid(jnp.dot(u, wg_ref[...], preferred_element_type=F32))
    pp = jnp.dot(p_ref[...].astype(BF16), wp_ref[...], preferred_element_type=F32)
    h = h + pp * gv
    o_ref[...] = _rms(h, fn_ref[...])


def _combine(pos_flat, gate, h2d, ys, p2d, ple_norm, w_gate, w_proj, final_norm, tm=256):
    tok, d = h2d.shape
    pd = p2d.shape[1]
    kern = functools.partial(_combine_kernel, tm=tm)
    return pl.pallas_call(
        kern,
        grid=(tok // tm,),
        in_specs=[
            pl.BlockSpec((tm * TOP_K,), lambda i: (i,), memory_space=pltpu.SMEM),
            pl.BlockSpec((tm, LANES), lambda i: (i, 0)),
            pl.BlockSpec((tm, d), lambda i: (i, 0)),
            pl.BlockSpec(memory_space=pl.ANY),
            pl.BlockSpec((tm, pd), lambda i: (i, 0)),
            pl.BlockSpec((1, d), lambda i: (0, 0)),
            pl.BlockSpec((d, d), lambda i: (0, 0)),
            pl.BlockSpec((pd, d), lambda i: (0, 0)),
            pl.BlockSpec((1, d), lambda i: (0, 0)),
        ],
        out_specs=pl.BlockSpec((tm, d), lambda i: (i, 0)),
        out_shape=jax.ShapeDtypeStruct((tok, d), F32),
        scratch_shapes=[pltpu.VMEM((TOP_K, tm * ROW_TILE, LANES), jnp.uint32),
                        pltpu.SemaphoreType.DMA(())],
        compiler_params=_params(("arbitrary",)),
        name="combine",
    )(pos_flat, gate, h2d, ys, p2d, ple_norm, w_gate, w_proj, final_norm)


def _pad_cols(v, off):
    return jnp.pad(v.astype(F32), (off, LANES - off - v.shape[0]))


def _layer(h3, p3, mix_norm, w_in, gdn_conv_w, gdn_a_log, gdn_dt_bias, gdn_norm_w, ssm_conv_w,
           ssm_conv_b, ssm_a_log, ssm_dt_bias, ssm_d, ssm_norm_w, w_out, ffn_norm, w_router, b_router,
           w_mlp1, b_mlp1, w_mlp2, b_mlp2, ple_norm, w_ple_gate, w_ple_proj, final_norm, *, moe_bm, moe_tf, moe_sub):
    bsz, seq, d = h3.shape
    tok = bsz * seq
    n_vh = gdn_a_log.shape[0]
    n_qk = n_vh // 2
    key_dim = n_qk * GDN_HEAD_DIM
    val_dim = n_vh * GDN_HEAD_DIM
    gdn_conv_ch = 2 * key_dim + val_dim
    ssm_heads = ssm_a_log.shape[0]
    inner = ssm_heads * SSM_HEAD_DIM
    ssm_conv_ch = inner + 2 * SSM_GROUPS * SSM_STATE

    o0 = 0
    o_qkv, o0 = o0, o0 + gdn_conv_ch
    o_z, o0 = o0, o0 + val_dim
    o_b, o0 = o0, o0 + n_vh
    o_a, o0 = o0, o0 + n_vh
    o_xbc, o0 = o0, o0 + ssm_conv_ch
    o_sz, o0 = o0, o0 + inner
    o_dt, o0 = o0, o0 + ssm_heads
    w_main = jnp.concatenate([w_in[:, o_qkv:o_b], w_in[:, o_xbc:o_dt]], axis=1).astype(BF16)
    w_small = jnp.concatenate([w_in[:, o_b:o_xbc], w_in[:, o_dt:o0]], axis=1)
    n_small = w_small.shape[1]
    w_small = jnp.pad(w_small, ((0, 0), (0, LANES - n_small))).astype(BF16)
    m_qkv, m_z = 0, gdn_conv_ch
    m_xbc = gdn_conv_ch + val_dim
    m_sz = m_xbc + ssm_conv_ch
    dt_col0 = 2 * n_vh

    x2 = h3.reshape(tok, d)
    proj, small = _in_proj(x2, mix_norm.reshape(1, d), w_main, w_small)
    proj3 = proj.reshape(bsz, seq, proj.shape[1])
    small3 = small.reshape(bsz, seq, LANES)

    zrow = jnp.zeros((LANES,), F32)
    gdn_prm = jnp.stack([_pad_cols(gdn_a_log, n_vh), _pad_cols(gdn_dt_bias, n_vh)] + [zrow] * 6)
    o_gdn = _gdn(proj3, small3, gdn_conv_w, gdn_prm, gdn_norm_w.reshape(1, GDN_HEAD_DIM),
                 qkv_off=m_qkv, z_off=m_z, n_qk=n_qk, out_cols=val_dim)

    ssm_prm = jnp.stack([_pad_cols(ssm_a_log, dt_col0), _pad_cols(ssm_dt_bias, dt_col0)] + [zrow] * 6)
    d_rep = jnp.repeat(ssm_d.astype(F32), SSM_HEAD_DIM).reshape(1, inner)
    o_ssm = _ssd(proj3, small3, ssm_conv_w, ssm_conv_b.reshape(1, ssm_conv_ch), ssm_prm, d_rep,
                 ssm_norm_w.reshape(1, inner), xbc_off=m_xbc, z_off=m_sz, inner=inner, dt_col0=dt_col0)

    w_out_b = w_out.astype(BF16)
    h1 = _out_proj(o_gdn.reshape(tok, val_dim), o_ssm.reshape(tok, inner),
                   w_out_b[:val_dim], w_out_b[val_dim:], x2)

    n_exp = w_router.shape[1]
    w_r = jnp.pad(w_router, ((0, 0), (0, LANES - n_exp)))
    b_r = jnp.pad(b_router.astype(F32), (0, LANES - n_exp), constant_values=NEG_BIG).reshape(1, LANES)
    u2, idx_l, gate_l, rank_l, cnt = _router(h1, ffn_norm.reshape(1, d), w_r, b_r)

    counts = cnt[0, :n_exp].astype(jnp.int32)
    padded = ((counts + moe_bm - 1) // moe_bm) * moe_bm
    pad_end = jnp.cumsum(padded)
    pad_start = pad_end - padded
    top_idx = idx_l[:, :TOP_K]
    pos = ((pad_start[top_idx] + rank_l[:, :TOP_K]) * ROW_TILE).astype(jnp.int32).reshape(-1)
    n_blocks = (tok * TOP_K) // moe_bm + n_exp
    block_start = jnp.arange(n_blocks, dtype=jnp.int32) * moe_bm
    bexp = jnp.minimum(jnp.sum(block_start[:, None] >= pad_end[None, :], axis=1), n_exp - 1).astype(jnp.int32)
    nval = jnp.clip(pad_start[bexp] + counts[bexp] - block_start, 0, moe_bm)
    nval = jnp.where(block_start < pad_end[-1], nval, 0).astype(jnp.int32)

    xs = _dispatch(pos, u2, jnp.zeros((n_blocks * moe_bm * ROW_TILE, LANES), jnp.uint32))
    ff = w_mlp2.shape[1]
    ys = _moe(bexp, nval, xs, w_mlp1, b_mlp1.reshape(n_exp, 1, 2 * ff), w_mlp2,
              b_mlp2.reshape(n_exp, 1, d), bm=moe_bm, tf=moe_tf, sub=moe_sub)

    out = _combine(pos, gate_l, h1, ys, p3.reshape(tok, p3.shape[-1]), ple_norm.reshape(1, d),
                   w_ple_gate.astype(BF16), w_ple_proj.astype(BF16), final_norm.reshape(1, d))
    return out.reshape(bsz, seq, d)


def kernel(x, p, mix_norm, w_in, gdn_conv_w, gdn_a_log, gdn_dt_bias, gdn_norm_w, ssm_conv_w, ssm_conv_b,
           ssm_a_log, ssm_dt_bias, ssm_d, ssm_norm_w, w_out, ffn_norm, w_router, b_router, w_mlp1, b_mlp1,
           w_mlp2, b_mlp2, ple_norm, w_ple_gate, w_ple_proj, final_norm):
    assert mix_norm.shape[0] == 1, "single-layer trunk"
    return _layer(x, p[0], mix_norm[0], w_in[0], gdn_conv_w[0], gdn_a_log[0], gdn_dt_bias[0], gdn_norm_w[0],
                  ssm_conv_w[0], ssm_conv_b[0], ssm_a_log[0], ssm_dt_bias[0], ssm_d[0], ssm_norm_w[0],
                  w_out[0], ffn_norm[0], w_router[0], b_router[0], w_mlp1[0], b_mlp1[0], w_mlp2[0],
                  b_mlp2[0], ple_norm[0], w_ple_gate[0], w_ple_proj[0], final_norm, moe_bm=1024, moe_tf=512, moe_sub=256)
```

```python
import functools

import jax
import jax.numpy as jnp
from jax import lax
from jax.experimental import pallas as pl
from jax.experimental.pallas import tpu as pltpu

F32 = jnp.float32
BF16 = jnp.bfloat16
HIGHEST = lax.Precision.HIGHEST

NORM_EPS = 1e-6
CONV_WIDTH = 4
GDN_HEAD_DIM = 128
SSM_HEAD_DIM = 64
SSM_GROUPS = 4
SSM_STATE = 128
N_EXPERTS = 32
TOP_K = 4
SWIGLU_ALPHA = 1.702
SWIGLU_LIMIT = 7.0

LANES = 128
CHUNK = 128
GDN_CHUNKS_PER_TRIP = 16
VMEM_LIMIT = 56 * 1024 * 1024
MOE_VMEM_LIMIT = 60 * 1024 * 1024

NEG_BIG = -1e30


def _params(sem, vmem=VMEM_LIMIT):
    return pltpu.CompilerParams(dimension_semantics=sem, vmem_limit_bytes=vmem)


def _rms(x, w):
    ms = jnp.mean(x * x, axis=-1, keepdims=True)
    return x * lax.rsqrt(ms + NORM_EPS) * w


def _sigmoid(x):
    return 0.5 * jnp.tanh(0.5 * x) + 0.5


def _softplus(x):
    return jnp.maximum(x, 0.0) + jnp.log(1.0 + jnp.exp(-jnp.abs(x)))


def _mm(a, b):
    return jnp.dot(a.astype(BF16), b.astype(BF16), preferred_element_type=F32)


def _mm_nt(a, b):
    return lax.dot_general(a.astype(BF16), b.astype(BF16), (((1,), (1,)), ((), ())),
                           preferred_element_type=F32)


def _mm_tn(a, b):
    return lax.dot_general(a.astype(BF16), b.astype(BF16), (((0,), (0,)), ((), ())),
                           preferred_element_type=F32)


def _mm_exact(a, b):
    return jnp.dot(a, b, precision=HIGHEST, preferred_element_type=F32)


def _pack_bf16_pairs(x):
    n = x.shape[1] // 2
    xb = x.astype(BF16).astype(F32)
    lo = lax.bitcast_convert_type(xb[:, :n], jnp.uint32)
    hi = lax.bitcast_convert_type(xb[:, n:], jnp.uint32)
    return (lo >> 16) | (hi & jnp.uint32(0xFFFF0000))


def _unpack_bf16_pairs(w):
    lo = lax.bitcast_convert_type(w << 16, F32)
    hi = lax.bitcast_convert_type(w & jnp.uint32(0xFFFF0000), F32)
    return lo, hi


ROW_TILE = 8


def _store_row_tiles(ref, packed):
    m = packed.shape[0]
    for s in range(ROW_TILE):
        ref[pl.ds(s, m, stride=ROW_TILE), :] = packed[:, s * LANES:(s + 1) * LANES]


def _load_row_tiles(ref, row0, m):
    return [ref[pl.ds(row0 * ROW_TILE + s, m, stride=ROW_TILE), :] for s in range(ROW_TILE)]


def _unpack_row_tiles(pieces):
    halves = [_unpack_bf16_pairs(p) for p in pieces]
    return jnp.concatenate([lo.astype(BF16) for lo, _ in halves] + [hi.astype(BF16) for _, hi in halves],
                           axis=1)


def _silu(x):
    h = 0.5 * x
    return h * jnp.tanh(h) + h


CONV_HALO = 16


def _causal_conv_silu(x_ref, w, bias, out_ref, chunked):
    seq = x_ref.shape[0]

    def conv(xv, masked):
        acc = xv * w[CONV_WIDTH - 1:CONV_WIDTH, :]
        for k in range(CONV_WIDTH - 1):
            shift = CONV_WIDTH - 1 - k
            xs = pltpu.roll(xv, shift, axis=0)
            if masked:
                row = lax.broadcasted_iota(jnp.int32, xv.shape, 0)
                xs = jnp.where(row >= shift, xs, 0.0)
            acc = acc + xs * w[k:k + 1, :]
        if bias is not None:
            acc = acc + bias
        return _silu(acc)

    if not chunked:
        out_ref[...] = conv(x_ref[...].astype(F32), False).astype(out_ref.dtype)
        out_ref[0:8, :] = conv(x_ref[0:CONV_HALO, :].astype(F32), True)[0:8, :].astype(out_ref.dtype)
        return

    out_ref[0:CHUNK, :] = conv(x_ref[0:CHUNK, :].astype(F32), True).astype(out_ref.dtype)

    def body(ci, carry):
        start = pl.multiple_of(ci * CHUNK, CHUNK)
        win = x_ref[pl.ds(pl.multiple_of(start - CONV_HALO, CONV_HALO), CHUNK + CONV_HALO), :]
        y = conv(win.astype(F32), False)
        out_ref[pl.ds(start, CHUNK), :] = y[CONV_HALO:, :].astype(out_ref.dtype)
        return carry

    lax.fori_loop(1, seq // CHUNK, body, 0)


def _bf16_pieces(x):
    hi = x.astype(BF16)
    r1 = x - hi.astype(F32)
    mid = r1.astype(BF16)
    lo = (r1 - mid.astype(F32)).astype(BF16)
    return hi, mid, lo


def _select_cols(vals, emat):
    out = None
    for piece in _bf16_pieces(vals):
        t = jnp.dot(piece, emat, preferred_element_type=F32)
        out = t if out is None else out + t
    return out


def _chunk_cumsum(vals, out_ref):
    seq = vals.shape[0]
    r = lax.broadcasted_iota(jnp.int32, (CHUNK, CHUNK), 0)
    c = lax.broadcasted_iota(jnp.int32, (CHUNK, CHUNK), 1)
    tri = (r >= c).astype(BF16)
    pieces = _bf16_pieces(vals)
    for i in range(seq // CHUNK):
        rows = slice(i * CHUNK, (i + 1) * CHUNK)
        acc = None
        for piece in pieces:
            t = jnp.dot(tri, piece[rows, :], preferred_element_type=F32)
            acc = t if acc is None else acc + t
        out_ref[rows, :] = acc


def _in_proj_kernel(x_ref, g_ref, w_ref, ws_ref, o_ref, os_ref, u_ref):
    @pl.when(pl.program_id(1) == 0)
    def _():
        u = _rms(x_ref[...], g_ref[...]).astype(BF16)
        u_ref[...] = u
        os_ref[...] = jnp.dot(u, ws_ref[...], preferred_element_type=F32)

    o_ref[...] = jnp.dot(u_ref[...], w_ref[...], preferred_element_type=F32).astype(o_ref.dtype)


def _in_proj(x2, norm_w, w_main, w_small, tm=1024, tn=1024):
    tok, d = x2.shape
    n = w_main.shape[1]
    return pl.pallas_call(
        _in_proj_kernel,
        grid=(tok // tm, n // tn),
        in_specs=[
            pl.BlockSpec((tm, d), lambda i, j: (i, 0)),
            pl.BlockSpec((1, d), lambda i, j: (0, 0)),
            pl.BlockSpec((d, tn), lambda i, j: (0, j)),
            pl.BlockSpec((d, LANES), lambda i, j: (0, 0)),
        ],
        out_specs=[
            pl.BlockSpec((tm, tn), lambda i, j: (i, j)),
            pl.BlockSpec((tm, LANES), lambda i, j: (i, 0)),
        ],
        out_shape=[
            jax.ShapeDtypeStruct((tok, n), BF16),
            jax.ShapeDtypeStruct((tok, LANES), F32),
        ],
        scratch_shapes=[pltpu.VMEM((tm, d), BF16)],
        compiler_params=_params(("parallel", "arbitrary")),
        name="in_proj",
    )(x2, norm_w, w_main, w_small)


def _tri_masks(n):
    r = lax.broadcasted_iota(jnp.int32, (n, n), 0)
    c = lax.broadcasted_iota(jnp.int32, (n, n), 1)
    same = lambda s: (r // s) == (c // s)
    bd16 = same(16).astype(F32)
    levels = []
    s = 16
    while s < n:
        levels.append((same(2 * s) & jnp.logical_not(same(s))).astype(F32))
        s *= 2
    return bd16, levels


def _unit_lower_inverse(lmats, eye, bd16, levels):
    lds = [l * bd16 for l in lmats]
    ps = [eye - ld for ld in lds]
    ms = [_mm(ld, ld) for ld in lds]
    for it in range(3):
        ps = [p + _mm(p, m) for p, m in zip(ps, ms)]
        if it < 2:
            ms = [_mm(m, m) for m in ms]
    for msk in levels:
        ts = [_mm(l * msk, p) for l, p in zip(lmats, ps)]
        ps = [p - _mm(p, t) for p, t in zip(ps, ts)]
    return ps


def _gdn_kernel(q_ref, k_ref, v_ref, z_ref, sm_ref, wq_ref, wk_ref, wv_ref, prm_ref, nw_ref,
                o_ref, gc_ref, gct_ref, cv_s, q_s, k_s, kb_s, rhs_s, qd_s, a_s, b_s, qp_s, oacc_s, *, n_vh):
    hq = pl.program_id(1)
    seq = q_ref.shape[0]
    dk = GDN_HEAD_DIM
    nchunk = seq // CHUNK

    _causal_conv_silu(q_ref, wq_ref[...], None, cv_s.at[:, 0:dk], chunked=False)
    q = cv_s[:, 0:dk]
    _causal_conv_silu(k_ref, wk_ref[...], None, cv_s.at[:, 0:dk], chunked=False)
    k = cv_s[:, 0:dk]
    _causal_conv_silu(v_ref, wv_ref[...], None, cv_s, chunked=False)
    v = cv_s[...]
    q = q * lax.rsqrt(jnp.sum(q * q, axis=-1, keepdims=True) + NORM_EPS) * (dk ** -0.5)
    k = k * lax.rsqrt(jnp.sum(k * k, axis=-1, keepdims=True) + NORM_EPS)
    q_s[...] = q.astype(BF16)
    k_s[...] = k.astype(BF16)

    sm = sm_ref[...]
    beta_all = _sigmoid(sm)
    g_all = -jnp.exp(prm_ref[0:1, :]) * _softplus(sm + prm_ref[1:2, :])
    r = lax.broadcasted_iota(jnp.int32, (LANES, LANES), 0)
    c = lax.broadcasted_iota(jnp.int32, (LANES, LANES), 1)
    sel_g = ((c < 2) & (r == n_vh + 2 * hq + c)).astype(BF16)
    sel_b = ((c < 2) & (r == 2 * hq + c)).astype(BF16)
    g_sel = _select_cols(g_all, sel_g)
    beta = _select_cols(beta_all, sel_b)
    _chunk_cumsum(g_sel, gc_ref)
    gc = gc_ref[...]
    gct_ref[...] = gc.T
    eg = jnp.exp(gc)
    for hv in range(2):
        bcol = beta[:, hv:hv + 1]
        ecol = eg[:, hv:hv + 1]
        kb = k * bcol
        kb_s[hv] = kb.astype(BF16)
        rhs_s[hv, :, 0:dk] = (v[:, hv * dk:(hv + 1) * dk] * bcol).astype(BF16)
        rhs_s[hv, :, dk:2 * dk] = (kb * ecol).astype(BF16)
        qd_s[hv] = (q * ecol).astype(BF16)

    rr = lax.broadcasted_iota(jnp.int32, (CHUNK, CHUNK), 0)
    cc = lax.broadcasted_iota(jnp.int32, (CHUNK, CHUNK), 1)
    causal = rr >= cc
    strict = rr > cc
    eye = (rr == cc).astype(F32)
    bd16, levels = _tri_masks(CHUNK)

    unroll = GDN_CHUNKS_PER_TRIP

    def terms_body(ci, carry):
        sls = [pl.ds(pl.multiple_of((ci * unroll + s) * CHUNK, CHUNK), CHUNK) for s in range(unroll)]
        k_cs = [k_s[sl, :] for sl in sls]
        qks = [_mm_nt(q_s[sl, :], k_c) for sl, k_c in zip(sls, k_cs)]
        gcols = [gc_ref[sl, :] for sl in sls]
        grows = [gct_ref[:, sl] for sl in sls]
        chains = [(s, hv) for s in range(unroll) for hv in range(2)]
        gc_cs = [gcols[s][:, hv:hv + 1] for s, hv in chains]
        decays = [jnp.where(causal, jnp.exp(jnp.minimum(gc_c - grows[s][hv:hv + 1, :], 0.0)), 0.0)
                  for gc_c, (s, hv) in zip(gc_cs, chains)]
        lmats = [jnp.where(strict, _mm_nt(kb_s[hv, sls[s], :], k_cs[s]) * dec, 0.0)
                 for dec, (s, hv) in zip(decays, chains)]
        pinvs = _unit_lower_inverse(lmats, eye, bd16, levels)
        uws = [_mm(pinv, rhs_s[hv, sls[s], :]) for pinv, (s, hv) in zip(pinvs, chains)]
        kds = [k_cs[s].astype(F32) * jnp.exp(gc_c[CHUNK - 1:CHUNK, :] - gc_c)
               for gc_c, (s, hv) in zip(gc_cs, chains)]
        kdws = [_mm_tn(kd, uw) for kd, uw in zip(kds, uws)]
        aws = [_mm(qks[s] * dec, uw) for dec, uw, (s, hv) in zip(decays, uws, chains)]
        for kdw, aw, (s, hv) in zip(kdws, aws, chains):
            sl = sls[s]
            a_s[hv, sl, :] = (-kdw[:, dk:2 * dk]).astype(BF16)
            b_s[hv, sl, :] = kdw[:, 0:dk]
            qp_s[hv, sl, :] = (qd_s[hv, sl, :].astype(F32) - aw[:, dk:2 * dk]).astype(BF16)
            oacc_s[hv, sl, :] = aw[:, 0:dk]
        return carry

    lax.fori_loop(0, nchunk // unroll, terms_body, 0)

    def recur_body(ci, states):
        sl = pl.ds(pl.multiple_of(ci * CHUNK, CHUNK), CHUNK)
        g_last = gc_ref[pl.ds(ci * CHUNK + CHUNK - 1, 1), :]
        new_states = []
        for hv in range(2):
            sb = states[hv].astype(BF16)
            oacc_s[hv, sl, :] += _mm(qp_s[hv, sl, :], sb)
            new_states.append(states[hv] * jnp.exp(g_last[:, hv:hv + 1])
                              + _mm(a_s[hv, sl, :], sb) + b_s[hv, sl, :])
        return tuple(new_states)

    zero = jnp.zeros((dk, dk), F32)
    lax.fori_loop(0, nchunk, recur_body, (zero, zero))

    z = z_ref[...].astype(F32)
    nw = nw_ref[...]
    for hv in range(2):
        o = _rms(oacc_s[hv], nw)
        zz = z[:, hv * dk:(hv + 1) * dk]
        o_ref[:, hv * dk:(hv + 1) * dk] = (o * _silu(zz)).astype(o_ref.dtype)


def _gdn(proj3, small3, conv_w, prm, norm_w, *, qkv_off, z_off, n_qk, out_cols):
    bsz, seq, _ = proj3.shape
    assert seq % (CHUNK * GDN_CHUNKS_PER_TRIP) == 0
    dk = GDN_HEAD_DIM
    qb = qkv_off // dk
    kb = qb + n_qk
    vb = (qkv_off + 2 * n_qk * dk) // (2 * dk)
    zb = z_off // (2 * dk)
    cq = 0
    ck = n_qk
    cv = (2 * n_qk * dk) // (2 * dk)
    return pl.pallas_call(
        functools.partial(_gdn_kernel, n_vh=2 * n_qk),
        grid=(bsz, n_qk),
        in_specs=[
            pl.BlockSpec((None, seq, dk), lambda b, h: (b, 0, qb + h)),
            pl.BlockSpec((None, seq, dk), lambda b, h: (b, 0, kb + h)),
            pl.BlockSpec((None, seq, 2 * dk), lambda b, h: (b, 0, vb + h)),
            pl.BlockSpec((None, seq, 2 * dk), lambda b, h: (b, 0, zb + h)),
            pl.BlockSpec((None, seq, LANES), lambda b, h: (b, 0, 0)),
            pl.BlockSpec((CONV_WIDTH, dk), lambda b, h: (0, cq + h)),
            pl.BlockSpec((CONV_WIDTH, dk), lambda b, h: (0, ck + h)),
            pl.BlockSpec((CONV_WIDTH, 2 * dk), lambda b, h: (0, cv + h)),
            pl.BlockSpec((8, LANES), lambda b, h: (0, 0)),
            pl.BlockSpec((1, dk), lambda b, h: (0, 0)),
        ],
        out_specs=pl.BlockSpec((None, seq, 2 * dk), lambda b, h: (b, 0, h)),
        out_shape=jax.ShapeDtypeStruct((bsz, seq, out_cols), BF16),
        scratch_shapes=[
            pltpu.VMEM((seq, LANES), F32),
            pltpu.VMEM((LANES, seq), F32),
            pltpu.VMEM((seq, 2 * dk), F32),
            pltpu.VMEM((seq, dk), BF16),
            pltpu.VMEM((seq, dk), BF16),
            pltpu.VMEM((2, seq, dk), BF16),
            pltpu.VMEM((2, seq, 2 * dk), BF16),
            pltpu.VMEM((2, seq, dk), BF16),
            pltpu.VMEM((2, seq, dk), BF16),
            pltpu.VMEM((2, seq, dk), F32),
            pltpu.VMEM((2, seq, dk), BF16),
            pltpu.VMEM((2, seq, dk), F32),
        ],
        compiler_params=_params(("parallel", "parallel")),
        name="gdn",
    )(proj3, proj3, proj3, proj3, small3, conv_w, conv_w, conv_w, prm, norm_w)


def _ssd_kernel(x_ref, b_ref, c_ref, z_ref, sm_ref, wx_ref, wb_ref, wc_ref, cbx_ref, cbb_ref, cbc_ref,
                prm_ref, drep_ref, nw_ref, o_ref,
                acs_ref, acst_ref, cv_s, xs_s, bt_s, c_s, st_s, *, dt_col0, heads_per_group):
    grp = pl.program_id(1)
    seq = x_ref.shape[0]
    hp = SSM_HEAD_DIM
    hg = heads_per_group
    npair = hg // 2
    gw = hg * hp
    nchunk = seq // CHUNK

    _causal_conv_silu(x_ref, wx_ref[...], cbx_ref[...], xs_s, chunked=True)
    _causal_conv_silu(b_ref, wb_ref[...], cbb_ref[...], cv_s, chunked=True)
    bt_s[...] = cv_s[...].T.astype(BF16)
    _causal_conv_silu(c_ref, wc_ref[...], cbc_ref[...], cv_s, chunked=True)
    c_s[...] = cv_s[...].astype(BF16)

    sm = sm_ref[...]
    dt_all = _softplus(sm + prm_ref[1:2, :])
    a_all = -jnp.exp(prm_ref[0:1, :]) * dt_all
    r = lax.broadcasted_iota(jnp.int32, (LANES, LANES), 0)
    c = lax.broadcasted_iota(jnp.int32, (LANES, LANES), 1)
    sel = ((c < hg) & (r == dt_col0 + hg * grp + c)).astype(BF16)
    a_g = _select_cols(a_all, sel)
    _chunk_cumsum(a_g, acs_ref)
    acst_ref[...] = acs_ref[...].T
    st_s[...] = jnp.zeros_like(st_s)

    def replicate(width):
        rr_ = lax.broadcasted_iota(jnp.int32, (LANES, hg * width), 0)
        cc_ = lax.broadcasted_iota(jnp.int32, (LANES, hg * width), 1)
        return rr_, cc_ // width

    r64, c64 = replicate(hp)
    rep_a = (r64 == c64).astype(BF16)
    rep_dt = (r64 == dt_col0 + hg * grp + c64).astype(BF16)
    r128, c128 = replicate(CHUNK)
    rep_full = (r128 == c128).astype(BF16)
    lane = lax.broadcasted_iota(jnp.int32, (CHUNK, gw), 1)
    first = (lane % (2 * hp)) < hp
    rr = lax.broadcasted_iota(jnp.int32, (CHUNK, CHUNK), 0)
    cc = lax.broadcasted_iota(jnp.int32, (CHUNK, CHUNK), 1)
    causal = rr >= cc
    dt_bias = prm_ref[1:2, :]

    def chunk_body(ci, carry):
        sl = pl.ds(pl.multiple_of(ci * CHUNK, CHUNK), CHUNK)
        bt_c = bt_s[:, sl]
        c_c = c_s[sl, :]
        cb = jnp.dot(c_c, bt_c, preferred_element_type=F32)
        acs_c = acs_ref[sl, :]
        arow = acst_ref[:, sl]
        acrep = _select_cols(acs_c, rep_a)
        acf = _select_cols(acs_c, rep_full)
        xs_c = xs_s[sl, :]
        xdt = xs_c * _select_cols(_softplus(sm_ref[sl, :] + dt_bias), rep_dt)
        eac = jnp.exp(acrep)
        e_last = eac[CHUNK - 1:CHUNK, :]
        xdec = (xdt * jnp.exp(acrep[CHUNK - 1:CHUNK, :] - acrep)).astype(BF16)
        xlo = jnp.where(first, xdt, 0.0).astype(BF16)
        xhi = jnp.where(first, 0.0, xdt).astype(BF16)
        lms = [cb * jnp.where(causal, jnp.exp(jnp.minimum(
            acf[:, j * CHUNK:(j + 1) * CHUNK] - arow[j:j + 1, :], 0.0)), 0.0) for j in range(hg)]
        pairs = range(npair)
        cols = [slice(p * 2 * hp, (p + 1) * 2 * hp) for p in pairs]
        prevs = [st_s[p] for p in pairs]
        y_offs = [_mm(c_c, prev) for prev in prevs]
        y_diags = [jnp.dot(jnp.concatenate([lms[2 * p], lms[2 * p + 1]], axis=1).astype(BF16),
                           jnp.concatenate([xlo[:, cols[p]], xhi[:, cols[p]]], axis=0),
                           preferred_element_type=F32) for p in pairs]
        st_news = [jnp.dot(bt_c, xdec[:, cols[p]], preferred_element_type=F32) for p in pairs]
        for p in pairs:
            st_s[p] = prevs[p] * e_last[:, cols[p]] + st_news[p]
        y = jnp.concatenate([y_diags[p] + y_offs[p] * eac[:, cols[p]] for p in pairs], axis=1)
        y = (y + drep_ref[...] * xs_c) * _silu(z_ref[sl, :].astype(F32))
        o_ref[sl, :] = _rms(y, nw_ref[...]).astype(o_ref.dtype)
        return carry

    lax.fori_loop(0, nchunk, chunk_body, 0)


def _ssd(proj3, small3, conv_w, conv_b, prm, d_rep, norm_w, *, xbc_off, z_off, inner, dt_col0):
    bsz, seq, _ = proj3.shape
    gw = inner // SSM_GROUPS
    hg = gw // SSM_HEAD_DIM
    ns = SSM_STATE
    xb = xbc_off // gw
    bb = (xbc_off + inner) // ns
    cb = bb + SSM_GROUPS
    zb = z_off // gw
    wbb = inner // ns
    wcb = wbb + SSM_GROUPS
    kern = functools.partial(_ssd_kernel, dt_col0=dt_col0, heads_per_group=hg)
    return pl.pallas_call(
        kern,
        grid=(bsz, SSM_GROUPS),
        in_specs=[
            pl.BlockSpec((None, seq, gw), lambda b, g: (b, 0, xb + g)),
            pl.BlockSpec((None, seq, ns), lambda b, g: (b, 0, bb + g)),
            pl.BlockSpec((None, seq, ns), lambda b, g: (b, 0, cb + g)),
            pl.BlockSpec((None, seq, gw), lambda b, g: (b, 0, zb + g)),
            pl.BlockSpec((None, seq, LANES), lambda b, g: (b, 0, 0)),
            pl.BlockSpec((CONV_WIDTH, gw), lambda b, g: (0, g)),
            pl.BlockSpec((CONV_WIDTH, ns), lambda b, g: (0, wbb + g)),
            pl.BlockSpec((CONV_WIDTH, ns), lambda b, g: (0, wcb + g)),
            pl.BlockSpec((1, gw), lambda b, g: (0, g)),
            pl.BlockSpec((1, ns), lambda b, g: (0, wbb + g)),
            pl.BlockSpec((1, ns), lambda b, g: (0, wcb + g)),
            pl.BlockSpec((8, LANES), lambda b, g: (0, 0)),
            pl.BlockSpec((1, gw), lambda b, g: (0, g)),
            pl.BlockSpec((1, gw), lambda b, g: (0, g)),
        ],
        out_specs=pl.BlockSpec((None, seq, gw), lambda b, g: (b, 0, g)),
        out_shape=jax.ShapeDtypeStruct((bsz, seq, inner), BF16),
        scratch_shapes=[
            pltpu.VMEM((seq, LANES), F32),
            pltpu.VMEM((LANES, seq), F32),
            pltpu.VMEM((seq, ns), F32),
            pltpu.VMEM((seq, gw), F32),
            pltpu.VMEM((ns, seq), BF16),
            pltpu.VMEM((seq, ns), BF16),
            pltpu.VMEM((hg // 2, ns, 2 * SSM_HEAD_DIM), F32),
        ],
        compiler_params=_params(("parallel", "parallel")),
        name="ssd",
    )(proj3, proj3, proj3, proj3, small3, conv_w, conv_w, conv_w, conv_b, conv_b, conv_b,
      prm, d_rep, norm_w)


def _out_proj_kernel(a_ref, b_ref, wa_ref, wb_ref, x_ref, o_ref):
    acc = jnp.dot(a_ref[...], wa_ref[...], preferred_element_type=F32)
    acc = acc + jnp.dot(b_ref[...], wb_ref[...], preferred_element_type=F32)
    o_ref[...] = x_ref[...] + acc


def _out_proj(o_a, o_b, w_a, w_b, x2, tm=1024, tn=1024):
    tok, d = x2.shape
    ka = o_a.shape[1]
    kb = o_b.shape[1]
    return pl.pallas_call(
        _out_proj_kernel,
        grid=(tok // tm, d // tn),
        in_specs=[
            pl.BlockSpec((tm, ka), lambda i, j: (i, 0)),
            pl.BlockSpec((tm, kb), lambda i, j: (i, 0)),
            pl.BlockSpec((ka, tn), lambda i, j: (0, j)),
            pl.BlockSpec((kb, tn), lambda i, j: (0, j)),
            pl.BlockSpec((tm, tn), lambda i, j: (i, j)),
        ],
        out_specs=pl.BlockSpec((tm, tn), lambda i, j: (i, j)),
        out_shape=jax.ShapeDtypeStruct((tok, d), F32),
        compiler_params=_params(("parallel", "parallel")),
        name="out_proj",
    )(o_a, o_b, w_a, w_b, x2)


def _router_kernel(h_ref, nw_ref, wr_ref, br_ref, u_ref, idx_ref, gate_ref, rank_ref, cnt_ref, carry_s):
    step = pl.program_id(0)
    tm = h_ref.shape[0]

    @pl.when(step == 0)
    def _():
        carry_s[...] = jnp.zeros_like(carry_s)

    u = _rms(h_ref[...], nw_ref[...])
    _store_row_tiles(u_ref, _pack_bf16_pairs(u))
    logits = _mm_exact(u, wr_ref[...]) + br_ref[...]
    lane = lax.broadcasted_iota(jnp.int32, (tm, LANES), 1).astype(F32)
    work = logits
    vals, idxs = [], []
    for _ in range(TOP_K):
        m = jnp.max(work, axis=1, keepdims=True)
        idx = jnp.min(jnp.where(work == m, lane, float(LANES)), axis=1, keepdims=True)
        vals.append(m)
        idxs.append(idx)
        work = jnp.where(lane == idx, -jnp.inf, work)
    onehot = (work == -jnp.inf).astype(F32)
    exps = [jnp.exp(v - vals[0]) for v in vals]
    denom = exps[0]
    for e in exps[1:]:
        denom = denom + e
    r = lax.broadcasted_iota(jnp.int32, (tm, tm), 0)
    c = lax.broadcasted_iota(jnp.int32, (tm, tm), 1)
    below = (r > c).astype(BF16)
    excl = jnp.dot(below, onehot.astype(BF16), preferred_element_type=F32) + carry_s[...]
    idx_out = jnp.zeros((tm, LANES), F32)
    gate_out = jnp.zeros((tm, LANES), F32)
    rank_out = jnp.zeros((tm, LANES), F32)
    for kk in range(TOP_K):
        rank_k = jnp.sum(jnp.where(lane == idxs[kk], excl, 0.0), axis=1, keepdims=True)
        idx_out = jnp.where(lane == kk, idxs[kk], idx_out)
        gate_out = jnp.where(lane == kk, exps[kk] / denom, gate_out)
        rank_out = jnp.where(lane == kk, rank_k, rank_out)
    idx_ref[...] = idx_out.astype(jnp.int32)
    gate_ref[...] = gate_out
    rank_ref[...] = rank_out.astype(jnp.int32)
    total = carry_s[...] + jnp.sum(onehot, axis=0, keepdims=True)
    carry_s[...] = total
    cnt_ref[...] = jnp.broadcast_to(total, cnt_ref.shape)


def _router(h2d, norm_w, w_r, b_r, tm=512):
    tok, d = h2d.shape
    return pl.pallas_call(
        _router_kernel,
        grid=(tok // tm,),
        in_specs=[
            pl.BlockSpec((tm, d), lambda i: (i, 0)),
            pl.BlockSpec((1, d), lambda i: (0, 0)),
            pl.BlockSpec((d, LANES), lambda i: (0, 0)),
            pl.BlockSpec((1, LANES), lambda i: (0, 0)),
        ],
        out_specs=[
            pl.BlockSpec((tm * ROW_TILE, LANES), lambda i: (i, 0)),
            pl.BlockSpec((tm, LANES), lambda i: (i, 0)),
            pl.BlockSpec((tm, LANES), lambda i: (i, 0)),
            pl.BlockSpec((tm, LANES), lambda i: (i, 0)),
            pl.BlockSpec((8, LANES), lambda i: (0, 0)),
        ],
        out_shape=[
            jax.ShapeDtypeStruct((tok * ROW_TILE, LANES), jnp.uint32),
            jax.ShapeDtypeStruct((tok, LANES), jnp.int32),
            jax.ShapeDtypeStruct((tok, LANES), F32),
            jax.ShapeDtypeStruct((tok, LANES), jnp.int32),
            jax.ShapeDtypeStruct((8, LANES), F32),
        ],
        scratch_shapes=[pltpu.VMEM((1, LANES), F32)],
        compiler_params=_params(("arbitrary",)),
        name="router",
    )(h2d, norm_w, w_r, b_r)


def _dispatch_kernel(pos_ref, u_ref, xs_in, xs_hbm, sem, *, tm):
    del xs_in

    def row_copy(t, k):
        dst = pl.multiple_of(pos_ref[t * TOP_K + k], ROW_TILE)
        return pltpu.make_async_copy(u_ref.at[pl.ds(pl.multiple_of(t * ROW_TILE, ROW_TILE), ROW_TILE)],
                                     xs_hbm.at[pl.ds(dst, ROW_TILE)], sem)

    def issue(t, carry):
        for k in range(TOP_K):
            row_copy(t, k).start(priority=k % 2)
        return carry

    def drain(t, carry):
        for k in range(TOP_K):
            row_copy(t, k).wait()
        return carry

    lax.fori_loop(0, tm, issue, 0)
    lax.fori_loop(0, tm, drain, 0)


def _dispatch(pos_flat, u_tiles, xs_init, tm=256):
    tok = u_tiles.shape[0] // ROW_TILE
    kern = functools.partial(_dispatch_kernel, tm=tm)
    return pl.pallas_call(
        kern,
        grid=(tok // tm,),
        in_specs=[
            pl.BlockSpec((tm * TOP_K,), lambda i: (i,), memory_space=pltpu.SMEM),
            pl.BlockSpec((tm * ROW_TILE, LANES), lambda i: (i, 0)),
            pl.BlockSpec(memory_space=pl.ANY),
        ],
        out_specs=pl.BlockSpec(memory_space=pl.ANY),
        out_shape=jax.ShapeDtypeStruct(xs_init.shape, xs_init.dtype),
        scratch_shapes=[pltpu.SemaphoreType.DMA(())],
        input_output_aliases={2: 0},
        compiler_params=_params(("arbitrary",)),
        name="dispatch",
    )(pos_flat, u_tiles, xs_init)


def _moe_kernel(bexp_ref, nval_ref, x_ref, w1g_ref, w1l_ref, b1g_ref, b1l_ref, w2_ref, b2_ref,
                o_ref, acc_s, xb_s, *, sub):
    i = pl.program_id(0)
    j = pl.program_id(1)
    nf = pl.num_programs(1)
    nval = nval_ref[i]
    bm, d = acc_s.shape

    @pl.when((nval > 0) & (j == 0))
    def _():
        acc_s[...] = jnp.broadcast_to(b2_ref[...], acc_s.shape)

    def process(subs, first_slice):
        wg = w1g_ref[...].astype(BF16)
        wl = w1l_ref[...].astype(BF16)
        w2 = w2_ref[...].astype(BF16)
        rows = [pl.ds(s * sub, sub) for s in subs]
        if first_slice:
            xs = [_unpack_row_tiles(_load_row_tiles(x_ref, s * sub, sub)) for s in subs]
            for r, x in zip(rows, xs):
                xb_s[r, :] = x
        else:
            xs = [xb_s[r, :] for r in rows]
        hgs = [jnp.dot(x, wg, preferred_element_type=F32) + b1g_ref[...] for x in xs]
        hls = [jnp.dot(x, wl, preferred_element_type=F32) + b1l_ref[...] for x in xs]
        acts = []
        for hg, hl in zip(hgs, hls):
            glu = jnp.minimum(hg, SWIGLU_LIMIT)
            lin = jnp.clip(hl, -SWIGLU_LIMIT, SWIGLU_LIMIT)
            acts.append((glu * _sigmoid(SWIGLU_ALPHA * glu) * (lin + 1.0)).astype(BF16))
        for r, act in zip(rows, acts):
            acc_s[r, :] += jnp.dot(act, w2, preferred_element_type=F32)

    for pair in range(bm // (2 * sub)):
        first, second = 2 * pair, 2 * pair + 1

        for first_slice in (True, False):
            on_slice = (j == 0) if first_slice else (j > 0)

            @pl.when((nval > second * sub) & on_slice)
            def _(first=first, second=second, first_slice=first_slice):
                process([first, second], first_slice)

            @pl.when((nval > first * sub) & (nval <= second * sub) & on_slice)
            def _(first=first, first_slice=first_slice):
                process([first], first_slice)

    @pl.when((nval > 0) & (j == nf - 1))
    def _():
        _store_row_tiles(o_ref, _pack_bf16_pairs(acc_s[...]))

    @pl.when((nval == 0) & (j == 0))
    def _():
        o_ref[...] = jnp.zeros_like(o_ref)


def _moe(bexp, nval, xs, w1, b1, w2, b2, *, bm, tf, sub):
    rows = xs.shape[0] // ROW_TILE
    d = w1.shape[1]
    assert d == 2 * ROW_TILE * LANES, "a packed row must fill exactly one (8, 128) tile"
    ff = w2.shape[1]
    nb = rows // bm
    nf = ff // tf
    bt = bm * ROW_TILE

    def jj(i, j, nv):
        return jnp.where(nv[i] > 0, j, nf - 1)

    grid_spec = pltpu.PrefetchScalarGridSpec(
        num_scalar_prefetch=2,
        grid=(nb, nf),
        in_specs=[
            pl.BlockSpec((bt, LANES), lambda i, j, be, nv: (jnp.where(nv[i] > 0, i, 0), 0)),
            pl.BlockSpec((None, d, tf), lambda i, j, be, nv: (be[i], 0, jj(i, j, nv))),
            pl.BlockSpec((None, d, tf), lambda i, j, be, nv: (be[i], 0, nf + jj(i, j, nv))),
            pl.BlockSpec((None, 1, tf), lambda i, j, be, nv: (be[i], 0, jj(i, j, nv))),
            pl.BlockSpec((None, 1, tf), lambda i, j, be, nv: (be[i], 0, nf + jj(i, j, nv))),
            pl.BlockSpec((None, tf, d), lambda i, j, be, nv: (be[i], jj(i, j, nv), 0)),
            pl.BlockSpec((None, 1, d), lambda i, j, be, nv: (be[i], 0, 0)),
        ],
        out_specs=pl.BlockSpec((bt, LANES), lambda i, j, be, nv: (i, 0)),
        scratch_shapes=[
            pltpu.VMEM((bm, d), F32),
            pltpu.VMEM((bm, d), BF16),
        ],
    )
    return pl.pallas_call(
        functools.partial(_moe_kernel, sub=sub),
        grid_spec=grid_spec,
        out_shape=jax.ShapeDtypeStruct((rows * ROW_TILE, LANES), jnp.uint32),
        compiler_params=_params(("arbitrary", "arbitrary"), MOE_VMEM_LIMIT),
        name="moe",
    )(bexp, nval, xs, w1, w1, b1, b1, w2, b2)


def _combine_kernel(pos_ref, gate_ref, h_ref, ys_hbm, p_ref, pn_ref, wg_ref, wp_ref, fn_ref, o_ref,
                    gbuf, sem, *, tm):
    def row_copy(t, k):
        src = pl.multiple_of(pos_ref[t * TOP_K + k], ROW_TILE)
        return pltpu.make_async_copy(
            ys_hbm.at[pl.ds(src, ROW_TILE)],
            gbuf.at[k, pl.ds(pl.multiple_of(t * ROW_TILE, ROW_TILE), ROW_TILE)], sem)

    def issue(t, carry):
        for k in range(TOP_K):
            row_copy(t, k).start(priority=k % 2)
        return carry

    def drain(t, carry):
        for k in range(TOP_K):
            row_copy(t, k).wait()
        return carry

    lax.fori_loop(0, tm, issue, 0)
    lax.fori_loop(0, tm, drain, 0)

    gate = gate_ref[...]
    h = h_ref[...]
    moe = None
    for k in range(TOP_K):
        halves = [_unpack_bf16_pairs(p) for p in _load_row_tiles(gbuf.at[k], 0, tm)]
        row = jnp.concatenate([lo for lo, _ in halves] + [hi for _, hi in halves], axis=1)
        term = gate[:, k:k + 1] * row
        moe = term if moe is None else moe + term
    h = h + moe
    u = _rms(h, pn_ref[...]).astype(BF16)
    gv = _sigmoid(jnp.dot(u, wg_ref[...], preferred_element_type=F32))
    pp = jnp.dot(p_ref[...].astype(BF16), wp_ref[...], preferred_element_type=F32)
    h = h + pp * gv
    o_ref[...] = _rms(h, fn_ref[...])


def _combine(pos_flat, gate, h2d, ys, p2d, ple_norm, w_gate, w_proj, final_norm, tm=256):
    tok, d = h2d.shape
    pd = p2d.shape[1]
    kern = functools.partial(_combine_kernel, tm=tm)
    return pl.pallas_call(
        kern,
        grid=(tok // tm,),
        in_specs=[
            pl.BlockSpec((tm * TOP_K,), lambda i: (i,), memory_space=pltpu.SMEM),
            pl.BlockSpec((tm, LANES), lambda i: (i, 0)),
            pl.BlockSpec((tm, d), lambda i: (i, 0)),
            pl.BlockSpec(memory_space=pl.ANY),
            pl.BlockSpec((tm, pd), lambda i: (i, 0)),
            pl.BlockSpec((1, d), lambda i: (0, 0)),
            pl.BlockSpec((d, d), lambda i: (0, 0)),
            pl.BlockSpec((pd, d), lambda i: (0, 0)),
            pl.BlockSpec((1, d), lambda i: (0, 0)),
        ],
        out_specs=pl.BlockSpec((tm, d), lambda i: (i, 0)),
        out_shape=jax.ShapeDtypeStruct((tok, d), F32),
        scratch_shapes=[pltpu.VMEM((TOP_K, tm * ROW_TILE, LANES), jnp.uint32),
                        pltpu.SemaphoreType.DMA(())],
        compiler_params=_params(("arbitrary",)),
        name="combine",
    )(pos_flat, gate, h2d, ys, p2d, ple_norm, w_gate, w_proj, final_norm)


def _pad_cols(v, off):
    return jnp.pad(v.astype(F32), (off, LANES - off - v.shape[0]))


def _layer(h3, p3, mix_norm, w_in, gdn_conv_w, gdn_a_log, gdn_dt_bias, gdn_norm_w, ssm_conv_w,
           ssm_conv_b, ssm_a_log, ssm_dt_bias, ssm_d, ssm_norm_w, w_out, ffn_norm, w_router, b_router,
           w_mlp1, b_mlp1, w_mlp2, b_mlp2, ple_norm, w_ple_gate, w_ple_proj, final_norm, *, moe_bm, moe_tf, moe_sub):
    bsz, seq, d = h3.shape
    tok = bsz * seq
    n_vh = gdn_a_log.shape[0]
    n_qk = n_vh // 2
    key_dim = n_qk * GDN_HEAD_DIM
    val_dim = n_vh * GDN_HEAD_DIM
    gdn_conv_ch = 2 * key_dim + val_dim
    ssm_heads = ssm_a_log.shape[0]
    inner = ssm_heads * SSM_HEAD_DIM
    ssm_conv_ch = inner + 2 * SSM_GROUPS * SSM_STATE

    o0 = 0
    o_qkv, o0 = o0, o0 + gdn_conv_ch
    o_z, o0 = o0, o0 + val_dim
    o_b, o0 = o0, o0 + n_vh
    o_a, o0 = o0, o0 + n_vh
    o_xbc, o0 = o0, o0 + ssm_conv_ch
    o_sz, o0 = o0, o0 + inner
    o_dt, o0 = o0, o0 + ssm_heads
    w_main = jnp.concatenate([w_in[:, o_qkv:o_b], w_in[:, o_xbc:o_dt]], axis=1).astype(BF16)
    w_small = jnp.concatenate([w_in[:, o_b:o_xbc], w_in[:, o_dt:o0]], axis=1)
    n_small = w_small.shape[1]
    w_small = jnp.pad(w_small, ((0, 0), (0, LANES - n_small))).astype(BF16)
    m_qkv, m_z = 0, gdn_conv_ch
    m_xbc = gdn_conv_ch + val_dim
    m_sz = m_xbc + ssm_conv_ch
    dt_col0 = 2 * n_vh

    x2 = h3.reshape(tok, d)
    proj, small = _in_proj(x2, mix_norm.reshape(1, d), w_main, w_small)
    proj3 = proj.reshape(bsz, seq, proj.shape[1])
    small3 = small.reshape(bsz, seq, LANES)

    zrow = jnp.zeros((LANES,), F32)
    gdn_prm = jnp.stack([_pad_cols(gdn_a_log, n_vh), _pad_cols(gdn_dt_bias, n_vh)] + [zrow] * 6)
    o_gdn = _gdn(proj3, small3, gdn_conv_w, gdn_prm, gdn_norm_w.reshape(1, GDN_HEAD_DIM),
                 qkv_off=m_qkv, z_off=m_z, n_qk=n_qk, out_cols=val_dim)

    ssm_prm = jnp.stack([_pad_cols(ssm_a_log, dt_col0), _pad_cols(ssm_dt_bias, dt_col0)] + [zrow] * 6)
    d_rep = jnp.repeat(ssm_d.astype(F32), SSM_HEAD_DIM).reshape(1, inner)
    o_ssm = _ssd(proj3, small3, ssm_conv_w, ssm_conv_b.reshape(1, ssm_conv_ch), ssm_prm, d_rep,
                 ssm_norm_w.reshape(1, inner), xbc_off=m_xbc, z_off=m_sz, inner=inner, dt_col0=dt_col0)

    w_out_b = w_out.astype(BF16)
    h1 = _out_proj(o_gdn.reshape(tok, val_dim), o_ssm.reshape(tok, inner),
                   w_out_b[:val_dim], w_out_b[val_dim:], x2)

    n_exp = w_router.shape[1]
    w_r = jnp.pad(w_router, ((0, 0), (0, LANES - n_exp)))
    b_r = jnp.pad(b_router.astype(F32), (0, LANES - n_exp), constant_values=NEG_BIG).reshape(1, LANES)
    u2, idx_l, gate_l, rank_l, cnt = _router(h1, ffn_norm.reshape(1, d), w_r, b_r)

    counts = cnt[0, :n_exp].astype(jnp.int32)
    padded = ((counts + moe_bm - 1) // moe_bm) * moe_bm
    pad_end = jnp.cumsum(padded)
    pad_start = pad_end - padded
    top_idx = idx_l[:, :TOP_K]
    pos = ((pad_start[top_idx] + rank_l[:, :TOP_K]) * ROW_TILE).astype(jnp.int32).reshape(-1)
    n_blocks = (tok * TOP_K) // moe_bm + n_exp
    block_start = jnp.arange(n_blocks, dtype=jnp.int32) * moe_bm
    bexp = jnp.minimum(jnp.sum(block_start[:, None] >= pad_end[None, :], axis=1), n_exp - 1).astype(jnp.int32)
    nval = jnp.clip(pad_start[bexp] + counts[bexp] - block_start, 0, moe_bm)
    nval = jnp.where(block_start < pad_end[-1], nval, 0).astype(jnp.int32)

    xs = _dispatch(pos, u2, jnp.zeros((n_blocks * moe_bm * ROW_TILE, LANES), jnp.uint32))
    ff = w_mlp2.shape[1]
    ys = _moe(bexp, nval, xs, w_mlp1, b_mlp1.reshape(n_exp, 1, 2 * ff), w_mlp2,
              b_mlp2.reshape(n_exp, 1, d), bm=moe_bm, tf=moe_tf, sub=moe_sub)

    out = _combine(pos, gate_l, h1, ys, p3.reshape(tok, p3.shape[-1]), ple_norm.reshape(1, d),
                   w_ple_gate.astype(BF16), w_ple_proj.astype(BF16), final_norm.reshape(1, d))
    return out.reshape(bsz, seq, d)


def kernel(x, p, mix_norm, w_in, gdn_conv_w, gdn_a_log, gdn_dt_bias, gdn_norm_w, ssm_conv_w, ssm_conv_b,
           ssm_a_log, ssm_dt_bias, ssm_d, ssm_norm_w, w_out, ffn_norm, w_router, b_router, w_mlp1, b_mlp1,
           w_mlp2, b_mlp2, ple_norm, w_ple_gate, w_ple_proj, final_norm):
    assert mix_norm.shape[0] == 1, "single-layer trunk"
    return _layer(x, p[0], mix_norm[0], w_in[0], gdn_conv_w[0], gdn_a_log[0], gdn_dt_bias[0], gdn_norm_w[0],
                  ssm_conv_w[0], ssm_conv_b[0], ssm_a_log[0], ssm_dt_bias[0], ssm_d[0], ssm_norm_w[0],
                  w_out[0], ffn_norm[0], w_router[0], b_router[0], w_mlp1[0], b_mlp1[0], w_mlp2[0],
                  b_mlp2[0], ple_norm[0], w_ple_gate[0], w_ple_proj[0], final_norm, moe_bm=1024, moe_tf=512, moe_sub=256)
```
